```python
import math
import jax
import jax.numpy as jnp
from jax import lax
import numpy as np

D_MODEL = 1024
BATCH = 4
SEQ = 4096
DEPTH = 2

GRID_W = 64
CTX_LEN = 256
EPS = 1e-6
NEG_INF = -1e30
D_A = 512
CONV_A = 31
H_B = 8
DH_B = 64
D_B = H_B * DH_B
NA_ROWS = 8
NA_COLS = 16
H_C = 4
DK_C = 128
DV_C = 128
D_CQK = H_C * DK_C
D_CV = H_C * DV_C
SHORT_CONV = 4
SC_PAD_L = SHORT_CONV // 2
SC_PAD_R = SHORT_CONV - 1 - SC_PAD_L
CHUNK = 64
ROPE_BASE = 10000.0
N_BRANCH = 3
D_FF = 2816
FFN_CONV = 3
N_IN = 2 * D_A + 3 * D_B + 2 * D_CQK + 2 * D_CV + 4 * H_C + N_BRANCH * D_MODEL

kernel_name = "hybrid_conv_na_gdn_flow_block"


def _rms_norm(x, g):
    xf = x.astype(jnp.float32)
    y = xf * lax.rsqrt(jnp.mean(xf * xf, axis=-1, keepdims=True) + EPS)
    return (y * g.astype(jnp.float32)).astype(x.dtype)


def _layer_norm(x, g, b):
    xf = x.astype(jnp.float32)
    mu = jnp.mean(xf, axis=-1, keepdims=True)
    var = jnp.mean(jnp.square(xf - mu), axis=-1, keepdims=True)
    y = (xf - mu) * lax.rsqrt(var + EPS) * g.astype(jnp.float32) + b.astype(jnp.float32)
    return y.astype(x.dtype)


def _l2norm(x):
    return x * lax.rsqrt(jnp.sum(x * x, axis=-1, keepdims=True) + EPS)


def _modulate(h, shift, scale):
    return h * (1 + scale[:, None, :]) + shift[:, None, :]


def _dwconv(x, w, pad_l, pad_r):
    return lax.conv_general_dilated(
        x, w[:, None, :].astype(x.dtype), window_strides=(1,), padding=[(pad_l, pad_r)],
        dimension_numbers=("NWC", "WIO", "NWC"), feature_group_count=x.shape[-1])


def _split_in(p):
    sizes = [2 * D_A, 3 * D_B, 2 * D_CQK + D_CV, D_CV, 2 * H_C, 2 * H_C]
    return jnp.split(p, np.cumsum(sizes).tolist(), axis=-1)


def _axial_rope_tables(n_tok, head_dim):
    t = jnp.arange(n_tok)
    row = (t // GRID_W).astype(jnp.float32)
    col = (t % GRID_W).astype(jnp.float32)
    n_freq = head_dim // 4
    inv = jnp.power(ROPE_BASE, -jnp.arange(n_freq, dtype=jnp.float32) / n_freq)
    ang_r = row[:, None, None] * inv
    ang_c = col[:, None, None] * inv
    return (jnp.cos(ang_r), jnp.sin(ang_r), jnp.cos(ang_c), jnp.sin(ang_c))


def _rotate(x, cos, sin):
    x1, x2 = jnp.split(x, 2, axis=-1)
    return jnp.concatenate([x1 * cos - x2 * sin, x1 * sin + x2 * cos], axis=-1)


def _axial_rope(x, rope):
    cr, sr, cc, sc = rope
    xr, xc = jnp.split(x, 2, axis=-1)
    return jnp.concatenate([_rotate(xr, cr, sr), _rotate(xc, cc, sc)], axis=-1)


def _conformer_conv(u, conv_w, conv_b, ln_g, ln_b):
    a, gate = jnp.split(u, 2, axis=-1)
    y = a * jax.nn.sigmoid(gate)
    y = _dwconv(y, conv_w, CONV_A // 2, CONV_A // 2) + conv_b
    y = _layer_norm(y, ln_g, ln_b)
    return jax.nn.silu(y)


def _na_qkv(p, qn_g, kn_g):
    b, n, _ = p.shape
    q, k, v = jnp.split(p, 3, axis=-1)
    q = _rms_norm(q.reshape(b, n, H_B, DH_B), qn_g)
    k = _rms_norm(k.reshape(b, n, H_B, DH_B), kn_g)
    return q, k, v.reshape(b, n, H_B, DH_B)


def _neighbourhood_attention(q, k, v, k_ctx, v_ctx, rpb, rows):
    b = q.shape[0]
    kr = min(NA_ROWS, rows)
    qg = q.reshape(b, rows, GRID_W, H_B, DH_B)
    kg = k.reshape(b, rows, GRID_W, H_B, DH_B)
    vg = v.reshape(b, rows, GRID_W, H_B, DH_B)
    r = np.arange(rows)
    row_idx = np.clip(r - kr // 2, 0, rows - kr)[:, None] + np.arange(kr)
    cidx = np.arange(GRID_W)
    cs = np.clip(cidx - NA_COLS // 2, 0, GRID_W - NA_COLS)
    col_ok = (cidx[None, :] >= cs[:, None]) & (cidx[None, :] < cs[:, None] + NA_COLS)
    dr = row_idx - r[:, None] + NA_ROWS - 1
    dc = np.clip(cidx[None, :] - cidx[:, None] + NA_COLS - 1, 0, 2 * NA_COLS - 2)
    bias = rpb[:, dr[:, None, :, None], dc[None, :, None, :]]
    kb = kg[:, row_idx]
    vb = vg[:, row_idx]
    scale = DH_B ** -0.5
    s_loc = jnp.einsum("brqhd,brkwhd->bhrqkw", qg, kb, preferred_element_type=jnp.float32) * scale + bias[None]
    s_loc = jnp.where(col_ok[:, None, :], s_loc, NEG_INF)
    s_ctx = jnp.einsum("brqhd,bchd->bhrqc", qg, k_ctx, preferred_element_type=jnp.float32) * scale
    n_loc = kr * GRID_W
    s = jnp.concatenate([s_loc.reshape(b, H_B, rows, GRID_W, n_loc), s_ctx], axis=-1)
    p = jax.nn.softmax(s, axis=-1).astype(v.dtype)
    p_loc = p[..., :n_loc].reshape(b, H_B, rows, GRID_W, kr, GRID_W)
    o = jnp.einsum("bhrqkw,brkwhd->brqhd", p_loc, vb) + jnp.einsum("bhrqc,bchd->brqhd", p[..., n_loc:], v_ctx)
    return o.reshape(b, rows * GRID_W, D_B)


def _context_attention(q, k, v):
    s = jnp.einsum("bqhd,bkhd->bhqk", q, k, preferred_element_type=jnp.float32) * DH_B ** -0.5
    p = jax.nn.softmax(s, axis=-1).astype(v.dtype)
    o = jnp.einsum("bhqk,bkhd->bqhd", p, v)
    return o.reshape(q.shape[0], q.shape[1], D_B)


def _chunk_gated_delta(q, k, v, g, beta, s0):
    b, h, t, dk = k.shape
    dv = v.shape[-1]
    n = t // CHUNK
    q = q.reshape(b, h, n, CHUNK, dk)
    k = k.reshape(b, h, n, CHUNK, dk)
    v = v.reshape(b, h, n, CHUNK, dv)
    g = jnp.cumsum(g.reshape(b, h, n, CHUNK), axis=-1)
    beta = beta.reshape(b, h, n, CHUNK)[..., None]
    incl = jnp.tril(jnp.ones((CHUNK, CHUNK), dtype=bool))
    strict = jnp.tril(jnp.ones((CHUNK, CHUNK), dtype=bool), -1)
    diff = g[..., :, None] - g[..., None, :]
    decay = jnp.where(incl, jnp.exp(jnp.where(incl, diff, 0.0)), 0.0)
    kb = k * beta
    lmat = jnp.where(strict, jnp.einsum("bhnid,bhnjd->bhnij", kb, k) * decay, 0.0)
    rhs = jnp.concatenate([v * beta, kb * jnp.exp(g)[..., None]], axis=-1)
    sol = lax.linalg.triangular_solve(lmat, rhs, left_side=True, lower=True, unit_diagonal=True)
    u, w = sol[..., :dv], sol[..., dv:]
    a_intra = jnp.where(incl, jnp.einsum("bhnid,bhnjd->bhnij", q, k) * decay, 0.0)
    q_dec = q * jnp.exp(g)[..., None]
    k_dec = k * jnp.exp(g[..., -1:] - g)[..., None]
    g_last = jnp.exp(g[..., -1])

    def step(s, xs):
        qd, kd, ui, wi, ai, gl = xs
        v_new = ui - jnp.einsum("bhcd,bhde->bhce", wi, s)
        o = jnp.einsum("bhcd,bhde->bhce", qd, s) + jnp.einsum("bhij,bhje->bhie", ai, v_new)
        s = s * gl[..., None, None] + jnp.einsum("bhcd,bhce->bhde", kd, v_new)
        return s, o

    xs = tuple(jnp.moveaxis(a, 2, 0) for a in (q_dec, k_dec, u, w, a_intra, g_last))
    s, o = lax.scan(step, s0, xs)
    return jnp.moveaxis(o, 0, 2).reshape(b, h, t, dv), s


def _delta_scan(q, k, v, g, beta, s0, reverse):
    if reverse:
        q, k, v, g, beta = (jnp.flip(a, axis=2) for a in (q, k, v, g, beta))
    o, s = _chunk_gated_delta(q, k, v, g, beta, s0)
    if reverse:
        o = jnp.flip(o, axis=2)
    return o, s


def _gdn_prepare(qkv, dec, bet, conv_w, a_log, dt_bias, rope):
    b, t, _ = qkv.shape
    y = jax.nn.silu(_dwconv(qkv, conv_w, SC_PAD_L, SC_PAD_R)).astype(jnp.float32)
    q, k, v = jnp.split(y, [D_CQK, 2 * D_CQK], axis=-1)
    q = _l2norm(q.reshape(b, t, H_C, DK_C))
    k = _l2norm(k.reshape(b, t, H_C, DK_C))
    if rope is not None:
        q = _axial_rope(q, rope)
        k = _axial_rope(k, rope)
    q = q * DK_C ** -0.5
    v = v.reshape(b, t, H_C, DV_C)
    g = -jnp.exp(a_log.astype(jnp.float32)) * jax.nn.softplus(
        dec.astype(jnp.float32).reshape(b, t, 2, H_C) + dt_bias.astype(jnp.float32))
    beta = jax.nn.sigmoid(bet.astype(jnp.float32).reshape(b, t, 2, H_C))
    g = jnp.transpose(g, (2, 0, 3, 1))
    beta = jnp.transpose(beta, (2, 0, 3, 1))
    return (jnp.swapaxes(q, 1, 2), jnp.swapaxes(k, 1, 2), jnp.swapaxes(v, 1, 2), g, beta)


def _gdn_out(o, gate, gain):
    b, t, _ = gate.shape
    o = jnp.swapaxes(o, 1, 2)
    o = o * lax.rsqrt(jnp.mean(o * o, axis=-1, keepdims=True) + EPS) * gain.astype(jnp.float32)
    y = o * jax.nn.silu(gate.astype(jnp.float32).reshape(b, t, H_C, DV_C))
    return y.reshape(b, t, D_CV).astype(gate.dtype)


def _gated_deltanet(ctx_in, lat_in, gate_ctx, gate_lat, conv_w, a_log, dt_bias, gain, rope, ctx_out):
    qc, kc, vc, gc, bc = _gdn_prepare(*ctx_in, conv_w, a_log, dt_bias, None)
    ql, kl, vl, gl, bl = _gdn_prepare(*lat_in, conv_w, a_log, dt_bias, rope)
    s0 = jnp.zeros(kc.shape[:2] + (DK_C, DV_C), jnp.float32)
    o_lat = jnp.zeros_like(vl)
    o_ctx = jnp.zeros_like(vc)
    for d in range(2):
        rev = d == 1
        oc, s_ctx = _delta_scan(qc, kc, vc, gc[d], bc[d], s0, rev)
        ol, _ = _delta_scan(ql, kl, vl, gl[d], bl[d], s_ctx, rev)
        o_lat = o_lat + ol
        o_ctx = o_ctx + oc
    y_lat = _gdn_out(o_lat, gate_lat, gain)
    y_ctx = _gdn_out(o_ctx, gate_ctx, gain) if ctx_out else None
    return y_lat, y_ctx


def _merge(gate_logits, ya, yb, yc, w_br, w_o):
    ga, gb, gc = jnp.split(jax.nn.sigmoid(gate_logits), N_BRANCH, axis=-1)
    z = ga * (ya @ w_br[0]) + gb * (yb @ w_br[1]) + gc * (yc @ w_br[2])
    return z @ w_o


def _conv_ffn(h, w_up, conv_w, conv_b, w_down):
    u = _dwconv(h @ w_up, conv_w, FFN_CONV // 2, FFN_CONV // 2) + conv_b
    gate, val = jnp.split(u, 2, axis=-1)
    return (jax.nn.silu(gate) * val) @ w_down


def setup_inputs(seed: int = 0) -> dict:
    key = jax.random.key(seed)
    ks = jax.random.split(key, 32)
    f32 = jnp.float32

    def nrm(k, shape, scale):
        return jax.random.normal(k, shape, f32) * scale

    def gain(k, shape):
        return 1.0 + 0.02 * jax.random.normal(k, shape, f32)

    a_init = jax.random.uniform(ks[17], (DEPTH, 2, H_C), f32, 1.0, 16.0)
    dt = jnp.exp(jax.random.uniform(ks[18], (DEPTH, 2, H_C), f32, math.log(1e-3), math.log(1e-1)))
    return {
        "x": nrm(ks[0], (BATCH, SEQ, D_MODEL), 1.0),
        "c": nrm(ks[1], (BATCH, D_MODEL), 1.0),
        "ctx": nrm(ks[2], (BATCH, CTX_LEN, D_MODEL), 1.0),
        "c_ctx": nrm(ks[3], (D_MODEL,), 1.0),
        "ada_w": nrm(ks[4], (DEPTH, D_MODEL, 6 * D_MODEL), 0.5 * D_MODEL ** -0.5),
        "ada_b": nrm(ks[5], (DEPTH, 6 * D_MODEL), 0.01),
        "norm1_g": gain(ks[6], (DEPTH, D_MODEL)),
        "norm2_g": gain(ks[7], (DEPTH, D_MODEL)),
        "w_in": nrm(ks[8], (DEPTH, D_MODEL, N_IN), D_MODEL ** -0.5),
        "conv_a_w": nrm(ks[9], (DEPTH, CONV_A, D_A), CONV_A ** -0.5),
        "conv_a_b": nrm(ks[10], (DEPTH, D_A), 0.01),
        "ln_a_g": gain(ks[11], (DEPTH, D_A)),
        "ln_a_b": nrm(ks[12], (DEPTH, D_A), 0.01),
        "qn_g": gain(ks[13], (DEPTH, DH_B)),
        "kn_g": gain(ks[14], (DEPTH, DH_B)),
        "rpb": nrm(ks[15], (DEPTH, H_B, 2 * NA_ROWS - 1, 2 * NA_COLS - 1), 0.1),
        "conv_c_w": nrm(ks[16], (DEPTH, SHORT_CONV, 2 * D_CQK + D_CV), SHORT_CONV ** -0.5),
        "a_log": jnp.log(a_init),
        "dt_bias": dt + jnp.log(-jnp.expm1(-dt)),
        "onorm_g": gain(ks[19], (DEPTH, DV_C)),
        "w_branch": nrm(ks[20], (DEPTH, N_BRANCH, D_A, D_MODEL), D_A ** -0.5),
        "w_out": nrm(ks[21], (DEPTH, D_MODEL, D_MODEL), D_MODEL ** -0.5),
        "ffn_up": nrm(ks[22], (DEPTH, D_MODEL, 2 * D_FF), D_MODEL ** -0.5),
        "ffn_conv_w": nrm(ks[23], (DEPTH, FFN_CONV, 2 * D_FF), FFN_CONV ** -0.5),
        "ffn_conv_b": nrm(ks[24], (DEPTH, 2 * D_FF), 0.01),
        "ffn_down": nrm(ks[25], (DEPTH, D_FF, D_MODEL), D_FF ** -0.5),
    }


def reference(x, c, ctx, c_ctx, ada_w, ada_b, norm1_g, norm2_g, w_in, conv_a_w, conv_a_b, ln_a_g, ln_a_b,
              qn_g, kn_g, rpb, conv_c_w, a_log, dt_bias, onorm_g, w_branch, w_out, ffn_up, ffn_conv_w,
              ffn_conv_b, ffn_down):
    n_lat = x.shape[1]
    n_ctx = ctx.shape[1]
    rows = n_lat // GRID_W
    rope = _axial_rope_tables(n_lat, DK_C)
    silu_c = jax.nn.silu(c)
    silu_cc = jax.nn.silu(c_ctx)[None, :]
    x_lat, x_ctx = x, ctx
    for l in range(DEPTH):
        ctx_out = l < DEPTH - 1
        m_lat = jnp.split(silu_c @ ada_w[l] + ada_b[l], 6, axis=-1)
        m_ctx = jnp.split(silu_cc @ ada_w[l] + ada_b[l], 6, axis=-1)
        h_lat = _modulate(_rms_norm(x_lat, norm1_g[l]), m_lat[0], m_lat[1])
        h_ctx = _modulate(_rms_norm(x_ctx, norm1_g[l]), m_ctx[0], m_ctx[1])
        proj = jnp.concatenate([h_ctx, h_lat], axis=1) @ w_in[l]
        a_c, na_c, qkv_c, og_c, dec_c, bet_c, mg_c = _split_in(proj[:, :n_ctx])
        a_l, na_l, qkv_l, og_l, dec_l, bet_l, mg_l = _split_in(proj[:, n_ctx:])
        ya_l = _conformer_conv(a_l, conv_a_w[l], conv_a_b[l], ln_a_g[l], ln_a_b[l])
        qb_l, kb_l, vb_l = _na_qkv(na_l, qn_g[l], kn_g[l])
        qb_c, kb_c, vb_c = _na_qkv(na_c, qn_g[l], kn_g[l])
        yb_l = _neighbourhood_attention(qb_l, kb_l, vb_l, kb_c, vb_c, rpb[l], rows)
        yc_l, yc_c = _gated_deltanet((qkv_c, dec_c, bet_c), (qkv_l, dec_l, bet_l), og_c, og_l, conv_c_w[l],
                                     a_log[l], dt_bias[l], onorm_g[l], rope, ctx_out)
        x_lat = x_lat + m_lat[2][:, None, :] * _merge(mg_l, ya_l, yb_l, yc_l, w_branch[l], w_out[l])
        h2_lat = _modulate(_rms_norm(x_lat, norm2_g[l]), m_lat[3], m_lat[4])
        x_lat = x_lat + m_lat[5][:, None, :] * _conv_ffn(h2_lat, ffn_up[l], ffn_conv_w[l], ffn_conv_b[l], ffn_down[l])
        if ctx_out:
            ya_c = _conformer_conv(a_c, conv_a_w[l], conv_a_b[l], ln_a_g[l], ln_a_b[l])
            yb_c = _context_attention(qb_c, kb_c, vb_c)
            x_ctx = x_ctx + m_ctx[2][:, None, :] * _merge(mg_c, ya_c, yb_c, yc_c, w_branch[l], w_out[l])
            h2_ctx = _modulate(_rms_norm(x_ctx, norm2_g[l]), m_ctx[3], m_ctx[4])
            x_ctx = x_ctx + m_ctx[5][:, None, :] * _conv_ffn(h2_ctx, ffn_up[l], ffn_conv_w[l], ffn_conv_b[l], ffn_down[l])
    return x_lat
```

```python
import functools
import math

import numpy as np
import jax
import jax.numpy as jnp
from jax import lax
from jax.experimental import pallas as pl
from jax.experimental.pallas import tpu as pltpu

F32 = jnp.float32
BF16 = jnp.bfloat16

D_MODEL = 1024
GRID_W = 64
EPS = 1e-6
NEG_INF = -1e30
D_A = 512
CONV_A = 31
H_B = 8
DH_B = 64
D_B = H_B * DH_B
NA_ROWS = 8
NA_COLS = 16
H_C = 4
DK_C = 128
DV_C = 128
D_CQK = H_C * DK_C
D_CV = H_C * DV_C
SHORT_CONV = 4
CHUNK = 64
ROPE_BASE = 10000.0
D_FF = 2816
FFN_CONV = 3

VMEM_LIMIT_BYTES = 48 * 1024 * 1024
HALO = 16
GDN_BLOCK = 256
GDN_INV_PASSES = 3
GDN_SOL_PASSES = 2


def _cparams(*sem):
    return pltpu.CompilerParams(dimension_semantics=sem, vmem_limit_bytes=VMEM_LIMIT_BYTES)


def _dot(a, b):
    return jnp.dot(a, b, preferred_element_type=F32)


def _dot_nt(a, b):
    return lax.dot_general(a, b, (((1,), (1,)), ((), ())), preferred_element_type=F32)


def _split_bf16(x, n):
    parts = []
    r = x
    for idx in range(n):
        p = r.astype(BF16)
        parts.append(p)
        if idx + 1 < n:
            r = r - p.astype(F32)
    return parts


def _dot_exact_rhs(a, b_bf16, n):
    out = None
    for p in _split_bf16(a, n):
        t = _dot(p, b_bf16)
        out = t if out is None else out + t
    return out


def _dot_exact_lhs(a_bf16, b, n):
    out = None
    for p in _split_bf16(b, n):
        t = _dot(a_bf16, p)
        out = t if out is None else out + t
    return out


def _mm(a, b, passes):
    if passes == 1:
        return _dot(a.astype(BF16), b.astype(BF16))
    a_hi, a_lo = _split_bf16(a, 2)
    b_hi, b_lo = _split_bf16(b, 2)
    return _dot(a_hi, b_hi) + (_dot(a_lo, b_hi) + _dot(a_hi, b_lo))


def _sigmoid(x):
    return 1.0 / (1.0 + jnp.exp(-x))


def _silu(x):
    return x * _sigmoid(x)


def _softplus(x):
    return jnp.maximum(x, 0.0) + jnp.log(1.0 + jnp.exp(-jnp.abs(x)))


def _norm_mod(x, g, shift, scale):
    ms = jnp.mean(x * x, axis=-1, keepdims=True)
    y = x * lax.rsqrt(ms + EPS) * g
    return y * (1.0 + scale) + shift


def _halo_specs(tm, n_tok, width):
    per = tm // HALO
    nblk = n_tok // HALO
    prev = pl.BlockSpec((None, HALO, width), lambda b, i, *_: (b, jnp.maximum(i * per - 1, 0), 0))
    nxt = pl.BlockSpec((None, HALO, width), lambda b, i, *_: (b, jnp.minimum((i + 1) * per, nblk - 1), 0))
    return prev, nxt


def _ada_kernel(c_ref, w_ref, b_ref, o_ref):
    s = _silu(c_ref[...])
    o_ref[...] = _mm(s, w_ref[...], 3) + b_ref[...]


def _ada(cc, ada_w, ada_b):
    depth = ada_w.shape[0]
    d = D_MODEL
    return pl.pallas_call(
        _ada_kernel,
        name="ada_mod",
        grid=(depth, 6),
        in_specs=[
            pl.BlockSpec((8, d), lambda l, j: (0, 0)),
            pl.BlockSpec((None, d, d), lambda l, j: (l, 0, j)),
            pl.BlockSpec((None, 1, d), lambda l, j: (l, 0, j)),
        ],
        out_specs=pl.BlockSpec((None, 8, d), lambda l, j: (l, 0, j)),
        out_shape=jax.ShapeDtypeStruct((depth, 8, 6 * d), F32),
        compiler_params=_cparams("arbitrary", "arbitrary"),
    )(cc, ada_w, ada_b.reshape(depth, 1, 6 * d))


def _proj_a_kernel(x_ref, g_ref, sh_ref, sc_ref, wa_ref, wg_ref, o_ref):
    h = _norm_mod(x_ref[...], g_ref[...], sh_ref[...], sc_ref[...]).astype(BF16)
    a = _dot(h, wa_ref[...])
    gate = _dot(h, wg_ref[...])
    o_ref[...] = a * _sigmoid(gate)


def _proj_a(x, g, shift, scale, wa, wg, tm):
    b, n, d = x.shape
    row = pl.BlockSpec((None, 1, d), lambda bb, i: (bb, 0, 0))
    return pl.pallas_call(
        _proj_a_kernel,
        name="proj_a",
        grid=(b, n // tm),
        in_specs=[
            pl.BlockSpec((None, tm, d), lambda bb, i: (bb, i, 0)),
            pl.BlockSpec((1, d), lambda bb, i: (0, 0)),
            row, row,
            pl.BlockSpec((d, D_A), lambda bb, i: (0, 0)),
            pl.BlockSpec((d, D_A), lambda bb, i: (0, 0)),
        ],
        out_specs=pl.BlockSpec((None, tm, D_A), lambda bb, i: (bb, i, 0)),
        out_shape=jax.ShapeDtypeStruct((b, n, D_A), F32),
        compiler_params=_cparams("parallel", "parallel"),
    )(x, g, shift, scale, wa, wg)


def _conv_a_kernel(yp_ref, y_ref, yn_ref, cw_ref, cb_ref, lg_ref, lb_ref, o_ref, ext_ref, *, tm):
    i = pl.program_id(1)
    last = pl.num_programs(1) - 1
    ext_ref[0:HALO] = yp_ref[...] * (i > 0).astype(F32)
    ext_ref[HALO:HALO + tm] = y_ref[...]
    ext_ref[HALO + tm:2 * HALO + tm] = yn_ref[...] * (i < last).astype(F32)
    pad = CONV_A // 2
    acc = jnp.zeros((tm, D_A), F32) + cb_ref[...]
    for k in range(CONV_A):
        acc = acc + ext_ref[pl.ds(HALO - pad + k, tm), :] * cw_ref[k:k + 1, :]
    mu = jnp.mean(acc, axis=-1, keepdims=True)
    cen = acc - mu
    var = jnp.mean(cen * cen, axis=-1, keepdims=True)
    y = cen * lax.rsqrt(var + EPS) * lg_ref[...] + lb_ref[...]
    o_ref[...] = _silu(y).astype(BF16)


def _conv_a(y, cw, cb, lg, lb, tm):
    b, n, c = y.shape
    prev, nxt = _halo_specs(tm, n, c)
    vec = pl.BlockSpec((1, c), lambda bb, i: (0, 0))
    return pl.pallas_call(
        functools.partial(_conv_a_kernel, tm=tm),
        name="conv_a",
        grid=(b, n // tm),
        in_specs=[prev, pl.BlockSpec((None, tm, c), lambda bb, i: (bb, i, 0)), nxt,
                  pl.BlockSpec((CONV_A, c), lambda bb, i: (0, 0)), vec, vec, vec],
        out_specs=pl.BlockSpec((None, tm, c), lambda bb, i: (bb, i, 0)),
        out_shape=jax.ShapeDtypeStruct((b, n, c), BF16),
        scratch_shapes=[pltpu.VMEM((tm + 2 * HALO, c), F32)],
        compiler_params=_cparams("parallel", "parallel"),
    )(y, y, y, cw, cb.reshape(1, c), lg.reshape(1, c), lb.reshape(1, c))


def _proj_b_kernel(x_ref, g_ref, sh_ref, sc_ref, w_ref, gq_ref, gk_ref, gm_ref, q_ref, k_ref, v_ref):
    h = _norm_mod(x_ref[...], g_ref[...], sh_ref[...], sc_ref[...]).astype(BF16)
    for idx, (gain_ref, o_ref) in enumerate(((gq_ref, q_ref), (gk_ref, k_ref), (None, v_ref))):
        acc = _dot(h, w_ref[:, idx * D_B:(idx + 1) * D_B])
        if gain_ref is not None:
            ss = _dot_exact_rhs(acc * acc, gm_ref[...], 2)
            acc = acc * lax.rsqrt(ss * (1.0 / DH_B) + EPS) * gain_ref[...]
        for hh in range(H_B):
            o_ref[hh] = acc[:, hh * DH_B:(hh + 1) * DH_B].astype(BF16)


def _proj_b(x, g, shift, scale, w, gq, gk, gmat, tm):
    b, n, d = x.shape
    row = pl.BlockSpec((None, 1, d), lambda bb, i: (bb, 0, 0))
    vec = pl.BlockSpec((1, D_B), lambda bb, i: (0, 0))
    head_out = pl.BlockSpec((None, H_B, tm, DH_B), lambda bb, i: (bb, 0, i, 0))
    shp = jax.ShapeDtypeStruct((b, H_B, n, DH_B), BF16)
    return pl.pallas_call(
        _proj_b_kernel,
        name="proj_b",
        grid=(b, n // tm),
        in_specs=[
            pl.BlockSpec((None, tm, d), lambda bb, i: (bb, i, 0)),
            pl.BlockSpec((1, d), lambda bb, i: (0, 0)),
            row, row,
            pl.BlockSpec((d, 3 * D_B), lambda bb, i: (0, 0)),
            vec, vec,
            pl.BlockSpec((D_B, D_B), lambda bb, i: (0, 0)),
        ],
        out_specs=[head_out, head_out, head_out],
        out_shape=[shp, shp, shp],
        compiler_params=_cparams("parallel", "parallel"),
    )(x, g, shift, scale, w, gq, gk, gmat)


def _na_kernel(q_ref, k_ref, v_ref, kc_ref, vc_ref, bias_ref, o_ref, *, rb, rows):
    i = pl.program_id(2)
    n_loc = NA_ROWS * GRID_W
    qcol = lax.broadcasted_iota(jnp.int32, (GRID_W, n_loc), 0)
    kcol = lax.broadcasted_iota(jnp.int32, (GRID_W, n_loc), 1) & (GRID_W - 1)
    cs = jnp.clip(qcol - NA_COLS // 2, 0, GRID_W - NA_COLS)
    col_ok = jnp.abs(2 * (kcol - cs) - (NA_COLS - 1)) <= NA_COLS - 1
    for rr in range(rb):
        r = i * rb + rr
        start = jnp.clip(r - NA_ROWS // 2, 0, rows - NA_ROWS)
        variant = start - r + NA_ROWS - 1
        q = q_ref[rr * GRID_W:(rr + 1) * GRID_W, :]
        tok0 = pl.multiple_of(start * GRID_W, GRID_W)
        kb = k_ref[pl.ds(tok0, n_loc), :]
        vb = v_ref[pl.ds(tok0, n_loc), :]
        s = _dot_nt(q, kb) + bias_ref[variant]
        s = jnp.where(col_ok, s, NEG_INF)
        sc = _dot_nt(q, kc_ref[...])
        m = jnp.maximum(jnp.max(s, axis=-1, keepdims=True), jnp.max(sc, axis=-1, keepdims=True))
        p = jnp.exp(s - m)
        pc = jnp.exp(sc - m)
        l = jnp.sum(p, axis=-1, keepdims=True) + jnp.sum(pc, axis=-1, keepdims=True)
        o = _dot(p.astype(BF16), vb) + _dot(pc.astype(BF16), vc_ref[...])
        o_ref[rr * GRID_W:(rr + 1) * GRID_W, :] = (o / l).astype(BF16)


def _na_attention(q, k, v, kc, vc, bias, rb):
    b, h, t, dh = q.shape
    n_ctx = kc.shape[2]
    rows = t // GRID_W
    full = pl.BlockSpec((None, None, t, dh), lambda bb, hh, i: (bb, hh, 0, 0))
    cfull = pl.BlockSpec((None, None, n_ctx, dh), lambda bb, hh, i: (bb, hh, 0, 0))
    tile = pl.BlockSpec((None, None, rb * GRID_W, dh), lambda bb, hh, i: (bb, hh, i, 0))
    return pl.pallas_call(
        functools.partial(_na_kernel, rb=rb, rows=rows),
        name="na_attn",
        grid=(b, h, rows // rb),
        in_specs=[tile, full, full, cfull, cfull,
                  pl.BlockSpec((None, NA_ROWS, GRID_W, NA_ROWS * GRID_W), lambda bb, hh, i: (hh, 0, 0, 0))],
        out_specs=tile,
        out_shape=jax.ShapeDtypeStruct((b, h, t, dh), BF16),
        compiler_params=_cparams("parallel", "parallel", "arbitrary"),
    )(q, k, v, kc, vc, bias)


def _ctx_attn_kernel(q_ref, k_ref, v_ref, o_ref):
    s = _dot_nt(q_ref[...], k_ref[...])
    m = jnp.max(s, axis=-1, keepdims=True)
    p = jnp.exp(s - m)
    l = jnp.sum(p, axis=-1, keepdims=True)
    o_ref[...] = (_dot(p.astype(BF16), v_ref[...]) / l).astype(BF16)


def _ctx_attention(q, k, v):
    b, h, n, dh = q.shape
    full = pl.BlockSpec((None, None, n, dh), lambda bb, hh: (bb, hh, 0, 0))
    return pl.pallas_call(
        _ctx_attn_kernel,
        name="ctx_attn",
        grid=(b, h),
        in_specs=[full, full, full],
        out_specs=full,
        out_shape=jax.ShapeDtypeStruct((b, h, n, dh), BF16),
        compiler_params=_cparams("parallel", "parallel"),
    )(q, k, v)


def _na_bias_table(rpb):
    var = np.arange(NA_ROWS)
    kk = np.arange(NA_ROWS)
    dr = var[:, None] + kk[None, :]
    cidx = np.arange(GRID_W)
    dc = np.clip(cidx[None, :] - cidx[:, None] + NA_COLS - 1, 0, 2 * NA_COLS - 2)
    tab = rpb[:, dr[:, None, :, None], dc[None, :, None, :]]
    return tab.reshape(H_B, NA_ROWS, GRID_W, NA_ROWS * GRID_W)


def _proj_c_kernel(xp_ref, x_ref, xn_ref, g_ref, sh_ref, sc_ref, w_ref, wdb_ref, cw_ref, rc_ref, rs_ref,
                   gm_ref, gbp_ref, q_ref, k_ref, v_ref, og_ref, gb_ref, hs_ref, p_ref, *, tm, use_rope):
    i = pl.program_id(1)
    last = pl.num_programs(1) - 1
    g, sh, sc = g_ref[...], sh_ref[...], sc_ref[...]
    hs_ref[0:HALO] = (_norm_mod(xp_ref[...], g, sh, sc) * (i > 0).astype(F32)).astype(BF16)
    hs_ref[HALO:HALO + tm] = _norm_mod(x_ref[...], g, sh, sc).astype(BF16)
    hs_ref[HALO + tm:2 * HALO + tm] = (_norm_mod(xn_ref[...], g, sh, sc) * (i < last).astype(F32)).astype(BF16)
    pad_l = SHORT_CONV // 2
    lane = lax.broadcasted_iota(jnp.int32, (tm, DK_C), 1)
    first_half = (lane & (DK_C // 2 - 1)) < DK_C // 4
    for sec, o_ref in enumerate((q_ref, k_ref, v_ref)):
        cols = slice(sec * D_CQK, (sec + 1) * D_CQK)
        p_ref[...] = _dot(hs_ref[...], w_ref[:, cols])
        y = None
        for kk in range(SHORT_CONV):
            t = p_ref[pl.ds(HALO - pad_l + kk, tm), :] * cw_ref[kk:kk + 1, cols]
            y = t if y is None else y + t
        y = _silu(y)
        if sec < 2:
            ss = _dot_exact_rhs(y * y, gm_ref[...], 2)
            y = y * lax.rsqrt(ss + EPS)
            if use_rope:
                heads = []
                for hh in range(H_C):
                    yh = y[:, hh * DK_C:(hh + 1) * DK_C]
                    swapped = jnp.where(first_half, pltpu.roll(yh, DK_C - DK_C // 4, axis=1),
                                        pltpu.roll(yh, DK_C // 4, axis=1))
                    heads.append(yh * rc_ref[...] + swapped * rs_ref[...])
                y = jnp.concatenate(heads, axis=1)
            if sec == 0:
                y = y * (DK_C ** -0.5)
        o_ref[...] = y
    hc = hs_ref[HALO:HALO + tm]
    og_ref[...] = _dot(hc, w_ref[:, 3 * D_CQK:3 * D_CQK + D_CV])
    db = _dot(hc, wdb_ref[...])
    gval = -jnp.exp(gbp_ref[0:1, :]) * _softplus(db + gbp_ref[1:2, :])
    gb_ref[...] = gbp_ref[2:3, :] * gval + gbp_ref[3:4, :] * _sigmoid(db)


def _proj_c(x, g, shift, scale, w, wdb, cw, rc, rs, gmat, gbp, tm, use_rope):
    b, n, d = x.shape
    prev, nxt = _halo_specs(tm, n, d)
    row = pl.BlockSpec((None, 1, d), lambda bb, i: (bb, 0, 0))
    tile512 = pl.BlockSpec((None, tm, D_CQK), lambda bb, i: (bb, i, 0))
    shp = jax.ShapeDtypeStruct((b, n, D_CQK), F32)
    rope_spec = pl.BlockSpec((tm, DK_C), lambda bb, i: (i, 0))
    return pl.pallas_call(
        functools.partial(_proj_c_kernel, tm=tm, use_rope=use_rope),
        name="proj_c",
        grid=(b, n // tm),
        in_specs=[
            prev, pl.BlockSpec((None, tm, d), lambda bb, i: (bb, i, 0)), nxt,
            pl.BlockSpec((1, d), lambda bb, i: (0, 0)),
            row, row,
            pl.BlockSpec((d, 3 * D_CQK + D_CV), lambda bb, i: (0, 0)),
            pl.BlockSpec((d, D_CQK), lambda bb, i: (0, 0)),
            pl.BlockSpec((SHORT_CONV, 3 * D_CQK), lambda bb, i: (0, 0)),
            rope_spec, rope_spec,
            pl.BlockSpec((D_CQK, D_CQK), lambda bb, i: (0, 0)),
            pl.BlockSpec((8, D_CQK), lambda bb, i: (0, 0)),
        ],
        out_specs=[tile512] * 5,
        out_shape=[shp] * 5,
        scratch_shapes=[pltpu.VMEM((tm + 2 * HALO, d), BF16), pltpu.VMEM((tm + 2 * HALO, D_CQK), F32)],
        compiler_params=_cparams("parallel", "parallel"),
    )(x, x, x, g, shift, scale, w, wdb, cw, rc, rs, gmat, gbp)


def _gdn_masks():
    n = GDN_BLOCK
    r = np.arange(n)[:, None]
    c = np.arange(n)[None, :]
    same = lambda s: (r // s) == (c // s)
    chunk = same(CHUNK)
    ms = [chunk & (r >= c), chunk & (r > c), chunk & (r <= c), chunk & (r < c), same(8),
          same(16) & ~same(8), same(32) & ~same(16), same(64) & ~same(32)]
    return np.stack(ms).astype(np.float32)


def _tri_inv(lmat, eye, m_ref, passes):
    n0 = -(lmat * m_ref[4])
    n2 = _mm(n0, n0, passes)
    n4 = _mm(n2, n2, passes)
    t1 = eye + n0 + n2 + _mm(n0, n2, passes)
    t = t1 + _mm(t1, n4, passes)
    for lvl in (5, 6, 7):
        lo = lmat * m_ref[lvl]
        t = t - _mm(t, _mm(lo, t, passes), passes)
    return t


def _gdn_block(q_ref, k_ref, v_ref, g_ref, s_ref, m_ref, cum_ref, d):
    nblk = GDN_BLOCK
    nch = nblk // CHUNK
    q = q_ref[...]
    k = k_ref[...]
    v = v_ref[...]
    g_t = g_ref[...].T[0:8]
    csum = _dot_exact_rhs(g_t, cum_ref[d], 3)
    rid = lax.broadcasted_iota(jnp.int32, (8, nblk), 0)
    r8 = jnp.where(rid == d, csum, g_t)
    cols = jnp.concatenate([r8, jnp.zeros((DK_C - 8, nblk), F32)], axis=0).T
    gcol = jnp.broadcast_to(cols[:, d:d + 1], (nblk, DK_C))
    bcol = jnp.broadcast_to(cols[:, 2 + d:3 + d], (nblk, DK_C))
    grow = r8[d:d + 1, :]
    brow = r8[2 + d:3 + d, :]
    incl = m_ref[2 * d]
    strict = m_ref[2 * d + 1]
    eye = incl - strict
    gcol2 = jnp.concatenate([gcol, gcol], axis=1)
    bcol2 = jnp.concatenate([bcol, bcol], axis=1)
    decay = jnp.exp(jnp.where(incl > 0.5, gcol2 - grow, NEG_INF))
    k16 = k.astype(BF16)
    kk = _dot_nt(k16, k16)
    qk = _dot_nt(q.astype(BF16), k16)
    lmat = kk * bcol2 * (decay * strict)
    amat = qk * decay
    tinv = _tri_inv(lmat, eye, m_ref, GDN_INV_PASSES)
    u = _dot_exact_lhs_f32(tinv * brow, v.astype(BF16))
    w = _dot_exact_lhs_f32(tinv * (brow * jnp.exp(grow)), k16)
    qd = q * jnp.exp(gcol)
    order = range(nch) if d == 0 else range(nch - 1, -1, -1)
    gtot = []
    kd_parts = []
    for n in range(nch):
        end = n * CHUNK + (CHUNK - 1 if d == 0 else 0)
        gt = gcol[end:end + 1, :]
        gtot.append(gt)
        sl = slice(n * CHUNK, (n + 1) * CHUNK)
        kd_parts.append(k[sl] * jnp.exp(gt - gcol[sl]))
    kd_t = jnp.concatenate(kd_parts, axis=0).T.astype(BF16)
    s = s_ref[...]
    vnew = [None] * nch
    ointer = [None] * nch
    zeros = jnp.zeros((CHUNK, DV_C), BF16)
    for n in order:
        sl = slice(n * CHUNK, (n + 1) * CHUNK)
        wq = jnp.concatenate([w[sl], qd[sl]], axis=0).astype(BF16)
        r = _dot(wq, s.astype(BF16))
        vn = u[sl] - r[0:CHUNK]
        ointer[n] = r[CHUNK:2 * CHUNK]
        vnew[n] = vn
        vpad = jnp.concatenate([vn.astype(BF16) if m == n else zeros for m in range(nch)], axis=0)
        s = s * jnp.exp(gtot[n]) + _dot(kd_t, vpad)
    s_ref[...] = s
    vn_all = jnp.concatenate(vnew, axis=0).astype(BF16)
    return jnp.concatenate(ointer, axis=0) + _dot(amat.astype(BF16), vn_all)


def _dot_exact_lhs_f32(a, b_bf16):
    out = None
    for p in _split_bf16(a, GDN_SOL_PASSES):
        t = _dot(p, b_bf16)
        out = t if out is None else out + t
    return out


def _gdn_finalize(o, gate, gain):
    ms = jnp.mean(o * o, axis=-1, keepdims=True)
    return (o * lax.rsqrt(ms + EPS) * gain * _silu(gate)).astype(BF16)


def _gdn_kernel(qf_ref, kf_ref, vf_ref, gf_ref, ogf_ref, qb_ref, kb_ref, vb_ref, gb_ref, ogb_ref,
                s0f_ref, s0b_ref, m_ref, cum_ref, gain_ref,
                y_ref, sfo_ref, sbo_ref, oacc_ref, sf_ref, sb_ref, *, nb):
    i = pl.program_id(2)

    @pl.when(i == 0)
    def _():
        sf_ref[...] = s0f_ref[...]
        sb_ref[...] = s0b_ref[...]

    of = _gdn_block(qf_ref, kf_ref, vf_ref, gf_ref, sf_ref, m_ref, cum_ref, 0)
    ob = _gdn_block(qb_ref, kb_ref, vb_ref, gb_ref, sb_ref, m_ref, cum_ref, 1)
    rows_f = pl.ds(pl.multiple_of(i * GDN_BLOCK, GDN_BLOCK), GDN_BLOCK)
    rows_b = pl.ds(pl.multiple_of((nb - 1 - i) * GDN_BLOCK, GDN_BLOCK), GDN_BLOCK)
    gain = gain_ref[...]

    @pl.when(2 * i < nb - 1)
    def _():
        oacc_ref[rows_f, :] = of
        oacc_ref[rows_b, :] = ob

    if nb % 2 == 1:
        @pl.when(2 * i == nb - 1)
        def _():
            y_ref[rows_f, :] = _gdn_finalize(of + ob, ogf_ref[...], gain)

    @pl.when(2 * i > nb - 1)
    def _():
        y_ref[rows_f, :] = _gdn_finalize(oacc_ref[rows_f, :] + of, ogf_ref[...], gain)
        y_ref[rows_b, :] = _gdn_finalize(oacc_ref[rows_b, :] + ob, ogb_ref[...], gain)

    @pl.when(i == nb - 1)
    def _():
        sfo_ref[...] = sf_ref[...]
        sbo_ref[...] = sb_ref[...]


def _gdn(q, k, v, gb, og, s0f, s0b, masks, cums, gain):
    b, n, _ = q.shape
    nb = n // GDN_BLOCK
    fwd = pl.BlockSpec((None, GDN_BLOCK, DK_C), lambda bb, hh, i: (bb, i, hh))
    bwd = pl.BlockSpec((None, GDN_BLOCK, DK_C), lambda bb, hh, i: (bb, nb - 1 - i, hh))
    st = pl.BlockSpec((None, None, DK_C, DV_C), lambda bb, hh, i: (bb, hh, 0, 0))
    st_shape = jax.ShapeDtypeStruct((b, H_C, DK_C, DV_C), F32)
    return pl.pallas_call(
        functools.partial(_gdn_kernel, nb=nb),
        name="gdn_scan",
        grid=(b, H_C, nb),
        in_specs=[fwd] * 5 + [bwd] * 5 + [
            st, st,
            pl.BlockSpec((8, GDN_BLOCK, GDN_BLOCK), lambda bb, hh, i: (0, 0, 0)),
            pl.BlockSpec((2, GDN_BLOCK, GDN_BLOCK), lambda bb, hh, i: (0, 0, 0)),
            pl.BlockSpec((1, DV_C), lambda bb, hh, i: (0, 0)),
        ],
        out_specs=[pl.BlockSpec((None, n, DV_C), lambda bb, hh, i: (bb, 0, hh)), st, st],
        out_shape=[jax.ShapeDtypeStruct((b, n, D_CV), BF16), st_shape, st_shape],
        scratch_shapes=[pltpu.VMEM((n, DV_C), F32), pltpu.VMEM((DK_C, DV_C), F32), pltpu.VMEM((DK_C, DV_C), F32)],
        compiler_params=_cparams("parallel", "parallel", "arbitrary"),
    )(q, k, v, gb, og, q, k, v, gb, og, s0f, s0b, masks, cums, gain)


def _merge_kernel(x_ref, g_ref, sh_ref, sc_ref, gt_ref, ya_ref, yb_ref, yc_ref, wg_ref, wbr_ref, wo_ref,
                  o_ref, hs_ref, acc_ref):
    j = pl.program_id(2)

    @pl.when(j == 0)
    def _():
        hs_ref[...] = _norm_mod(x_ref[...], g_ref[...], sh_ref[...], sc_ref[...]).astype(BF16)
        acc_ref[...] = jnp.zeros_like(acc_ref)

    h = hs_ref[...]
    z = None
    for br, y_ref in enumerate((ya_ref, yb_ref, yc_ref)):
        t = _sigmoid(_dot(h, wg_ref[br])) * _dot(y_ref[...], wbr_ref[br])
        z = t if z is None else z + t
    acc_ref[...] += _dot(z.astype(BF16), wo_ref[...])

    @pl.when(j == pl.num_programs(2) - 1)
    def _():
        o_ref[...] = x_ref[...] + gt_ref[...] * acc_ref[...]


def _merge(x, g, shift, scale, gate, ya, yb, yc, wg, wbr, wo, tm, tn):
    b, n, d = x.shape
    row = pl.BlockSpec((None, 1, d), lambda bb, i, j: (bb, 0, 0))
    xt = pl.BlockSpec((None, tm, d), lambda bb, i, j: (bb, i, 0))
    yt = pl.BlockSpec((None, tm, D_A), lambda bb, i, j: (bb, i, 0))
    return pl.pallas_call(
        _merge_kernel,
        name="merge_out",
        grid=(b, n // tm, d // tn),
        in_specs=[
            xt, pl.BlockSpec((1, d), lambda bb, i, j: (0, 0)), row, row, row, yt, yt, yt,
            pl.BlockSpec((3, d, tn), lambda bb, i, j: (0, 0, j)),
            pl.BlockSpec((3, D_A, tn), lambda bb, i, j: (0, 0, j)),
            pl.BlockSpec((tn, d), lambda bb, i, j: (j, 0)),
        ],
        out_specs=xt,
        out_shape=jax.ShapeDtypeStruct((b, n, d), F32),
        scratch_shapes=[pltpu.VMEM((tm, d), BF16), pltpu.VMEM((tm, d), F32)],
        compiler_params=_cparams("parallel", "parallel", "arbitrary"),
    )(x, g, shift, scale, gate, ya, yb, yc, wg, wbr, wo)


def _ffn_kernel(xp_ref, x_ref, xn_ref, g_ref, sh_ref, sc_ref, gt_ref, wug_ref, wuv_ref, cwg_ref, cwv_ref,
                cbg_ref, cbv_ref, wd_ref, o_ref, hs_ref, ug_ref, uv_ref, acc_ref, *, tm):
    i = pl.program_id(1)
    j = pl.program_id(2)
    last = pl.num_programs(1) - 1

    @pl.when(j == 0)
    def _():
        g, sh, sc = g_ref[...], sh_ref[...], sc_ref[...]
        hs_ref[0:HALO] = (_norm_mod(xp_ref[...], g, sh, sc) * (i > 0).astype(F32)).astype(BF16)
        hs_ref[HALO:HALO + tm] = _norm_mod(x_ref[...], g, sh, sc).astype(BF16)
        hs_ref[HALO + tm:2 * HALO + tm] = (_norm_mod(xn_ref[...], g, sh, sc) * (i < last).astype(F32)).astype(BF16)
        acc_ref[...] = jnp.zeros_like(acc_ref)

    h = hs_ref[...]
    ug_ref[...] = _dot(h, wug_ref[...])
    uv_ref[...] = _dot(h, wuv_ref[...])
    pad = FFN_CONV // 2
    cg = cbg_ref[...]
    cv = cbv_ref[...]
    for kk in range(FFN_CONV):
        cg = cg + ug_ref[pl.ds(HALO - pad + kk, tm), :] * cwg_ref[kk:kk + 1, :]
        cv = cv + uv_ref[pl.ds(HALO - pad + kk, tm), :] * cwv_ref[kk:kk + 1, :]
    act = (_silu(cg) * cv).astype(BF16)
    acc_ref[...] += _dot(act, wd_ref[...])

    @pl.when(j == pl.num_programs(2) - 1)
    def _():
        o_ref[...] = x_ref[...] + gt_ref[...] * acc_ref[...]


def _ffn(x, g, shift, scale, gate, wup, cw, cb, wdown, tm, tf):
    b, n, d = x.shape
    nf = D_FF // tf
    prev, nxt = _halo_specs(tm, n, d)
    row = pl.BlockSpec((None, 1, d), lambda bb, i, j: (bb, 0, 0))
    xt = pl.BlockSpec((None, tm, d), lambda bb, i, j: (bb, i, 0))
    return pl.pallas_call(
        functools.partial(_ffn_kernel, tm=tm),
        name="conv_ffn",
        grid=(b, n // tm, nf),
        in_specs=[
            prev, xt, nxt, pl.BlockSpec((1, d), lambda bb, i, j: (0, 0)), row, row, row,
            pl.BlockSpec((d, tf), lambda bb, i, j: (0, j)),
            pl.BlockSpec((d, tf), lambda bb, i, j: (0, nf + j)),
            pl.BlockSpec((FFN_CONV, tf), lambda bb, i, j: (0, j)),
            pl.BlockSpec((FFN_CONV, tf), lambda bb, i, j: (0, nf + j)),
            pl.BlockSpec((1, tf), lambda bb, i, j: (0, j)),
            pl.BlockSpec((1, tf), lambda bb, i, j: (0, nf + j)),
            pl.BlockSpec((tf, d), lambda bb, i, j: (j, 0)),
        ],
        out_specs=xt,
        out_shape=jax.ShapeDtypeStruct((b, n, d), F32),
        scratch_shapes=[pltpu.VMEM((tm + 2 * HALO, d), BF16), pltpu.VMEM((tm + 2 * HALO, tf), F32),
                        pltpu.VMEM((tm + 2 * HALO, tf), F32), pltpu.VMEM((tm, d), F32)],
        compiler_params=_cparams("parallel", "parallel", "arbitrary"),
    )(x, x, x, g, shift, scale, gate, wup, wup, cw, cw, cb, cb, wdown)


def _rope_tables(n_tok):
    t = jnp.arange(n_tok)
    row = (t // GRID_W).astype(F32)
    col = (t % GRID_W).astype(F32)
    n_freq = DK_C // 4
    inv = jnp.power(ROPE_BASE, -jnp.arange(n_freq, dtype=F32) / n_freq)
    ar = row[:, None] * inv
    ac = col[:, None] * inv
    cos = jnp.concatenate([jnp.cos(ar), jnp.cos(ar), jnp.cos(ac), jnp.cos(ac)], axis=-1)
    sin = jnp.concatenate([-jnp.sin(ar), jnp.sin(ar), -jnp.sin(ac), jnp.sin(ac)], axis=-1)
    return cos, sin


def _block_ones(n, blk):
    idx = np.arange(n) // blk
    return jnp.asarray((idx[:, None] == idx[None, :]).astype(np.float32), dtype=BF16)


def _heads_to_lanes(y):
    b, h, n, dh = y.shape
    return jnp.transpose(y, (0, 2, 1, 3)).reshape(b, n, h * dh)


def kernel(x, c, ctx, c_ctx, ada_w, ada_b, norm1_g, norm2_g, w_in, conv_a_w, conv_a_b, ln_a_g, ln_a_b, qn_g, kn_g,
           rpb, conv_c_w, a_log, dt_bias, onorm_g, w_branch, w_out, ffn_up, ffn_conv_w, ffn_conv_b, ffn_down):
    batch, n_lat, d = x.shape
    n_ctx = ctx.shape[1]
    depth = ada_w.shape[0]

    cc = jnp.zeros((8, d), F32).at[:batch].set(c).at[batch].set(c_ctx)
    mods = _ada(cc, ada_w, ada_b).reshape(depth, 8, 6, 1, d)

    rope_c, rope_s = _rope_tables(n_lat)
    ones_c = jnp.ones((n_ctx, DK_C), F32)
    zeros_c = jnp.zeros((n_ctx, DK_C), F32)
    gm64 = _block_ones(D_B, DH_B)
    gm128 = _block_ones(D_CQK, DK_C)
    masks_np = _gdn_masks()
    masks = jnp.asarray(masks_np)
    cums = jnp.asarray(np.stack([masks_np[0].T, masks_np[2].T]), dtype=BF16)
    s_zero = jnp.zeros((batch, H_C, DK_C, DV_C), F32)

    off_b = 2 * D_A
    off_c = off_b + 3 * D_B
    off_db = off_c + 3 * D_CQK + D_CV
    off_g = off_db + 4 * H_C

    x_lat, x_ctx = x, ctx
    for l in range(depth):
        ctx_out = l < depth - 1
        m_lat = [mods[l, :batch, s] for s in range(6)]
        m_ctx = [jnp.broadcast_to(mods[l, batch:batch + 1, s], (batch, 1, d)) for s in range(6)]
        g1 = norm1_g[l].reshape(1, d)
        g2 = norm2_g[l].reshape(1, d)
        wl = w_in[l]
        wa = wl[:, :D_A].astype(BF16)
        wag = wl[:, D_A:2 * D_A].astype(BF16)
        wb = wl[:, off_b:off_c].astype(BF16)
        wc = wl[:, off_c:off_db].astype(BF16)
        wdb_cols = wl[:, off_db:off_g].reshape(d, 2, 2, H_C)
        wdb = jnp.zeros((d, H_C, DK_C), F32).at[:, :, :4].set(
            jnp.transpose(wdb_cols, (0, 3, 1, 2)).reshape(d, H_C, 4)).reshape(d, D_CQK).astype(BF16)
        wgates = jnp.transpose(wl[:, off_g:].reshape(d, 3, d), (1, 0, 2)).astype(BF16)
        gbp = jnp.zeros((8, H_C, DK_C), F32)
        gbp = gbp.at[0, :, :2].set(a_log[l].T).at[1, :, :2].set(dt_bias[l].T)
        gbp = gbp.at[2, :, :2].set(1.0).at[3, :, 2:4].set(1.0).reshape(8, D_CQK)
        gq = (jnp.tile(qn_g[l], H_B) * DH_B ** -0.5).reshape(1, D_B)
        gk = jnp.tile(kn_g[l], H_B).reshape(1, D_B)
        bias = _na_bias_table(rpb[l])
        wbr = w_branch[l].astype(BF16)
        wo = w_out[l].astype(BF16)
        wup = ffn_up[l].astype(BF16)
        wdn = ffn_down[l].astype(BF16)
        cwf = ffn_conv_w[l]
        cbf = ffn_conv_b[l].reshape(1, 2 * D_FF)
        gain = onorm_g[l].reshape(1, DV_C)

        qb_c, kb_c, vb_c = _proj_b(x_ctx, g1, m_ctx[0], m_ctx[1], wb, gq, gk, gm64, n_ctx)
        qc_c, kc_c, vc_c, og_c, gb_c = _proj_c(x_ctx, g1, m_ctx[0], m_ctx[1], wc, wdb, conv_c_w[l], ones_c, zeros_c,
                                               gm128, gbp, n_ctx, False)
        yc_c, sf_c, sb_c = _gdn(qc_c, kc_c, vc_c, gb_c, og_c, s_zero, s_zero, masks, cums, gain)

        ya_l = _conv_a(_proj_a(x_lat, g1, m_lat[0], m_lat[1], wa, wag, 512),
                       conv_a_w[l], conv_a_b[l], ln_a_g[l], ln_a_b[l], 256)
        qb_l, kb_l, vb_l = _proj_b(x_lat, g1, m_lat[0], m_lat[1], wb, gq, gk, gm64, 512)
        yb_l = _heads_to_lanes(_na_attention(qb_l, kb_l, vb_l, kb_c, vb_c, bias, 8))
        qc_l, kc_l, vc_l, og_l, gb_l = _proj_c(x_lat, g1, m_lat[0], m_lat[1], wc, wdb, conv_c_w[l], rope_c, rope_s,
                                               gm128, gbp, 512, True)
        yc_l, _, _ = _gdn(qc_l, kc_l, vc_l, gb_l, og_l, sf_c, sb_c, masks, cums, gain)
        x_lat = _merge(x_lat, g1, m_lat[0], m_lat[1], m_lat[2], ya_l, yb_l, yc_l, wgates, wbr, wo, 512, 256)
        x_lat = _ffn(x_lat, g2, m_lat[3], m_lat[4], m_lat[5], wup, cwf, cbf, wdn, 1024, 256)

        if ctx_out:
            ya_c = _conv_a(_proj_a(x_ctx, g1, m_ctx[0], m_ctx[1], wa, wag, n_ctx),
                           conv_a_w[l], conv_a_b[l], ln_a_g[l], ln_a_b[l], n_ctx)
            yb_c = _heads_to_lanes(_ctx_attention(qb_c, kb_c, vb_c))
            x_ctx = _merge(x_ctx, g1, m_ctx[0], m_ctx[1], m_ctx[2], ya_c, yb_c, yc_c, wgates, wbr, wo, n_ctx, 256)
            x_ctx = _ffn(x_ctx, g2, m_ctx[3], m_ctx[4], m_ctx[5], wup, cwf, cbf, wdn, n_ctx, 256)
    return x_lat
```

```python
import functools
import math

import numpy as np
import jax
import jax.numpy as jnp
from jax import lax
from jax.experimental import pallas as pl
from jax.experimental.pallas import tpu as pltpu

F32 = jnp.float32
BF16 = jnp.bfloat16

D_MODEL = 1024
GRID_W = 64
EPS = 1e-6
NEG_INF = -1e30
D_A = 512
CONV_A = 31
H_B = 8
DH_B = 64
D_B = H_B * DH_B
NA_ROWS = 8
NA_COLS = 16
H_C = 4
DK_C = 128
DV_C = 128
D_CQK = H_C * DK_C
D_CV = H_C * DV_C
SHORT_CONV = 4
CHUNK = 64
ROPE_BASE = 10000.0
D_FF = 2816
FFN_CONV = 3

VMEM_LIMIT_BYTES = 48 * 1024 * 1024
HALO = 16
GDN_BLOCK = 256
GDN_INV_PASSES = 1
GDN_SOL_PASSES = 1


def _cparams(*sem):
    return pltpu.CompilerParams(dimension_semantics=sem, vmem_limit_bytes=VMEM_LIMIT_BYTES)


def _dot(a, b):
    return jnp.dot(a, b, preferred_element_type=F32)


def _dot_nt(a, b):
    return lax.dot_general(a, b, (((1,), (1,)), ((), ())), preferred_element_type=F32)


def _split_bf16(x, n):
    parts = []
    r = x
    for idx in range(n):
        p = r.astype(BF16)
        parts.append(p)
        if idx + 1 < n:
            r = r - p.astype(F32)
    return parts


def _dot_exact_rhs(a, b_bf16, n):
    out = None
    for p in _split_bf16(a, n):
        t = _dot(p, b_bf16)
        out = t if out is None else out + t
    return out


def _dot_exact_lhs(a_bf16, b, n):
    out = None
    for p in _split_bf16(b, n):
        t = _dot(a_bf16, p)
        out = t if out is None else out + t
    return out


def _mm(a, b, passes):
    if passes == 1:
        return _dot(a.astype(BF16), b.astype(BF16))
    a_hi, a_lo = _split_bf16(a, 2)
    b_hi, b_lo = _split_bf16(b, 2)
    return _dot(a_hi, b_hi) + (_dot(a_lo, b_hi) + _dot(a_hi, b_lo))


def _sigmoid(x):
    return 1.0 / (1.0 + jnp.exp(-x))


def _silu(x):
    return x * _sigmoid(x)


def _softplus(x):
    return jnp.maximum(x, 0.0) + jnp.log(1.0 + jnp.exp(-jnp.abs(x)))


def _norm_mod(x, g, shift, scale):
    ms = jnp.mean(x * x, axis=-1, keepdims=True)
    y = x * lax.rsqrt(ms + EPS) * g
    return y * (1.0 + scale) + shift


def _halo_specs(tm, n_tok, width):
    per = tm // HALO
    nblk = n_tok // HALO
    prev = pl.BlockSpec((None, HALO, width), lambda b, i, *_: (b, jnp.maximum(i * per - 1, 0), 0))
    nxt = pl.BlockSpec((None, HALO, width), lambda b, i, *_: (b, jnp.minimum((i + 1) * per, nblk - 1), 0))
    return prev, nxt


def _ada_kernel(c_ref, w_ref, b_ref, o_ref):
    s = _silu(c_ref[...])
    o_ref[...] = _mm(s, w_ref[...], 3) + b_ref[...]


def _ada(cc, ada_w, ada_b):
    depth = ada_w.shape[0]
    d = D_MODEL
    return pl.pallas_call(
        _ada_kernel,
        name="ada_mod",
        grid=(depth, 6),
        in_specs=[
            pl.BlockSpec((8, d), lambda l, j: (0, 0)),
            pl.BlockSpec((None, d, d), lambda l, j: (l, 0, j)),
            pl.BlockSpec((None, 1, d), lambda l, j: (l, 0, j)),
        ],
        out_specs=pl.BlockSpec((None, 8, d), lambda l, j: (l, 0, j)),
        out_shape=jax.ShapeDtypeStruct((depth, 8, 6 * d), F32),
        compiler_params=_cparams("arbitrary", "arbitrary"),
    )(cc, ada_w, ada_b.reshape(depth, 1, 6 * d))


def _proj_a_kernel(x_ref, g_ref, sh_ref, sc_ref, wa_ref, wg_ref, o_ref):
    h = _norm_mod(x_ref[...], g_ref[...], sh_ref[...], sc_ref[...]).astype(BF16)
    a = _dot(h, wa_ref[...])
    gate = _dot(h, wg_ref[...])
    o_ref[...] = a * _sigmoid(gate)


def _proj_a(x, g, shift, scale, wa, wg, tm):
    b, n, d = x.shape
    row = pl.BlockSpec((None, 1, d), lambda bb, i: (bb, 0, 0))
    return pl.pallas_call(
        _proj_a_kernel,
        name="proj_a",
        grid=(b, n // tm),
        in_specs=[
            pl.BlockSpec((None, tm, d), lambda bb, i: (bb, i, 0)),
            pl.BlockSpec((1, d), lambda bb, i: (0, 0)),
            row, row,
            pl.BlockSpec((d, D_A), lambda bb, i: (0, 0)),
            pl.BlockSpec((d, D_A), lambda bb, i: (0, 0)),
        ],
        out_specs=pl.BlockSpec((None, tm, D_A), lambda bb, i: (bb, i, 0)),
        out_shape=jax.ShapeDtypeStruct((b, n, D_A), F32),
        compiler_params=_cparams("parallel", "parallel"),
    )(x, g, shift, scale, wa, wg)


def _conv_a_kernel(yp_ref, y_ref, yn_ref, cw_ref, cb_ref, lg_ref, lb_ref, o_ref, ext_ref, *, tm):
    i = pl.program_id(1)
    last = pl.num_programs(1) - 1
    ext_ref[0:HALO] = yp_ref[...] * (i > 0).astype(F32)
    ext_ref[HALO:HALO + tm] = y_ref[...]
    ext_ref[HALO + tm:2 * HALO + tm] = yn_ref[...] * (i < last).astype(F32)
    pad = CONV_A // 2
    acc = jnp.zeros((tm, D_A), F32) + cb_ref[...]
    for k in range(CONV_A):
        acc = acc + ext_ref[pl.ds(HALO - pad + k, tm), :] * cw_ref[k:k + 1, :]
    mu = jnp.mean(acc, axis=-1, keepdims=True)
    cen = acc - mu
    var = jnp.mean(cen * cen, axis=-1, keepdims=True)
    y = cen * lax.rsqrt(var + EPS) * lg_ref[...] + lb_ref[...]
    o_ref[...] = _silu(y).astype(BF16)


def _conv_a(y, cw, cb, lg, lb, tm):
    b, n, c = y.shape
    prev, nxt = _halo_specs(tm, n, c)
    vec = pl.BlockSpec((1, c), lambda bb, i: (0, 0))
    return pl.pallas_call(
        functools.partial(_conv_a_kernel, tm=tm),
        name="conv_a",
        grid=(b, n // tm),
        in_specs=[prev, pl.BlockSpec((None, tm, c), lambda bb, i: (bb, i, 0)), nxt,
                  pl.BlockSpec((CONV_A, c), lambda bb, i: (0, 0)), vec, vec, vec],
        out_specs=pl.BlockSpec((None, tm, c), lambda bb, i: (bb, i, 0)),
        out_shape=jax.ShapeDtypeStruct((b, n, c), BF16),
        scratch_shapes=[pltpu.VMEM((tm + 2 * HALO, c), F32)],
        compiler_params=_cparams("parallel", "parallel"),
    )(y, y, y, cw, cb.reshape(1, c), lg.reshape(1, c), lb.reshape(1, c))


def _proj_b_kernel(x_ref, g_ref, sh_ref, sc_ref, w_ref, gq_ref, gk_ref, gm_ref, q_ref, k_ref, v_ref):
    h = _norm_mod(x_ref[...], g_ref[...], sh_ref[...], sc_ref[...]).astype(BF16)
    for idx, (gain_ref, o_ref) in enumerate(((gq_ref, q_ref), (gk_ref, k_ref), (None, v_ref))):
        acc = _dot(h, w_ref[:, idx * D_B:(idx + 1) * D_B])
        if gain_ref is not None:
            ss = _dot_exact_rhs(acc * acc, gm_ref[...], 2)
            acc = acc * lax.rsqrt(ss * (1.0 / DH_B) + EPS) * gain_ref[...]
        for hh in range(H_B):
            o_ref[hh] = acc[:, hh * DH_B:(hh + 1) * DH_B].astype(BF16)


def _proj_b(x, g, shift, scale, w, gq, gk, gmat, tm):
    b, n, d = x.shape
    row = pl.BlockSpec((None, 1, d), lambda bb, i: (bb, 0, 0))
    vec = pl.BlockSpec((1, D_B), lambda bb, i: (0, 0))
    head_out = pl.BlockSpec((None, H_B, tm, DH_B), lambda bb, i: (bb, 0, i, 0))
    shp = jax.ShapeDtypeStruct((b, H_B, n, DH_B), BF16)
    return pl.pallas_call(
        _proj_b_kernel,
        name="proj_b",
        grid=(b, n // tm),
        in_specs=[
            pl.BlockSpec((None, tm, d), lambda bb, i: (bb, i, 0)),
            pl.BlockSpec((1, d), lambda bb, i: (0, 0)),
            row, row,
            pl.BlockSpec((d, 3 * D_B), lambda bb, i: (0, 0)),
            vec, vec,
            pl.BlockSpec((D_B, D_B), lambda bb, i: (0, 0)),
        ],
        out_specs=[head_out, head_out, head_out],
        out_shape=[shp, shp, shp],
        compiler_params=_cparams("parallel", "parallel"),
    )(x, g, shift, scale, w, gq, gk, gmat)


def _na_kernel(q_ref, k_ref, v_ref, kc_ref, vc_ref, bias_ref, o_ref, *, rb, rows):
    i = pl.program_id(2)
    n_loc = NA_ROWS * GRID_W
    qcol = lax.broadcasted_iota(jnp.int32, (GRID_W, n_loc), 0)
    kcol = lax.broadcasted_iota(jnp.int32, (GRID_W, n_loc), 1) & (GRID_W - 1)
    cs = jnp.clip(qcol - NA_COLS // 2, 0, GRID_W - NA_COLS)
    col_ok = jnp.abs(2 * (kcol - cs) - (NA_COLS - 1)) <= NA_COLS - 1
    for rr in range(rb):
        r = i * rb + rr
        start = jnp.clip(r - NA_ROWS // 2, 0, rows - NA_ROWS)
        variant = start - r + NA_ROWS - 1
        q = q_ref[rr * GRID_W:(rr + 1) * GRID_W, :]
        tok0 = pl.multiple_of(start * GRID_W, GRID_W)
        kb = k_ref[pl.ds(tok0, n_loc), :]
        vb = v_ref[pl.ds(tok0, n_loc), :]
        s = _dot_nt(q, kb) + bias_ref[variant]
        s = jnp.where(col_ok, s, NEG_INF)
        sc = _dot_nt(q, kc_ref[...])
        m = jnp.maximum(jnp.max(s, axis=-1, keepdims=True), jnp.max(sc, axis=-1, keepdims=True))
        p = jnp.exp(s - m)
        pc = jnp.exp(sc - m)
        l = jnp.sum(p, axis=-1, keepdims=True) + jnp.sum(pc, axis=-1, keepdims=True)
        o = _dot(p.astype(BF16), vb) + _dot(pc.astype(BF16), vc_ref[...])
        o_ref[rr * GRID_W:(rr + 1) * GRID_W, :] = (o / l).astype(BF16)


def _na_attention(q, k, v, kc, vc, bias, rb):
    b, h, t, dh = q.shape
    n_ctx = kc.shape[2]
    rows = t // GRID_W
    full = pl.BlockSpec((None, None, t, dh), lambda bb, hh, i: (bb, hh, 0, 0))
    cfull = pl.BlockSpec((None, None, n_ctx, dh), lambda bb, hh, i: (bb, hh, 0, 0))
    tile = pl.BlockSpec((None, None, rb * GRID_W, dh), lambda bb, hh, i: (bb, hh, i, 0))
    return pl.pallas_call(
        functools.partial(_na_kernel, rb=rb, rows=rows),
        name="na_attn",
        grid=(b, h, rows // rb),
        in_specs=[tile, full, full, cfull, cfull,
                  pl.BlockSpec((None, NA_ROWS, GRID_W, NA_ROWS * GRID_W), lambda bb, hh, i: (hh, 0, 0, 0))],
        out_specs=tile,
        out_shape=jax.ShapeDtypeStruct((b, h, t, dh), BF16),
        compiler_params=_cparams("parallel", "parallel", "arbitrary"),
    )(q, k, v, kc, vc, bias)


def _ctx_attn_kernel(q_ref, k_ref, v_ref, o_ref):
    s = _dot_nt(q_ref[...], k_ref[...])
    m = jnp.max(s, axis=-1, keepdims=True)
    p = jnp.exp(s - m)
    l = jnp.sum(p, axis=-1, keepdims=True)
    o_ref[...] = (_dot(p.astype(BF16), v_ref[...]) / l).astype(BF16)


def _ctx_attention(q, k, v):
    b, h, n, dh = q.shape
    full = pl.BlockSpec((None, None, n, dh), lambda bb, hh: (bb, hh, 0, 0))
    return pl.pallas_call(
        _ctx_attn_kernel,
        name="ctx_attn",
        grid=(b, h),
        in_specs=[full, full, full],
        out_specs=full,
        out_shape=jax.ShapeDtypeStruct((b, h, n, dh), BF16),
        compiler_params=_cparams("parallel", "parallel"),
    )(q, k, v)


def _na_bias_table(rpb):
    cidx = np.arange(GRID_W)
    dc = np.clip(cidx[None, :] - cidx[:, None] + NA_COLS - 1, 0, 2 * NA_COLS - 2)
    onehot = (dc[None] == np.arange(2 * NA_COLS - 1)[:, None, None]).astype(np.float32)
    toep = jnp.einsum("hrd,dqk->hrqk", rpb, onehot, precision=lax.Precision.HIGHEST)
    tabs = [jnp.transpose(toep[:, v:v + NA_ROWS], (0, 2, 1, 3)).reshape(H_B, GRID_W, NA_ROWS * GRID_W)
            for v in range(NA_ROWS)]
    return jnp.stack(tabs, axis=1)


def _proj_c_kernel(xp_ref, x_ref, xn_ref, g_ref, sh_ref, sc_ref, w_ref, wdb_ref, cw_ref, rc_ref, rs_ref,
                   gm_ref, gbp_ref, q_ref, k_ref, v_ref, og_ref, gb_ref, hs_ref, p_ref, *, tm, use_rope):
    i = pl.program_id(1)
    last = pl.num_programs(1) - 1
    g, sh, sc = g_ref[...], sh_ref[...], sc_ref[...]
    hs_ref[0:HALO] = (_norm_mod(xp_ref[...], g, sh, sc) * (i > 0).astype(F32)).astype(BF16)
    hs_ref[HALO:HALO + tm] = _norm_mod(x_ref[...], g, sh, sc).astype(BF16)
    hs_ref[HALO + tm:2 * HALO + tm] = (_norm_mod(xn_ref[...], g, sh, sc) * (i < last).astype(F32)).astype(BF16)
    pad_l = SHORT_CONV // 2
    lane = lax.broadcasted_iota(jnp.int32, (tm, DK_C), 1)
    first_half = (lane & (DK_C // 2 - 1)) < DK_C // 4
    for sec, o_ref in enumerate((q_ref, k_ref, v_ref)):
        cols = slice(sec * D_CQK, (sec + 1) * D_CQK)
        p_ref[...] = _dot(hs_ref[...], w_ref[:, cols])
        y = None
        for kk in range(SHORT_CONV):
            t = p_ref[pl.ds(HALO - pad_l + kk, tm), :] * cw_ref[kk:kk + 1, cols]
            y = t if y is None else y + t
        y = _silu(y)
        if sec < 2:
            ss = _dot_exact_rhs(y * y, gm_ref[...], 2)
            y = y * lax.rsqrt(ss + EPS)
            if use_rope:
                heads = []
                for hh in range(H_C):
                    yh = y[:, hh * DK_C:(hh + 1) * DK_C]
                    swapped = jnp.where(first_half, pltpu.roll(yh, DK_C - DK_C // 4, axis=1),
                                        pltpu.roll(yh, DK_C // 4, axis=1))
                    heads.append(yh * rc_ref[...] + swapped * rs_ref[...])
                y = jnp.concatenate(heads, axis=1)
            if sec == 0:
                y = y * (DK_C ** -0.5)
        o_ref[...] = y
    hc = hs_ref[HALO:HALO + tm]
    og_ref[...] = _dot(hc, w_ref[:, 3 * D_CQK:3 * D_CQK + D_CV])
    db = _dot(hc, wdb_ref[...])
    gval = -jnp.exp(gbp_ref[0:1, :]) * _softplus(db + gbp_ref[1:2, :])
    gb_ref[...] = gbp_ref[2:3, :] * gval + gbp_ref[3:4, :] * _sigmoid(db)


def _proj_c(x, g, shift, scale, w, wdb, cw, rc, rs, gmat, gbp, tm, use_rope):
    b, n, d = x.shape
    prev, nxt = _halo_specs(tm, n, d)
    row = pl.BlockSpec((None, 1, d), lambda bb, i: (bb, 0, 0))
    tile512 = pl.BlockSpec((None, tm, D_CQK), lambda bb, i: (bb, i, 0))
    shp = jax.ShapeDtypeStruct((b, n, D_CQK), F32)
    rope_spec = pl.BlockSpec((tm, DK_C), lambda bb, i: (i, 0))
    return pl.pallas_call(
        functools.partial(_proj_c_kernel, tm=tm, use_rope=use_rope),
        name="proj_c",
        grid=(b, n // tm),
        in_specs=[
            prev, pl.BlockSpec((None, tm, d), lambda bb, i: (bb, i, 0)), nxt,
            pl.BlockSpec((1, d), lambda bb, i: (0, 0)),
            row, row,
            pl.BlockSpec((d, 3 * D_CQK + D_CV), lambda bb, i: (0, 0)),
            pl.BlockSpec((d, D_CQK), lambda bb, i: (0, 0)),
            pl.BlockSpec((SHORT_CONV, 3 * D_CQK), lambda bb, i: (0, 0)),
            rope_spec, rope_spec,
            pl.BlockSpec((D_CQK, D_CQK), lambda bb, i: (0, 0)),
            pl.BlockSpec((8, D_CQK), lambda bb, i: (0, 0)),
        ],
        out_specs=[tile512] * 5,
        out_shape=[shp] * 5,
        scratch_shapes=[pltpu.VMEM((tm + 2 * HALO, d), BF16), pltpu.VMEM((tm + 2 * HALO, D_CQK), F32)],
        compiler_params=_cparams("parallel", "parallel"),
    )(x, x, x, g, shift, scale, w, wdb, cw, rc, rs, gmat, gbp)


def _gdn_masks():
    n = GDN_BLOCK
    r = np.arange(n)[:, None]
    c = np.arange(n)[None, :]
    same = lambda s: (r // s) == (c // s)
    chunk = same(CHUNK)
    ms = [chunk & (r >= c), chunk & (r > c), chunk & (r <= c), chunk & (r < c), same(8),
          same(16) & ~same(8), same(32) & ~same(16), same(64) & ~same(32)]
    return np.stack(ms).astype(np.float32)


def _tri_inv(lmat, eye, m_ref, passes):
    n0 = -(lmat * m_ref[4])
    n2 = _mm(n0, n0, passes)
    n4 = _mm(n2, n2, passes)
    t1 = eye + n0 + n2 + _mm(n0, n2, passes)
    t = t1 + _mm(t1, n4, passes)
    for lvl in (5, 6, 7):
        lo = lmat * m_ref[lvl]
        t = t - _mm(t, _mm(lo, t, passes), passes)
    return t


def _gdn_block(q_ref, k_ref, v_ref, g_ref, s_ref, m_ref, cum_ref, d):
    nblk = GDN_BLOCK
    nch = nblk // CHUNK
    q = q_ref[...]
    k = k_ref[...]
    v = v_ref[...]
    g_t = g_ref[...].T[0:8]
    csum = _dot_exact_rhs(g_t, cum_ref[d], 3)
    rid = lax.broadcasted_iota(jnp.int32, (8, nblk), 0)
    r8 = jnp.where(rid == d, csum, g_t)
    cols = jnp.concatenate([r8, jnp.zeros((DK_C - 8, nblk), F32)], axis=0).T
    gcol = jnp.broadcast_to(cols[:, d:d + 1], (nblk, DK_C))
    bcol = jnp.broadcast_to(cols[:, 2 + d:3 + d], (nblk, DK_C))
    grow = r8[d:d + 1, :]
    brow = r8[2 + d:3 + d, :]
    incl = m_ref[2 * d]
    strict = m_ref[2 * d + 1]
    eye = incl - strict
    gcol2 = jnp.concatenate([gcol, gcol], axis=1)
    bcol2 = jnp.concatenate([bcol, bcol], axis=1)
    decay = jnp.exp(jnp.where(incl > 0.5, gcol2 - grow, NEG_INF))
    k16 = k.astype(BF16)
    kk = _dot_nt(k16, k16)
    qk = _dot_nt(q.astype(BF16), k16)
    lmat = kk * bcol2 * (decay * strict)
    amat = qk * decay
    tinv = _tri_inv(lmat, eye, m_ref, GDN_INV_PASSES)
    u = _dot_exact_lhs_f32(tinv * brow, v.astype(BF16))
    w = _dot_exact_lhs_f32(tinv * (brow * jnp.exp(grow)), k16)
    qd = q * jnp.exp(gcol)
    order = range(nch) if d == 0 else range(nch - 1, -1, -1)
    gtot = []
    kd_parts = []
    for n in range(nch):
        end = n * CHUNK + (CHUNK - 1 if d == 0 else 0)
        gt = gcol[end:end + 1, :]
        gtot.append(gt)
        sl = slice(n * CHUNK, (n + 1) * CHUNK)
        kd_parts.append(k[sl] * jnp.exp(gt - gcol[sl]))
    kd_t = jnp.concatenate(kd_parts, axis=0).T.astype(BF16)
    s = s_ref[...]
    vnew = [None] * nch
    ointer = [None] * nch
    zeros = jnp.zeros((CHUNK, DV_C), BF16)
    for n in order:
        sl = slice(n * CHUNK, (n + 1) * CHUNK)
        wq = jnp.concatenate([w[sl], qd[sl]], axis=0).astype(BF16)
        r = _dot(wq, s.astype(BF16))
        vn = u[sl] - r[0:CHUNK]
        ointer[n] = r[CHUNK:2 * CHUNK]
        vnew[n] = vn
        vpad = jnp.concatenate([vn.astype(BF16) if m == n else zeros for m in range(nch)], axis=0)
        s = s * jnp.exp(gtot[n]) + _dot(kd_t, vpad)
    s_ref[...] = s
    vn_all = jnp.concatenate(vnew, axis=0).astype(BF16)
    return jnp.concatenate(ointer, axis=0) + _dot(amat.astype(BF16), vn_all)


def _dot_exact_lhs_f32(a, b_bf16):
    out = None
    for p in _split_bf16(a, GDN_SOL_PASSES):
        t = _dot(p, b_bf16)
        out = t if out is None else out + t
    return out


def _gdn_finalize(o, gate, gain):
    ms = jnp.mean(o * o, axis=-1, keepdims=True)
    return (o * lax.rsqrt(ms + EPS) * gain * _silu(gate)).astype(BF16)


def _gdn_kernel(qf_ref, kf_ref, vf_ref, gf_ref, ogf_ref, qb_ref, kb_ref, vb_ref, gb_ref, ogb_ref,
                s0f_ref, s0b_ref, m_ref, cum_ref, gain_ref,
                y_ref, sfo_ref, sbo_ref, oacc_ref, sf_ref, sb_ref, *, nb):
    i = pl.program_id(2)

    @pl.when(i == 0)
    def _():
        sf_ref[...] = s0f_ref[...]
        sb_ref[...] = s0b_ref[...]

    of = _gdn_block(qf_ref, kf_ref, vf_ref, gf_ref, sf_ref, m_ref, cum_ref, 0)
    ob = _gdn_block(qb_ref, kb_ref, vb_ref, gb_ref, sb_ref, m_ref, cum_ref, 1)
    rows_f = pl.ds(pl.multiple_of(i * GDN_BLOCK, GDN_BLOCK), GDN_BLOCK)
    rows_b = pl.ds(pl.multiple_of((nb - 1 - i) * GDN_BLOCK, GDN_BLOCK), GDN_BLOCK)
    gain = gain_ref[...]

    @pl.when(2 * i < nb - 1)
    def _():
        oacc_ref[rows_f, :] = of
        oacc_ref[rows_b, :] = ob

    if nb % 2 == 1:
        @pl.when(2 * i == nb - 1)
        def _():
            y_ref[rows_f, :] = _gdn_finalize(of + ob, ogf_ref[...], gain)

    @pl.when(2 * i > nb - 1)
    def _():
        y_ref[rows_f, :] = _gdn_finalize(oacc_ref[rows_f, :] + of, ogf_ref[...], gain)
        y_ref[rows_b, :] = _gdn_finalize(oacc_ref[rows_b, :] + ob, ogb_ref[...], gain)

    @pl.when(i == nb - 1)
    def _():
        sfo_ref[...] = sf_ref[...]
        sbo_ref[...] = sb_ref[...]


def _gdn(q, k, v, gb, og, s0f, s0b, masks, cums, gain):
    b, n, _ = q.shape
    nb = n // GDN_BLOCK
    fwd = pl.BlockSpec((None, GDN_BLOCK, DK_C), lambda bb, hh, i: (bb, i, hh))
    bwd = pl.BlockSpec((None, GDN_BLOCK, DK_C), lambda bb, hh, i: (bb, nb - 1 - i, hh))
    st = pl.BlockSpec((None, None, DK_C, DV_C), lambda bb, hh, i: (bb, hh, 0, 0))
    st_shape = jax.ShapeDtypeStruct((b, H_C, DK_C, DV_C), F32)
    return pl.pallas_call(
        functools.partial(_gdn_kernel, nb=nb),
        name="gdn_scan",
        grid=(b, H_C, nb),
        in_specs=[fwd] * 5 + [bwd] * 5 + [
            st, st,
            pl.BlockSpec((8, GDN_BLOCK, GDN_BLOCK), lambda bb, hh, i: (0, 0, 0)),
            pl.BlockSpec((2, GDN_BLOCK, GDN_BLOCK), lambda bb, hh, i: (0, 0, 0)),
            pl.BlockSpec((1, DV_C), lambda bb, hh, i: (0, 0)),
        ],
        out_specs=[pl.BlockSpec((None, n, DV_C), lambda bb, hh, i: (bb, 0, hh)), st, st],
        out_shape=[jax.ShapeDtypeStruct((b, n, D_CV), BF16), st_shape, st_shape],
        scratch_shapes=[pltpu.VMEM((n, DV_C), F32), pltpu.VMEM((DK_C, DV_C), F32), pltpu.VMEM((DK_C, DV_C), F32)],
        compiler_params=_cparams("parallel", "parallel", "arbitrary"),
    )(q, k, v, gb, og, q, k, v, gb, og, s0f, s0b, masks, cums, gain)


def _merge_kernel(x_ref, g_ref, sh_ref, sc_ref, gt_ref, ya_ref, yb_ref, yc_ref, wg_ref, wbr_ref, wo_ref,
                  o_ref, hs_ref, acc_ref):
    j = pl.program_id(2)

    @pl.when(j == 0)
    def _():
        hs_ref[...] = _norm_mod(x_ref[...], g_ref[...], sh_ref[...], sc_ref[...]).astype(BF16)
        acc_ref[...] = jnp.zeros_like(acc_ref)

    h = hs_ref[...]
    z = None
    for br, y_ref in enumerate((ya_ref, yb_ref, yc_ref)):
        t = _sigmoid(_dot(h, wg_ref[br])) * _dot(y_ref[...], wbr_ref[br])
        z = t if z is None else z + t
    acc_ref[...] += _dot(z.astype(BF16), wo_ref[...])

    @pl.when(j == pl.num_programs(2) - 1)
    def _():
        o_ref[...] = x_ref[...] + gt_ref[...] * acc_ref[...]


def _merge(x, g, shift, scale, gate, ya, yb, yc, wg, wbr, wo, tm, tn):
    b, n, d = x.shape
    row = pl.BlockSpec((None, 1, d), lambda bb, i, j: (bb, 0, 0))
    xt = pl.BlockSpec((None, tm, d), lambda bb, i, j: (bb, i, 0))
    yt = pl.BlockSpec((None, tm, D_A), lambda bb, i, j: (bb, i, 0))
    return pl.pallas_call(
        _merge_kernel,
        name="merge_out",
        grid=(b, n // tm, d // tn),
        in_specs=[
            xt, pl.BlockSpec((1, d), lambda bb, i, j: (0, 0)), row, row, row, yt, yt, yt,
            pl.BlockSpec((3, d, tn), lambda bb, i, j: (0, 0, j)),
            pl.BlockSpec((3, D_A, tn), lambda bb, i, j: (0, 0, j)),
            pl.BlockSpec((tn, d), lambda bb, i, j: (j, 0)),
        ],
        out_specs=xt,
        out_shape=jax.ShapeDtypeStruct((b, n, d), F32),
        scratch_shapes=[pltpu.VMEM((tm, d), BF16), pltpu.VMEM((tm, d), F32)],
        compiler_params=_cparams("parallel", "parallel", "arbitrary"),
    )(x, g, shift, scale, gate, ya, yb, yc, wg, wbr, wo)


def _ffn_kernel(xp_ref, x_ref, xn_ref, g_ref, sh_ref, sc_ref, gt_ref, wug_ref, wuv_ref, cwg_ref, cwv_ref,
                cbg_ref, cbv_ref, wd_ref, o_ref, hs_ref, ug_ref, uv_ref, acc_ref, *, tm):
    i = pl.program_id(1)
    j = pl.program_id(2)
    last = pl.num_programs(1) - 1

    @pl.when(j == 0)
    def _():
        g, sh, sc = g_ref[...], sh_ref[...], sc_ref[...]
        hs_ref[0:HALO] = (_norm_mod(xp_ref[...], g, sh, sc) * (i > 0).astype(F32)).astype(BF16)
        hs_ref[HALO:HALO + tm] = _norm_mod(x_ref[...], g, sh, sc).astype(BF16)
        hs_ref[HALO + tm:2 * HALO + tm] = (_norm_mod(xn_ref[...], g, sh, sc) * (i < last).astype(F32)).astype(BF16)
        acc_ref[...] = jnp.zeros_like(acc_ref)

    h = hs_ref[...]
    ug_ref[...] = _dot(h, wug_ref[...])
    uv_ref[...] = _dot(h, wuv_ref[...])
    pad = FFN_CONV // 2
    cg = cbg_ref[...]
    cv = cbv_ref[...]
    for kk in range(FFN_CONV):
        cg = cg + ug_ref[pl.ds(HALO - pad + kk, tm), :] * cwg_ref[kk:kk + 1, :]
        cv = cv + uv_ref[pl.ds(HALO - pad + kk, tm), :] * cwv_ref[kk:kk + 1, :]
    act = (_silu(cg) * cv).astype(BF16)
    acc_ref[...] += _dot(act, wd_ref[...])

    @pl.when(j == pl.num_programs(2) - 1)
    def _():
        o_ref[...] = x_ref[...] + gt_ref[...] * acc_ref[...]


def _ffn(x, g, shift, scale, gate, wup, cw, cb, wdown, tm, tf):
    b, n, d = x.shape
    nf = D_FF // tf
    prev, nxt = _halo_specs(tm, n, d)
    row = pl.BlockSpec((None, 1, d), lambda bb, i, j: (bb, 0, 0))
    xt = pl.BlockSpec((None, tm, d), lambda bb, i, j: (bb, i, 0))
    return pl.pallas_call(
        functools.partial(_ffn_kernel, tm=tm),
        name="conv_ffn",
        grid=(b, n // tm, nf),
        in_specs=[
            prev, xt, nxt, pl.BlockSpec((1, d), lambda bb, i, j: (0, 0)), row, row, row,
            pl.BlockSpec((d, tf), lambda bb, i, j: (0, j)),
            pl.BlockSpec((d, tf), lambda bb, i, j: (0, nf + j)),
            pl.BlockSpec((FFN_CONV, tf), lambda bb, i, j: (0, j)),
            pl.BlockSpec((FFN_CONV, tf), lambda bb, i, j: (0, nf + j)),
            pl.BlockSpec((1, tf), lambda bb, i, j: (0, j)),
            pl.BlockSpec((1, tf), lambda bb, i, j: (0, nf + j)),
            pl.BlockSpec((tf, d), lambda bb, i, j: (j, 0)),
        ],
        out_specs=xt,
        out_shape=jax.ShapeDtypeStruct((b, n, d), F32),
        scratch_shapes=[pltpu.VMEM((tm + 2 * HALO, d), BF16), pltpu.VMEM((tm + 2 * HALO, tf), F32),
                        pltpu.VMEM((tm + 2 * HALO, tf), F32), pltpu.VMEM((tm, d), F32)],
        compiler_params=_cparams("parallel", "parallel", "arbitrary"),
    )(x, x, x, g, shift, scale, gate, wup, wup, cw, cw, cb, cb, wdown)


def _rope_tables(n_tok):
    t = jnp.arange(n_tok)
    row = (t // GRID_W).astype(F32)
    col = (t % GRID_W).astype(F32)
    n_freq = DK_C // 4
    inv = jnp.power(ROPE_BASE, -jnp.arange(n_freq, dtype=F32) / n_freq)
    ar = row[:, None] * inv
    ac = col[:, None] * inv
    cos = jnp.concatenate([jnp.cos(ar), jnp.cos(ar), jnp.cos(ac), jnp.cos(ac)], axis=-1)
    sin = jnp.concatenate([-jnp.sin(ar), jnp.sin(ar), -jnp.sin(ac), jnp.sin(ac)], axis=-1)
    return cos, sin


def _block_ones(n, blk):
    idx = np.arange(n) // blk
    return jnp.asarray((idx[:, None] == idx[None, :]).astype(np.float32), dtype=BF16)


def _heads_to_lanes(y):
    b, h, n, dh = y.shape
    return jnp.transpose(y, (0, 2, 1, 3)).reshape(b, n, h * dh)


def kernel(x, c, ctx, c_ctx, ada_w, ada_b, norm1_g, norm2_g, w_in, conv_a_w, conv_a_b, ln_a_g, ln_a_b, qn_g, kn_g,
           rpb, conv_c_w, a_log, dt_bias, onorm_g, w_branch, w_out, ffn_up, ffn_conv_w, ffn_conv_b, ffn_down):
    batch, n_lat, d = x.shape
    n_ctx = ctx.shape[1]
    depth = ada_w.shape[0]

    cc = jnp.zeros((8, d), F32).at[:batch].set(c).at[batch].set(c_ctx)
    mods = _ada(cc, ada_w, ada_b).reshape(depth, 8, 6, 1, d)

    rope_c, rope_s = _rope_tables(n_lat)
    ones_c = jnp.ones((n_ctx, DK_C), F32)
    zeros_c = jnp.zeros((n_ctx, DK_C), F32)
    gm64 = _block_ones(D_B, DH_B)
    gm128 = _block_ones(D_CQK, DK_C)
    masks_np = _gdn_masks()
    masks = jnp.asarray(masks_np)
    cums = jnp.asarray(np.stack([masks_np[0].T, masks_np[2].T]), dtype=BF16)
    s_zero = jnp.zeros((batch, H_C, DK_C, DV_C), F32)

    off_b = 2 * D_A
    off_c = off_b + 3 * D_B
    off_db = off_c + 3 * D_CQK + D_CV
    off_g = off_db + 4 * H_C

    x_lat, x_ctx = x, ctx
    for l in range(depth):
        ctx_out = l < depth - 1
        m_lat = [mods[l, :batch, s] for s in range(6)]
        m_ctx = [jnp.broadcast_to(mods[l, batch:batch + 1, s], (batch, 1, d)) for s in range(6)]
        g1 = norm1_g[l].reshape(1, d)
        g2 = norm2_g[l].reshape(1, d)
        wl = w_in[l]
        wa = wl[:, :D_A].astype(BF16)
        wag = wl[:, D_A:2 * D_A].astype(BF16)
        wb = wl[:, off_b:off_c].astype(BF16)
        wc = wl[:, off_c:off_db].astype(BF16)
        wdb_cols = wl[:, off_db:off_g].reshape(d, 2, 2, H_C)
        wdb = jnp.zeros((d, H_C, DK_C), F32).at[:, :, :4].set(
            jnp.transpose(wdb_cols, (0, 3, 1, 2)).reshape(d, H_C, 4)).reshape(d, D_CQK).astype(BF16)
        wgates = jnp.transpose(wl[:, off_g:].reshape(d, 3, d), (1, 0, 2)).astype(BF16)
        gbp = jnp.zeros((8, H_C, DK_C), F32)
        gbp = gbp.at[0, :, :2].set(a_log[l].T).at[1, :, :2].set(dt_bias[l].T)
        gbp = gbp.at[2, :, :2].set(1.0).at[3, :, 2:4].set(1.0).reshape(8, D_CQK)
        gq = (jnp.tile(qn_g[l], H_B) * DH_B ** -0.5).reshape(1, D_B)
        gk = jnp.tile(kn_g[l], H_B).reshape(1, D_B)
        bias = _na_bias_table(rpb[l])
        wbr = w_branch[l].astype(BF16)
        wo = w_out[l].astype(BF16)
        wup = ffn_up[l].astype(BF16)
        wdn = ffn_down[l].astype(BF16)
        cwf = ffn_conv_w[l]
        cbf = ffn_conv_b[l].reshape(1, 2 * D_FF)
        gain = onorm_g[l].reshape(1, DV_C)

        qb_c, kb_c, vb_c = _proj_b(x_ctx, g1, m_ctx[0], m_ctx[1], wb, gq, gk, gm64, n_ctx)
        qc_c, kc_c, vc_c, og_c, gb_c = _proj_c(x_ctx, g1, m_ctx[0], m_ctx[1], wc, wdb, conv_c_w[l], ones_c, zeros_c,
                                               gm128, gbp, n_ctx, False)
        yc_c, sf_c, sb_c = _gdn(qc_c, kc_c, vc_c, gb_c, og_c, s_zero, s_zero, masks, cums, gain)

        ya_l = _conv_a(_proj_a(x_lat, g1, m_lat[0], m_lat[1], wa, wag, 512),
                       conv_a_w[l], conv_a_b[l], ln_a_g[l], ln_a_b[l], 256)
        qb_l, kb_l, vb_l = _proj_b(x_lat, g1, m_lat[0], m_lat[1], wb, gq, gk, gm64, 512)
        yb_l = _heads_to_lanes(_na_attention(qb_l, kb_l, vb_l, kb_c, vb_c, bias, 8))
        qc_l, kc_l, vc_l, og_l, gb_l = _proj_c(x_lat, g1, m_lat[0], m_lat[1], wc, wdb, conv_c_w[l], rope_c, rope_s,
                                               gm128, gbp, 512, True)
        yc_l, _, _ = _gdn(qc_l, kc_l, vc_l, gb_l, og_l, sf_c, sb_c, masks, cums, gain)
        x_lat = _merge(x_lat, g1, m_lat[0], m_lat[1], m_lat[2], ya_l, yb_l, yc_l, wgates, wbr, wo, 512, 256)
        x_lat = _ffn(x_lat, g2, m_lat[3], m_lat[4], m_lat[5], wup, cwf, cbf, wdn, 1024, 256)

        if ctx_out:
            ya_c = _conv_a(_proj_a(x_ctx, g1, m_ctx[0], m_ctx[1], wa, wag, n_ctx),
                           conv_a_w[l], conv_a_b[l], ln_a_g[l], ln_a_b[l], n_ctx)
            yb_c = _heads_to_lanes(_ctx_attention(qb_c, kb_c, vb_c))
            x_ctx = _merge(x_ctx, g1, m_ctx[0], m_ctx[1], m_ctx[2], ya_c, yb_c, yc_c, wgates, wbr, wo, n_ctx, 256)
            x_ctx = _ffn(x_ctx, g2, m_ctx[3], m_ctx[4], m_ctx[5], wup, cwf, cbf, wdn, n_ctx, 256)
    return x_lat
```

```python
import functools
import math

import numpy as np
import jax
import jax.numpy as jnp
from jax import lax
from jax.experimental import pallas as pl
from jax.experimental.pallas import tpu as pltpu

F32 = jnp.float32
BF16 = jnp.bfloat16

D_MODEL = 1024
GRID_W = 64
EPS = 1e-6
NEG_INF = -1e30
D_A = 512
CONV_A = 31
H_B = 8
DH_B = 64
D_B = H_B * DH_B
NA_ROWS = 8
NA_COLS = 16
H_C = 4
DK_C = 128
DV_C = 128
D_CQK = H_C * DK_C
D_CV = H_C * DV_C
SHORT_CONV = 4
CHUNK = 64
ROPE_BASE = 10000.0
D_FF = 2816
FFN_CONV = 3

VMEM_LIMIT_BYTES = 48 * 1024 * 1024
HALO = 16
GDN_BLOCK = 256
GDN_SUB = 128


def _cparams(*sem):
    return pltpu.CompilerParams(dimension_semantics=sem, vmem_limit_bytes=VMEM_LIMIT_BYTES)


def _dot(a, b):
    return jnp.dot(a, b, preferred_element_type=F32)


def _dot_nt(a, b):
    return lax.dot_general(a, b, (((1,), (1,)), ((), ())), preferred_element_type=F32)


def _split_bf16(x, n):
    parts = []
    r = x
    for idx in range(n):
        p = r.astype(BF16)
        parts.append(p)
        if idx + 1 < n:
            r = r - p.astype(F32)
    return parts


def _dot_exact_rhs(a, b_bf16, n):
    out = None
    for p in _split_bf16(a, n):
        t = _dot(p, b_bf16)
        out = t if out is None else out + t
    return out


def _dot_exact_lhs(a_bf16, b, n):
    out = None
    for p in _split_bf16(b, n):
        t = _dot(a_bf16, p)
        out = t if out is None else out + t
    return out


def _mm(a, b, passes=1):
    if passes == 1:
        return _dot(a.astype(BF16), b.astype(BF16))
    a_hi, a_lo = _split_bf16(a, 2)
    b_hi, b_lo = _split_bf16(b, 2)
    return _dot(a_hi, b_hi) + (_dot(a_lo, b_hi) + _dot(a_hi, b_lo))


def _sigmoid(x):
    return 1.0 / (1.0 + jnp.exp(-x))


def _silu(x):
    return x * _sigmoid(x)


def _softplus(x):
    return jnp.maximum(x, 0.0) + jnp.log(1.0 + jnp.exp(-jnp.abs(x)))


def _norm_mod(x, g, shift, scale):
    ms = jnp.mean(x * x, axis=-1, keepdims=True)
    y = x * lax.rsqrt(ms + EPS) * g
    return y * (1.0 + scale) + shift


def _halo_specs(tm, n_tok, width):
    per = tm // HALO
    nblk = n_tok // HALO
    prev = pl.BlockSpec((None, HALO, width), lambda b, i, *_: (b, jnp.maximum(i * per - 1, 0), 0))
    nxt = pl.BlockSpec((None, HALO, width), lambda b, i, *_: (b, jnp.minimum((i + 1) * per, nblk - 1), 0))
    return prev, nxt


def _ada_kernel(c_ref, w_ref, b_ref, o_ref):
    s = _silu(c_ref[...])
    o_ref[...] = _mm(s, w_ref[...], 3) + b_ref[...]


def _ada(cc, ada_w, ada_b):
    depth = ada_w.shape[0]
    d = D_MODEL
    return pl.pallas_call(
        _ada_kernel,
        name="ada_mod",
        grid=(depth, 6),
        in_specs=[
            pl.BlockSpec((8, d), lambda l, j: (0, 0)),
            pl.BlockSpec((None, d, d), lambda l, j: (l, 0, j)),
            pl.BlockSpec((None, 1, d), lambda l, j: (l, 0, j)),
        ],
        out_specs=pl.BlockSpec((None, 8, d), lambda l, j: (l, 0, j)),
        out_shape=jax.ShapeDtypeStruct((depth, 8, 6 * d), F32),
        compiler_params=_cparams("arbitrary", "arbitrary"),
    )(cc, ada_w, ada_b.reshape(depth, 1, 6 * d))


def _proj_a_kernel(x_ref, g_ref, sh_ref, sc_ref, wa_ref, wg_ref, o_ref):
    h = _norm_mod(x_ref[...], g_ref[...], sh_ref[...], sc_ref[...]).astype(BF16)
    a = _dot(h, wa_ref[...])
    gate = _dot(h, wg_ref[...])
    o_ref[...] = a * _sigmoid(gate)


def _proj_a(x, g, shift, scale, wa, wg, tm):
    b, n, d = x.shape
    row = pl.BlockSpec((None, 1, d), lambda bb, i: (bb, 0, 0))
    return pl.pallas_call(
        _proj_a_kernel,
        name="proj_a",
        grid=(b, n // tm),
        in_specs=[
            pl.BlockSpec((None, tm, d), lambda bb, i: (bb, i, 0)),
            pl.BlockSpec((1, d), lambda bb, i: (0, 0)),
            row, row,
            pl.BlockSpec((d, D_A), lambda bb, i: (0, 0)),
            pl.BlockSpec((d, D_A), lambda bb, i: (0, 0)),
        ],
        out_specs=pl.BlockSpec((None, tm, D_A), lambda bb, i: (bb, i, 0)),
        out_shape=jax.ShapeDtypeStruct((b, n, D_A), F32),
        compiler_params=_cparams("parallel", "parallel"),
    )(x, g, shift, scale, wa, wg)


def _conv_a_kernel(yp_ref, y_ref, yn_ref, cw_ref, cb_ref, lg_ref, lb_ref, o_ref, ext_ref, *, tm):
    i = pl.program_id(1)
    last = pl.num_programs(1) - 1
    ext_ref[0:HALO] = yp_ref[...] * (i > 0).astype(F32)
    ext_ref[HALO:HALO + tm] = y_ref[...]
    ext_ref[HALO + tm:2 * HALO + tm] = yn_ref[...] * (i < last).astype(F32)
    pad = CONV_A // 2
    acc = jnp.zeros((tm, D_A), F32) + cb_ref[...]
    for k in range(CONV_A):
        acc = acc + ext_ref[pl.ds(HALO - pad + k, tm), :] * cw_ref[k:k + 1, :]
    mu = jnp.mean(acc, axis=-1, keepdims=True)
    cen = acc - mu
    var = jnp.mean(cen * cen, axis=-1, keepdims=True)
    y = cen * lax.rsqrt(var + EPS) * lg_ref[...] + lb_ref[...]
    o_ref[...] = _silu(y).astype(BF16)


def _conv_a(y, cw, cb, lg, lb, tm):
    b, n, c = y.shape
    prev, nxt = _halo_specs(tm, n, c)
    vec = pl.BlockSpec((1, c), lambda bb, i: (0, 0))
    return pl.pallas_call(
        functools.partial(_conv_a_kernel, tm=tm),
        name="conv_a",
        grid=(b, n // tm),
        in_specs=[prev, pl.BlockSpec((None, tm, c), lambda bb, i: (bb, i, 0)), nxt,
                  pl.BlockSpec((CONV_A, c), lambda bb, i: (0, 0)), vec, vec, vec],
        out_specs=pl.BlockSpec((None, tm, c), lambda bb, i: (bb, i, 0)),
        out_shape=jax.ShapeDtypeStruct((b, n, c), BF16),
        scratch_shapes=[pltpu.VMEM((tm + 2 * HALO, c), F32)],
        compiler_params=_cparams("parallel", "parallel"),
    )(y, y, y, cw, cb.reshape(1, c), lg.reshape(1, c), lb.reshape(1, c))


def _proj_b_kernel(x_ref, g_ref, sh_ref, sc_ref, w_ref, gq_ref, gk_ref, gm_ref, q_ref, k_ref, v_ref):
    h = _norm_mod(x_ref[...], g_ref[...], sh_ref[...], sc_ref[...]).astype(BF16)
    for idx, (gain_ref, o_ref) in enumerate(((gq_ref, q_ref), (gk_ref, k_ref), (None, v_ref))):
        acc = _dot(h, w_ref[:, idx * D_B:(idx + 1) * D_B])
        if gain_ref is not None:
            ss = _dot_exact_rhs(acc * acc, gm_ref[...], 2)
            acc = acc * lax.rsqrt(ss * (1.0 / DH_B) + EPS) * gain_ref[...]
        for hh in range(H_B):
            o_ref[hh] = acc[:, hh * DH_B:(hh + 1) * DH_B].astype(BF16)


def _proj_b(x, g, shift, scale, w, gq, gk, gmat, tm):
    b, n, d = x.shape
    row = pl.BlockSpec((None, 1, d), lambda bb, i: (bb, 0, 0))
    vec = pl.BlockSpec((1, D_B), lambda bb, i: (0, 0))
    head_out = pl.BlockSpec((None, H_B, tm, DH_B), lambda bb, i: (bb, 0, i, 0))
    shp = jax.ShapeDtypeStruct((b, H_B, n, DH_B), BF16)
    return pl.pallas_call(
        _proj_b_kernel,
        name="proj_b",
        grid=(b, n // tm),
        in_specs=[
            pl.BlockSpec((None, tm, d), lambda bb, i: (bb, i, 0)),
            pl.BlockSpec((1, d), lambda bb, i: (0, 0)),
            row, row,
            pl.BlockSpec((d, 3 * D_B), lambda bb, i: (0, 0)),
            vec, vec,
            pl.BlockSpec((D_B, D_B), lambda bb, i: (0, 0)),
        ],
        out_specs=[head_out, head_out, head_out],
        out_shape=[shp, shp, shp],
        compiler_params=_cparams("parallel", "parallel"),
    )(x, g, shift, scale, w, gq, gk, gmat)


def _na_kernel(q_ref, k_ref, v_ref, kc_ref, vc_ref, bias_ref, o_ref, *, rb, rows):
    i = pl.program_id(2)
    n_loc = NA_ROWS * GRID_W
    qcol = lax.broadcasted_iota(jnp.int32, (GRID_W, n_loc), 0)
    kcol = lax.broadcasted_iota(jnp.int32, (GRID_W, n_loc), 1) & (GRID_W - 1)
    cs = jnp.clip(qcol - NA_COLS // 2, 0, GRID_W - NA_COLS)
    col_ok = jnp.abs(2 * (kcol - cs) - (NA_COLS - 1)) <= NA_COLS - 1
    rws = [i * rb + rr for rr in range(rb)]
    starts = [jnp.clip(r - NA_ROWS // 2, 0, rows - NA_ROWS) for r in rws]
    variants = [st - r + NA_ROWS - 1 for st, r in zip(starts, rws)]
    tok0 = [pl.multiple_of(st * GRID_W, GRID_W) for st in starts]
    qs = [q_ref[rr * GRID_W:(rr + 1) * GRID_W, :] for rr in range(rb)]
    kc = kc_ref[...]
    vc = vc_ref[...]
    s = [_dot_nt(q, k_ref[pl.ds(t0, n_loc), :]) for q, t0 in zip(qs, tok0)]
    sc = [_dot_nt(q, kc) for q in qs]
    s = [jnp.where(col_ok, a + bias_ref[vr], NEG_INF) for a, vr in zip(s, variants)]
    m = [jnp.maximum(jnp.max(a, axis=-1, keepdims=True), jnp.max(b, axis=-1, keepdims=True)) for a, b in zip(s, sc)]
    p = [jnp.exp(a - mm) for a, mm in zip(s, m)]
    pc = [jnp.exp(b - mm) for b, mm in zip(sc, m)]
    l = [jnp.sum(a, axis=-1, keepdims=True) + jnp.sum(b, axis=-1, keepdims=True) for a, b in zip(p, pc)]
    o = [_dot(a.astype(BF16), v_ref[pl.ds(t0, n_loc), :]) + _dot(b.astype(BF16), vc) for a, b, t0 in zip(p, pc, tok0)]
    for rr in range(rb):
        o_ref[rr * GRID_W:(rr + 1) * GRID_W, :] = (o[rr] / l[rr]).astype(BF16)


def _na_attention(q, k, v, kc, vc, bias, rb):
    b, h, t, dh = q.shape
    n_ctx = kc.shape[2]
    rows = t // GRID_W
    full = pl.BlockSpec((None, None, t, dh), lambda bb, hh, i: (bb, hh, 0, 0))
    cfull = pl.BlockSpec((None, None, n_ctx, dh), lambda bb, hh, i: (bb, hh, 0, 0))
    tile = pl.BlockSpec((None, None, rb * GRID_W, dh), lambda bb, hh, i: (bb, hh, i, 0))
    return pl.pallas_call(
        functools.partial(_na_kernel, rb=rb, rows=rows),
        name="na_attn",
        grid=(b, h, rows // rb),
        in_specs=[tile, full, full, cfull, cfull,
                  pl.BlockSpec((None, NA_ROWS, GRID_W, NA_ROWS * GRID_W), lambda bb, hh, i: (hh, 0, 0, 0))],
        out_specs=tile,
        out_shape=jax.ShapeDtypeStruct((b, h, t, dh), BF16),
        compiler_params=_cparams("parallel", "parallel", "arbitrary"),
    )(q, k, v, kc, vc, bias)


def _ctx_attn_kernel(q_ref, k_ref, v_ref, o_ref):
    s = _dot_nt(q_ref[...], k_ref[...])
    m = jnp.max(s, axis=-1, keepdims=True)
    p = jnp.exp(s - m)
    l = jnp.sum(p, axis=-1, keepdims=True)
    o_ref[...] = (_dot(p.astype(BF16), v_ref[...]) / l).astype(BF16)


def _ctx_attention(q, k, v):
    b, h, n, dh = q.shape
    full = pl.BlockSpec((None, None, n, dh), lambda bb, hh: (bb, hh, 0, 0))
    return pl.pallas_call(
        _ctx_attn_kernel,
        name="ctx_attn",
        grid=(b, h),
        in_specs=[full, full, full],
        out_specs=full,
        out_shape=jax.ShapeDtypeStruct((b, h, n, dh), BF16),
        compiler_params=_cparams("parallel", "parallel"),
    )(q, k, v)


def _na_bias_table(rpb):
    cidx = np.arange(GRID_W)
    dc = np.clip(cidx[None, :] - cidx[:, None] + NA_COLS - 1, 0, 2 * NA_COLS - 2)
    onehot = (dc[None] == np.arange(2 * NA_COLS - 1)[:, None, None]).astype(np.float32)
    toep = jnp.einsum("hrd,dqk->hrqk", rpb, onehot, precision=lax.Precision.HIGHEST)
    tabs = [jnp.transpose(toep[:, v:v + NA_ROWS], (0, 2, 1, 3)).reshape(H_B, GRID_W, NA_ROWS * GRID_W)
            for v in range(NA_ROWS)]
    return jnp.stack(tabs, axis=1)


def _proj_c_kernel(xp_ref, x_ref, xn_ref, g_ref, sh_ref, sc_ref, w_ref, wdb_ref, cw_ref, rc_ref, rs_ref,
                   gm_ref, gbp_ref, q_ref, k_ref, v_ref, og_ref, gb_ref, hs_ref, p_ref, *, tm, use_rope):
    i = pl.program_id(1)
    last = pl.num_programs(1) - 1
    g, sh, sc = g_ref[...], sh_ref[...], sc_ref[...]
    hs_ref[0:HALO] = (_norm_mod(xp_ref[...], g, sh, sc) * (i > 0).astype(F32)).astype(BF16)
    hs_ref[HALO:HALO + tm] = _norm_mod(x_ref[...], g, sh, sc).astype(BF16)
    hs_ref[HALO + tm:2 * HALO + tm] = (_norm_mod(xn_ref[...], g, sh, sc) * (i < last).astype(F32)).astype(BF16)
    pad_l = SHORT_CONV // 2
    lane = lax.broadcasted_iota(jnp.int32, (tm, DK_C), 1)
    first_half = (lane & (DK_C // 2 - 1)) < DK_C // 4
    for sec, o_ref in enumerate((q_ref, k_ref, v_ref)):
        cols = slice(sec * D_CQK, (sec + 1) * D_CQK)
        p_ref[...] = _dot(hs_ref[...], w_ref[:, cols])
        y = None
        for kk in range(SHORT_CONV):
            t = p_ref[pl.ds(HALO - pad_l + kk, tm), :] * cw_ref[kk:kk + 1, cols]
            y = t if y is None else y + t
        y = _silu(y)
        if sec < 2:
            ss = _dot_exact_rhs(y * y, gm_ref[...], 2)
            y = y * lax.rsqrt(ss + EPS)
            if use_rope:
                heads = []
                for hh in range(H_C):
                    yh = y[:, hh * DK_C:(hh + 1) * DK_C]
                    swapped = jnp.where(first_half, pltpu.roll(yh, DK_C - DK_C // 4, axis=1),
                                        pltpu.roll(yh, DK_C // 4, axis=1))
                    heads.append(yh * rc_ref[...] + swapped * rs_ref[...])
                y = jnp.concatenate(heads, axis=1)
            if sec == 0:
                y = y * (DK_C ** -0.5)
        o_ref[...] = y
    hc = hs_ref[HALO:HALO + tm]
    og_ref[...] = _dot(hc, w_ref[:, 3 * D_CQK:3 * D_CQK + D_CV])
    db = _dot(hc, wdb_ref[...])
    gval = -jnp.exp(gbp_ref[0:1, :]) * _softplus(db + gbp_ref[1:2, :])
    gb_ref[...] = gbp_ref[2:3, :] * gval + gbp_ref[3:4, :] * _sigmoid(db)


def _proj_c(x, g, shift, scale, w, wdb, cw, rc, rs, gmat, gbp, tm, use_rope):
    b, n, d = x.shape
    prev, nxt = _halo_specs(tm, n, d)
    row = pl.BlockSpec((None, 1, d), lambda bb, i: (bb, 0, 0))
    tile512 = pl.BlockSpec((None, tm, D_CQK), lambda bb, i: (bb, i, 0))
    shp = jax.ShapeDtypeStruct((b, n, D_CQK), F32)
    rope_spec = pl.BlockSpec((tm, DK_C), lambda bb, i: (i, 0))
    return pl.pallas_call(
        functools.partial(_proj_c_kernel, tm=tm, use_rope=use_rope),
        name="proj_c",
        grid=(b, n // tm),
        in_specs=[
            prev, pl.BlockSpec((None, tm, d), lambda bb, i: (bb, i, 0)), nxt,
            pl.BlockSpec((1, d), lambda bb, i: (0, 0)),
            row, row,
            pl.BlockSpec((d, 3 * D_CQK + D_CV), lambda bb, i: (0, 0)),
            pl.BlockSpec((d, DK_C), lambda bb, i: (0, 0)),
            pl.BlockSpec((SHORT_CONV, 3 * D_CQK), lambda bb, i: (0, 0)),
            rope_spec, rope_spec,
            pl.BlockSpec((D_CQK, D_CQK), lambda bb, i: (0, 0)),
            pl.BlockSpec((8, DK_C), lambda bb, i: (0, 0)),
        ],
        out_specs=[tile512] * 4 + [pl.BlockSpec((None, tm, DK_C), lambda bb, i: (bb, i, 0))],
        out_shape=[shp] * 4 + [jax.ShapeDtypeStruct((b, n, DK_C), F32)],
        scratch_shapes=[pltpu.VMEM((tm + 2 * HALO, d), BF16), pltpu.VMEM((tm + 2 * HALO, D_CQK), F32)],
        compiler_params=_cparams("parallel", "parallel"),
    )(x, x, x, g, shift, scale, w, wdb, cw, rc, rs, gmat, gbp)


def _chunk_masks(n):
    r = np.arange(n)[:, None]
    c = np.arange(n)[None, :]
    same = lambda s: (r // s) == (c // s)
    chunk = same(CHUNK)
    ms = [chunk & (r >= c), chunk & (r > c), chunk & (r <= c), chunk & (r < c), same(8),
          same(16) & ~same(8), same(32) & ~same(16), same(64) & ~same(32)]
    return np.stack(ms).astype(np.float32)


def _tri_inv_all(lmats, eye, m_ref):
    m8 = m_ref[4]
    n0 = [-(l * m8) for l in lmats]
    n2 = [_mm(a, a) for a in n0]
    n4 = [_mm(a, a) for a in n2]
    n3 = [_mm(a, b) for a, b in zip(n0, n2)]
    t1 = [eye + a + b + c for a, b, c in zip(n0, n2, n3)]
    t1n4 = [_mm(a, b) for a, b in zip(t1, n4)]
    t = [a + b for a, b in zip(t1, t1n4)]
    for lvl in (5, 6, 7):
        off = m_ref[lvl]
        lt = [_mm(l * off, a) for l, a in zip(lmats, t)]
        tlt = [_mm(a, b) for a, b in zip(t, lt)]
        t = [a - b for a, b in zip(t, tlt)]
    return t


def _gdn_step(io, m_ref, cum_ref):
    nsub = GDN_BLOCK // GDN_SUB
    csub = GDN_SUB // CHUNK
    nch = nsub * csub
    r16, cols = [], []
    for d in range(2):
        g_t = io[d][3][...].T[0:4 * H_C]
        csum = _dot_exact_rhs(g_t, cum_ref[d], 3)
        rid = lax.broadcasted_iota(jnp.int32, g_t.shape, 0)
        r = jnp.where((rid & 3) == d, csum, g_t)
        r16.append(r)
        cols.append(jnp.concatenate([r, jnp.zeros((DK_C - 4 * H_C, GDN_BLOCK), F32)], axis=0).T)
    probs = [(d, h, sb) for d in range(2) for h in range(H_C) for sb in range(nsub)]
    incl = [m_ref[0], m_ref[2]]
    strict = [m_ref[1], m_ref[3]]
    eye = incl[0] - strict[0]

    def tile(ref, h, sb):
        return ref[sb * GDN_SUB:(sb + 1) * GDN_SUB, h * DK_C:(h + 1) * DK_C]

    q = [tile(io[d][0], h, sb) for d, h, sb in probs]
    k = [tile(io[d][1], h, sb) for d, h, sb in probs]
    v = [tile(io[d][2], h, sb) for d, h, sb in probs]
    gcol, bcol, decay = [], [], []
    for d, h, sb in probs:
        rows = slice(sb * GDN_SUB, (sb + 1) * GDN_SUB)
        c_g = h * 4 + d
        gc = jnp.broadcast_to(cols[d][rows, c_g:c_g + 1], (GDN_SUB, DK_C))
        gcol.append(gc)
        bcol.append(jnp.broadcast_to(cols[d][rows, c_g + 2:c_g + 3], (GDN_SUB, DK_C)))
        decay.append(jnp.exp(jnp.where(incl[d] > 0.5, gc - r16[d][c_g:c_g + 1, rows], NEG_INF)))
    k16 = [a.astype(BF16) for a in k]
    kk = [_dot_nt(a, a) for a in k16]
    qk = [_dot_nt(a.astype(BF16), b) for a, b in zip(q, k16)]
    lmats = [a * b * (c * strict[p[0]]) for a, b, c, p in zip(kk, bcol, decay, probs)]
    amat = [(a * c).astype(BF16) for a, c in zip(qk, decay)]
    tinv = _tri_inv_all(lmats, eye, m_ref)
    eg = [jnp.exp(a) for a in gcol]
    rhs = [jnp.concatenate([vv * b, kx * (b * e)], axis=1).astype(BF16) for vv, kx, b, e in zip(v, k, bcol, eg)]
    sol = [_dot(t.astype(BF16), r) for t, r in zip(tinv, rhs)]
    qd = [a * e for a, e in zip(q, eg)]
    gtot, kd_t = [], []
    for (d, h, sb), kx, gc in zip(probs, k, gcol):
        parts, gts = [], []
        for c in range(csub):
            end = c * CHUNK + (CHUNK - 1 if d == 0 else 0)
            gt = gc[end:end + 1, :]
            gts.append(gt)
            sl = slice(c * CHUNK, (c + 1) * CHUNK)
            parts.append(kx[sl] * jnp.exp(gt - gc[sl]))
        gtot.append(gts)
        kd_t.append(jnp.concatenate(parts, axis=0).T.astype(BF16))
    pidx = {p: n for n, p in enumerate(probs)}
    chains = [(d, h) for d in range(2) for h in range(H_C)]
    state = [io[d][4][h] for d, h in chains]
    vnew = [[None] * nch for _ in chains]
    ointer = [[None] * nch for _ in chains]
    zeros = jnp.zeros((CHUNK, DV_C), BF16)
    for step in range(nch):
        pos = [(step if d == 0 else nch - 1 - step) for d, _ in chains]
        loc = [(pidx[(d, h, n // csub)], n % csub) for (d, h), n in zip(chains, pos)]
        sls = [slice(c * CHUNK, (c + 1) * CHUNK) for _, c in loc]
        wq = [jnp.concatenate([sol[p][sl, DV_C:], qd[p][sl]], axis=0).astype(BF16) for (p, _), sl in zip(loc, sls)]
        r = [_dot(a, s.astype(BF16)) for a, s in zip(wq, state)]
        vn = [(sol[p][sl, :DV_C] - rr[0:CHUNK]).astype(BF16) for (p, _), sl, rr in zip(loc, sls, r)]
        vpad = [jnp.concatenate([x if m == c else zeros for m in range(csub)], axis=0) for x, (_, c) in zip(vn, loc)]
        upd = [_dot(kd_t[p], x) for (p, _), x in zip(loc, vpad)]
        state = [s * jnp.exp(gtot[p][c]) + u for s, (p, c), u in zip(state, loc, upd)]
        for ci, n in enumerate(pos):
            vnew[ci][n] = vn[ci]
            ointer[ci][n] = r[ci][CHUNK:2 * CHUNK]
    for (d, h), s in zip(chains, state):
        io[d][4][h] = s
    out = [[None] * H_C for _ in range(2)]
    for ci, (d, h) in enumerate(chains):
        parts = []
        for sb in range(nsub):
            p = pidx[(d, h, sb)]
            vn_sb = jnp.concatenate(vnew[ci][sb * csub:(sb + 1) * csub], axis=0)
            parts.append(jnp.concatenate(ointer[ci][sb * csub:(sb + 1) * csub], axis=0) + _dot(amat[p], vn_sb))
        out[d][h] = jnp.concatenate(parts, axis=0)
    return out


def _gdn_finalize(o_heads, gate_ref, gain):
    ys = []
    for h, o in enumerate(o_heads):
        ms = jnp.mean(o * o, axis=-1, keepdims=True)
        ys.append(o * lax.rsqrt(ms + EPS) * gain * _silu(gate_ref[:, h * DV_C:(h + 1) * DV_C]))
    return jnp.concatenate(ys, axis=1).astype(BF16)


def _gdn_kernel(qf_ref, kf_ref, vf_ref, gf_ref, ogf_ref, qb_ref, kb_ref, vb_ref, gb_ref, ogb_ref,
                s0f_ref, s0b_ref, m_ref, cum_ref, gain_ref,
                y_ref, sfo_ref, sbo_ref, oacc_ref, sf_ref, sb_ref, *, nb):
    i = pl.program_id(1)

    @pl.when(i == 0)
    def _():
        sf_ref[...] = s0f_ref[...]
        sb_ref[...] = s0b_ref[...]

    of, ob = _gdn_step(((qf_ref, kf_ref, vf_ref, gf_ref, sf_ref), (qb_ref, kb_ref, vb_ref, gb_ref, sb_ref)),
                       m_ref, cum_ref)
    rows_f = pl.ds(pl.multiple_of(i * GDN_BLOCK, GDN_BLOCK), GDN_BLOCK)
    rows_b = pl.ds(pl.multiple_of((nb - 1 - i) * GDN_BLOCK, GDN_BLOCK), GDN_BLOCK)
    gain = gain_ref[...]

    def split(x):
        return [x[:, h * DV_C:(h + 1) * DV_C] for h in range(H_C)]

    @pl.when(2 * i < nb - 1)
    def _():
        oacc_ref[rows_f, :] = jnp.concatenate(of, axis=1)
        oacc_ref[rows_b, :] = jnp.concatenate(ob, axis=1)

    if nb % 2 == 1:
        @pl.when(2 * i == nb - 1)
        def _():
            y_ref[rows_f, :] = _gdn_finalize([a + b for a, b in zip(of, ob)], ogf_ref, gain)

    @pl.when(2 * i > nb - 1)
    def _():
        y_ref[rows_f, :] = _gdn_finalize([a + b for a, b in zip(split(oacc_ref[rows_f, :]), of)], ogf_ref, gain)
        y_ref[rows_b, :] = _gdn_finalize([a + b for a, b in zip(split(oacc_ref[rows_b, :]), ob)], ogb_ref, gain)

    @pl.when(i == nb - 1)
    def _():
        sfo_ref[...] = sf_ref[...]
        sbo_ref[...] = sb_ref[...]


def _gdn(q, k, v, gb, og, s0f, s0b, masks, cums, gain):
    b, n, _ = q.shape
    nb = n // GDN_BLOCK
    fwd = pl.BlockSpec((None, GDN_BLOCK, D_CQK), lambda bb, i: (bb, i, 0))
    bwd = pl.BlockSpec((None, GDN_BLOCK, D_CQK), lambda bb, i: (bb, nb - 1 - i, 0))
    gfwd = pl.BlockSpec((None, GDN_BLOCK, DK_C), lambda bb, i: (bb, i, 0))
    gbwd = pl.BlockSpec((None, GDN_BLOCK, DK_C), lambda bb, i: (bb, nb - 1 - i, 0))
    st = pl.BlockSpec((None, H_C, DK_C, DV_C), lambda bb, i: (bb, 0, 0, 0))
    st_shape = jax.ShapeDtypeStruct((b, H_C, DK_C, DV_C), F32)
    return pl.pallas_call(
        functools.partial(_gdn_kernel, nb=nb),
        name="gdn_scan",
        grid=(b, nb),
        in_specs=[fwd, fwd, fwd, gfwd, fwd, bwd, bwd, bwd, gbwd, bwd,
                  st, st,
                  pl.BlockSpec((8, GDN_SUB, GDN_SUB), lambda bb, i: (0, 0, 0)),
                  pl.BlockSpec((2, GDN_BLOCK, GDN_BLOCK), lambda bb, i: (0, 0, 0)),
                  pl.BlockSpec((1, DV_C), lambda bb, i: (0, 0))],
        out_specs=[pl.BlockSpec((None, n, D_CV), lambda bb, i: (bb, 0, 0)), st, st],
        out_shape=[jax.ShapeDtypeStruct((b, n, D_CV), BF16), st_shape, st_shape],
        scratch_shapes=[pltpu.VMEM((n, D_CV), F32), pltpu.VMEM((H_C, DK_C, DV_C), F32),
                        pltpu.VMEM((H_C, DK_C, DV_C), F32)],
        compiler_params=_cparams("parallel", "arbitrary"),
    )(q, k, v, gb, og, q, k, v, gb, og, s0f, s0b, masks, cums, gain)


def _merge_kernel(x_ref, g_ref, sh_ref, sc_ref, gt_ref, ya_ref, yb_ref, yc_ref, wg_ref, wbr_ref, wo_ref,
                  o_ref, hs_ref, acc_ref):
    j = pl.program_id(2)

    @pl.when(j == 0)
    def _():
        hs_ref[...] = _norm_mod(x_ref[...], g_ref[...], sh_ref[...], sc_ref[...]).astype(BF16)
        acc_ref[...] = jnp.zeros_like(acc_ref)

    h = hs_ref[...]
    z = None
    for br, y_ref in enumerate((ya_ref, yb_ref, yc_ref)):
        t = _sigmoid(_dot(h, wg_ref[br])) * _dot(y_ref[...], wbr_ref[br])
        z = t if z is None else z + t
    acc_ref[...] += _dot(z.astype(BF16), wo_ref[...])

    @pl.when(j == pl.num_programs(2) - 1)
    def _():
        o_ref[...] = x_ref[...] + gt_ref[...] * acc_ref[...]


def _merge(x, g, shift, scale, gate, ya, yb, yc, wg, wbr, wo, tm, tn):
    b, n, d = x.shape
    row = pl.BlockSpec((None, 1, d), lambda bb, i, j: (bb, 0, 0))
    xt = pl.BlockSpec((None, tm, d), lambda bb, i, j: (bb, i, 0))
    yt = pl.BlockSpec((None, tm, D_A), lambda bb, i, j: (bb, i, 0))
    return pl.pallas_call(
        _merge_kernel,
        name="merge_out",
        grid=(b, n // tm, d // tn),
        in_specs=[
            xt, pl.BlockSpec((1, d), lambda bb, i, j: (0, 0)), row, row, row, yt, yt, yt,
            pl.BlockSpec((3, d, tn), lambda bb, i, j: (0, 0, j)),
            pl.BlockSpec((3, D_A, tn), lambda bb, i, j: (0, 0, j)),
            pl.BlockSpec((tn, d), lambda bb, i, j: (j, 0)),
        ],
        out_specs=xt,
        out_shape=jax.ShapeDtypeStruct((b, n, d), F32),
        scratch_shapes=[pltpu.VMEM((tm, d), BF16), pltpu.VMEM((tm, d), F32)],
        compiler_params=_cparams("parallel", "parallel", "arbitrary"),
    )(x, g, shift, scale, gate, ya, yb, yc, wg, wbr, wo)


def _ffn_kernel(xp_ref, x_ref, xn_ref, g_ref, sh_ref, sc_ref, gt_ref, wug_ref, wuv_ref, cwg_ref, cwv_ref,
                cbg_ref, cbv_ref, wd_ref, o_ref, hs_ref, ug_ref, uv_ref, acc_ref, *, tm):
    i = pl.program_id(1)
    j = pl.program_id(2)
    last = pl.num_programs(1) - 1

    @pl.when(j == 0)
    def _():
        g, sh, sc = g_ref[...], sh_ref[...], sc_ref[...]
        hs_ref[0:HALO] = (_norm_mod(xp_ref[...], g, sh, sc) * (i > 0).astype(F32)).astype(BF16)
        hs_ref[HALO:HALO + tm] = _norm_mod(x_ref[...], g, sh, sc).astype(BF16)
        hs_ref[HALO + tm:2 * HALO + tm] = (_norm_mod(xn_ref[...], g, sh, sc) * (i < last).astype(F32)).astype(BF16)
        acc_ref[...] = jnp.zeros_like(acc_ref)

    h = hs_ref[...]
    ug_ref[...] = _dot(h, wug_ref[...])
    uv_ref[...] = _dot(h, wuv_ref[...])
    pad = FFN_CONV // 2
    cg = cbg_ref[...]
    cv = cbv_ref[...]
    for kk in range(FFN_CONV):
        cg = cg + ug_ref[pl.ds(HALO - pad + kk, tm), :] * cwg_ref[kk:kk + 1, :]
        cv = cv + uv_ref[pl.ds(HALO - pad + kk, tm), :] * cwv_ref[kk:kk + 1, :]
    act = (_silu(cg) * cv).astype(BF16)
    acc_ref[...] += _dot(act, wd_ref[...])

    @pl.when(j == pl.num_programs(2) - 1)
    def _():
        o_ref[...] = x_ref[...] + gt_ref[...] * acc_ref[...]


def _ffn(x, g, shift, scale, gate, wup, cw, cb, wdown, tm, tf):
    b, n, d = x.shape
    nf = D_FF // tf
    prev, nxt = _halo_specs(tm, n, d)
    row = pl.BlockSpec((None, 1, d), lambda bb, i, j: (bb, 0, 0))
    xt = pl.BlockSpec((None, tm, d), lambda bb, i, j: (bb, i, 0))
    return pl.pallas_call(
        functools.partial(_ffn_kernel, tm=tm),
        name="conv_ffn",
        grid=(b, n // tm, nf),
        in_specs=[
            prev, xt, nxt, pl.BlockSpec((1, d), lambda bb, i, j: (0, 0)), row, row, row,
            pl.BlockSpec((d, tf), lambda bb, i, j: (0, j)),
            pl.BlockSpec((d, tf), lambda bb, i, j: (0, nf + j)),
            pl.BlockSpec((FFN_CONV, tf), lambda bb, i, j: (0, j)),
            pl.BlockSpec((FFN_CONV, tf), lambda bb, i, j: (0, nf + j)),
            pl.BlockSpec((1, tf), lambda bb, i, j: (0, j)),
            pl.BlockSpec((1, tf), lambda bb, i, j: (0, nf + j)),
            pl.BlockSpec((tf, d), lambda bb, i, j: (j, 0)),
        ],
        out_specs=xt,
        out_shape=jax.ShapeDtypeStruct((b, n, d), F32),
        scratch_shapes=[pltpu.VMEM((tm + 2 * HALO, d), BF16), pltpu.VMEM((tm + 2 * HALO, tf), F32),
                        pltpu.VMEM((tm + 2 * HALO, tf), F32), pltpu.VMEM((tm, d), F32)],
        compiler_params=_cparams("parallel", "parallel", "arbitrary"),
    )(x, x, x, g, shift, scale, gate, wup, wup, cw, cw, cb, cb, wdown)


def _rope_tables(n_tok):
    t = jnp.arange(n_tok)
    row = (t // GRID_W).astype(F32)
    col = (t % GRID_W).astype(F32)
    n_freq = DK_C // 4
    inv = jnp.power(ROPE_BASE, -jnp.arange(n_freq, dtype=F32) / n_freq)
    ar = row[:, None] * inv
    ac = col[:, None] * inv
    cos = jnp.concatenate([jnp.cos(ar), jnp.cos(ar), jnp.cos(ac), jnp.cos(ac)], axis=-1)
    sin = jnp.concatenate([-jnp.sin(ar), jnp.sin(ar), -jnp.sin(ac), jnp.sin(ac)], axis=-1)
    return cos, sin


def _block_ones(n, blk):
    idx = np.arange(n) // blk
    return jnp.asarray((idx[:, None] == idx[None, :]).astype(np.float32), dtype=BF16)


def _heads_to_lanes(y):
    b, h, n, dh = y.shape
    return jnp.transpose(y, (0, 2, 1, 3)).reshape(b, n, h * dh)


def kernel(x, c, ctx, c_ctx, ada_w, ada_b, norm1_g, norm2_g, w_in, conv_a_w, conv_a_b, ln_a_g, ln_a_b, qn_g, kn_g,
           rpb, conv_c_w, a_log, dt_bias, onorm_g, w_branch, w_out, ffn_up, ffn_conv_w, ffn_conv_b, ffn_down):
    batch, n_lat, d = x.shape
    n_ctx = ctx.shape[1]
    depth = ada_w.shape[0]

    cc = jnp.zeros((8, d), F32).at[:batch].set(c).at[batch].set(c_ctx)
    mods = _ada(cc, ada_w, ada_b).reshape(depth, 8, 6, 1, d)

    rope_c, rope_s = _rope_tables(n_lat)
    ones_c = jnp.ones((n_ctx, DK_C), F32)
    zeros_c = jnp.zeros((n_ctx, DK_C), F32)
    gm64 = _block_ones(D_B, DH_B)
    gm128 = _block_ones(D_CQK, DK_C)
    masks = jnp.asarray(_chunk_masks(GDN_SUB))
    blk_masks = _chunk_masks(GDN_BLOCK)
    cums = jnp.asarray(np.stack([blk_masks[0].T, blk_masks[2].T]), dtype=BF16)
    s_zero = jnp.zeros((batch, H_C, DK_C, DV_C), F32)

    off_b = 2 * D_A
    off_c = off_b + 3 * D_B
    off_db = off_c + 3 * D_CQK + D_CV
    off_g = off_db + 4 * H_C

    x_lat, x_ctx = x, ctx
    for l in range(depth):
        ctx_out = l < depth - 1
        m_lat = [mods[l, :batch, s] for s in range(6)]
        m_ctx = [jnp.broadcast_to(mods[l, batch:batch + 1, s], (batch, 1, d)) for s in range(6)]
        g1 = norm1_g[l].reshape(1, d)
        g2 = norm2_g[l].reshape(1, d)
        wl = w_in[l]
        wa = wl[:, :D_A].astype(BF16)
        wag = wl[:, D_A:2 * D_A].astype(BF16)
        wb = wl[:, off_b:off_c].astype(BF16)
        wc = wl[:, off_c:off_db].astype(BF16)
        wdb_cols = wl[:, off_db:off_g].reshape(d, 2, 2, H_C)
        wdb = jnp.zeros((d, DK_C), F32).at[:, :4 * H_C].set(
            jnp.transpose(wdb_cols, (0, 3, 1, 2)).reshape(d, 4 * H_C)).astype(BF16)
        wgates = jnp.transpose(wl[:, off_g:].reshape(d, 3, d), (1, 0, 2)).astype(BF16)
        gbp = jnp.zeros((8, H_C, 4), F32)
        gbp = gbp.at[0, :, :2].set(a_log[l].T).at[1, :, :2].set(dt_bias[l].T)
        gbp = gbp.at[2, :, :2].set(1.0).at[3, :, 2:4].set(1.0)
        gbp = jnp.zeros((8, DK_C), F32).at[:, :4 * H_C].set(gbp.reshape(8, 4 * H_C))
        gq = (jnp.tile(qn_g[l], H_B) * DH_B ** -0.5).reshape(1, D_B)
        gk = jnp.tile(kn_g[l], H_B).reshape(1, D_B)
        bias = _na_bias_table(rpb[l])
        wbr = w_branch[l].astype(BF16)
        wo = w_out[l].astype(BF16)
        wup = ffn_up[l].astype(BF16)
        wdn = ffn_down[l].astype(BF16)
        cwf = ffn_conv_w[l]
        cbf = ffn_conv_b[l].reshape(1, 2 * D_FF)
        gain = onorm_g[l].reshape(1, DV_C)

        qb_c, kb_c, vb_c = _proj_b(x_ctx, g1, m_ctx[0], m_ctx[1], wb, gq, gk, gm64, n_ctx)
        qc_c, kc_c, vc_c, og_c, gb_c = _proj_c(x_ctx, g1, m_ctx[0], m_ctx[1], wc, wdb, conv_c_w[l], ones_c, zeros_c,
                                               gm128, gbp, n_ctx, False)
        yc_c, sf_c, sb_c = _gdn(qc_c, kc_c, vc_c, gb_c, og_c, s_zero, s_zero, masks, cums, gain)

        ya_l = _conv_a(_proj_a(x_lat, g1, m_lat[0], m_lat[1], wa, wag, 512),
                       conv_a_w[l], conv_a_b[l], ln_a_g[l], ln_a_b[l], 256)
        qb_l, kb_l, vb_l = _proj_b(x_lat, g1, m_lat[0], m_lat[1], wb, gq, gk, gm64, 512)
        yb_l = _heads_to_lanes(_na_attention(qb_l, kb_l, vb_l, kb_c, vb_c, bias, 8))
        qc_l, kc_l, vc_l, og_l, gb_l = _proj_c(x_lat, g1, m_lat[0], m_lat[1], wc, wdb, conv_c_w[l], rope_c, rope_s,
                                               gm128, gbp, 512, True)
        yc_l, _, _ = _gdn(qc_l, kc_l, vc_l, gb_l, og_l, sf_c, sb_c, masks, cums, gain)
        x_lat = _merge(x_lat, g1, m_lat[0], m_lat[1], m_lat[2], ya_l, yb_l, yc_l, wgates, wbr, wo, 512, 256)
        x_lat = _ffn(x_lat, g2, m_lat[3], m_lat[4], m_lat[5], wup, cwf, cbf, wdn, 1024, 256)

        if ctx_out:
            ya_c = _conv_a(_proj_a(x_ctx, g1, m_ctx[0], m_ctx[1], wa, wag, n_ctx),
                           conv_a_w[l], conv_a_b[l], ln_a_g[l], ln_a_b[l], n_ctx)
            yb_c = _heads_to_lanes(_ctx_attention(qb_c, kb_c, vb_c))
            x_ctx = _merge(x_ctx, g1, m_ctx[0], m_ctx[1], m_ctx[2], ya_c, yb_c, yc_c, wgates, wbr, wo, n_ctx, 256)
            x_ctx = _ffn(x_ctx, g2, m_ctx[3], m_ctx[4], m_ctx[5], wup, cwf, cbf, wdn, n_ctx, 256)
    return x_lat
```

```python
import functools
import math

import numpy as np
import jax
import jax.numpy as jnp
from jax import lax
from jax.experimental import pallas as pl
from jax.experimental.pallas import tpu as pltpu

F32 = jnp.float32
BF16 = jnp.bfloat16

D_MODEL = 1024
GRID_W = 64
EPS = 1e-6
NEG_INF = -1e30
D_A = 512
CONV_A = 31
H_B = 8
DH_B = 64
D_B = H_B * DH_B
NA_ROWS = 8
NA_COLS = 16
H_C = 4
DK_C = 128
DV_C = 128
D_CQK = H_C * DK_C
D_CV = H_C * DV_C
SHORT_CONV = 4
CHUNK = 64
ROPE_BASE = 10000.0
D_FF = 2816
FFN_CONV = 3

VMEM_LIMIT_BYTES = 48 * 1024 * 1024
HALO = 16
SUBLANES = 8
GDN_BLOCK = 256
GDN_SUB = 128
NA_ROWS_PER_STEP = 8
MERGE_TN = 512
FFN_TF = 256


def _tiles(n_tok):
    cap = lambda t: min(t, n_tok)
    return {"norm": cap(1024), "proj": cap(512), "conv_a": cap(256), "merge": cap(512), "resid": cap(512),
            "ffn": cap(1024)}


def _cparams(*sem):
    return pltpu.CompilerParams(dimension_semantics=sem, vmem_limit_bytes=VMEM_LIMIT_BYTES)


def _dot(a, b):
    return jnp.dot(a, b, preferred_element_type=F32)


def _dot_nt(a, b):
    return lax.dot_general(a, b, (((1,), (1,)), ((), ())), preferred_element_type=F32)


def _split_bf16(x, n):
    parts = []
    r = x
    for idx in range(n):
        p = r.astype(BF16)
        parts.append(p)
        if idx + 1 < n:
            r = r - p.astype(F32)
    return parts


def _dot_exact_rhs(a, b_bf16, n):
    out = None
    for p in _split_bf16(a, n):
        t = _dot(p, b_bf16)
        out = t if out is None else out + t
    return out


def _dot_exact_lhs(a_bf16, b, n):
    out = None
    for p in _split_bf16(b, n):
        t = _dot(a_bf16, p)
        out = t if out is None else out + t
    return out


def _mm(a, b, passes=1):
    if passes == 1:
        return _dot(a.astype(BF16), b.astype(BF16))
    a_hi, a_lo = _split_bf16(a, 2)
    b_hi, b_lo = _split_bf16(b, 2)
    return _dot(a_hi, b_hi) + (_dot(a_lo, b_hi) + _dot(a_hi, b_lo))


def _sigmoid(x):
    return 1.0 / (1.0 + jnp.exp(-x))


def _silu(x):
    return x * _sigmoid(x)


def _softplus(x):
    return jnp.maximum(x, 0.0) + jnp.log(1.0 + jnp.exp(-jnp.abs(x)))


def _norm_mod(x, g, shift, scale):
    ms = jnp.mean(x * x, axis=-1, keepdims=True)
    y = x * lax.rsqrt(ms + EPS) * g
    return y * (1.0 + scale) + shift


def _halo_specs(tm, n_tok, width):
    per = tm // HALO
    nblk = n_tok // HALO
    prev = pl.BlockSpec((None, HALO, width), lambda b, i, *_: (b, jnp.maximum(i * per - 1, 0), 0))
    nxt = pl.BlockSpec((None, HALO, width), lambda b, i, *_: (b, jnp.minimum((i + 1) * per, nblk - 1), 0))
    return prev, nxt


def _ada_kernel(c_ref, w_ref, b_ref, o_ref):
    s = _silu(c_ref[...])
    o_ref[...] = _mm(s, w_ref[...], 3) + b_ref[...]


def _ada(cc, ada_w, ada_b):
    depth = ada_w.shape[0]
    d = D_MODEL
    return pl.pallas_call(
        _ada_kernel,
        name="ada_mod",
        grid=(depth, 6),
        in_specs=[
            pl.BlockSpec((8, d), lambda l, j: (0, 0)),
            pl.BlockSpec((None, d, d), lambda l, j: (l, 0, j)),
            pl.BlockSpec((None, 1, d), lambda l, j: (l, 0, j)),
        ],
        out_specs=pl.BlockSpec((None, 8, d), lambda l, j: (l, 0, j)),
        out_shape=jax.ShapeDtypeStruct((depth, 8, 6 * d), F32),
        compiler_params=_cparams("arbitrary", "arbitrary"),
    )(cc, ada_w, ada_b.reshape(depth, 1, 6 * d))


def _normmod_kernel(x_ref, g_ref, sh_ref, sc_ref, o_ref):
    o_ref[...] = _norm_mod(x_ref[...], g_ref[...], sh_ref[...], sc_ref[...]).astype(BF16)


def _normmod(x, g, shift, scale, tm):
    b, n, d = x.shape
    row = pl.BlockSpec((None, 1, d), lambda bb, i: (bb, 0, 0))
    tile = pl.BlockSpec((None, tm, d), lambda bb, i: (bb, i, 0))
    return pl.pallas_call(
        _normmod_kernel,
        name="norm_mod",
        grid=(b, n // tm),
        in_specs=[tile, pl.BlockSpec((1, d), lambda bb, i: (0, 0)), row, row],
        out_specs=tile,
        out_shape=jax.ShapeDtypeStruct((b, n, d), BF16),
        compiler_params=_cparams("parallel", "parallel"),
    )(x, g, shift, scale)


def _proj_a_kernel(h_ref, wa_ref, wg_ref, o_ref):
    h = h_ref[...]
    a = _dot(h, wa_ref[...])
    gate = _dot(h, wg_ref[...])
    o_ref[...] = a * _sigmoid(gate)


def _proj_a(h, wa, wg, tm):
    b, n, d = h.shape
    return pl.pallas_call(
        _proj_a_kernel,
        name="proj_a",
        grid=(b, n // tm),
        in_specs=[
            pl.BlockSpec((None, tm, d), lambda bb, i: (bb, i, 0)),
            pl.BlockSpec((d, D_A), lambda bb, i: (0, 0)),
            pl.BlockSpec((d, D_A), lambda bb, i: (0, 0)),
        ],
        out_specs=pl.BlockSpec((None, tm, D_A), lambda bb, i: (bb, i, 0)),
        out_shape=jax.ShapeDtypeStruct((b, n, D_A), F32),
        compiler_params=_cparams("parallel", "parallel"),
    )(h, wa, wg)


def _conv_a_kernel(yp_ref, y_ref, yn_ref, cw_ref, cb_ref, lg_ref, lb_ref, o_ref, ext_ref, *, tm):
    i = pl.program_id(1)
    last = pl.num_programs(1) - 1
    ext_ref[0, 0:HALO] = yp_ref[...] * (i > 0).astype(F32)
    ext_ref[0, HALO:HALO + tm] = y_ref[...]
    ext_ref[0, HALO + tm:2 * HALO + tm] = yn_ref[...] * (i < last).astype(F32)
    n_keep = tm + 2 * HALO - SUBLANES
    for r in range(1, SUBLANES):
        ext_ref[r, 0:n_keep] = ext_ref[0, pl.ds(r, n_keep), :]
    pad = CONV_A // 2
    acc = jnp.zeros((tm, D_A), F32) + cb_ref[...]
    for k in range(CONV_A):
        off = HALO - pad + k
        r = off % SUBLANES
        acc = acc + ext_ref[r, pl.ds(off - r, tm), :] * cw_ref[k:k + 1, :]
    mu = jnp.mean(acc, axis=-1, keepdims=True)
    cen = acc - mu
    var = jnp.mean(cen * cen, axis=-1, keepdims=True)
    y = cen * lax.rsqrt(var + EPS) * lg_ref[...] + lb_ref[...]
    o_ref[...] = _silu(y).astype(BF16)


def _conv_a(y, cw, cb, lg, lb, tm):
    b, n, c = y.shape
    prev, nxt = _halo_specs(tm, n, c)
    vec = pl.BlockSpec((1, c), lambda bb, i: (0, 0))
    return pl.pallas_call(
        functools.partial(_conv_a_kernel, tm=tm),
        name="conv_a",
        grid=(b, n // tm),
        in_specs=[prev, pl.BlockSpec((None, tm, c), lambda bb, i: (bb, i, 0)), nxt,
                  pl.BlockSpec((CONV_A, c), lambda bb, i: (0, 0)), vec, vec, vec],
        out_specs=pl.BlockSpec((None, tm, c), lambda bb, i: (bb, i, 0)),
        out_shape=jax.ShapeDtypeStruct((b, n, c), BF16),
        scratch_shapes=[pltpu.VMEM((SUBLANES, tm + 2 * HALO, c), F32)],
        compiler_params=_cparams("parallel", "parallel"),
    )(y, y, y, cw, cb.reshape(1, c), lg.reshape(1, c), lb.reshape(1, c))


def _proj_b_kernel(h_ref, w_ref, gq_ref, gk_ref, gm_ref, q_ref, k_ref, v_ref):
    h = h_ref[...]
    for idx, (gain_ref, o_ref) in enumerate(((gq_ref, q_ref), (gk_ref, k_ref), (None, v_ref))):
        acc = _dot(h, w_ref[:, idx * D_B:(idx + 1) * D_B])
        if gain_ref is not None:
            ss = _dot_exact_rhs(acc * acc, gm_ref[...], 2)
            acc = acc * lax.rsqrt(ss * (1.0 / DH_B) + EPS) * gain_ref[...]
        for hh in range(H_B):
            o_ref[hh] = acc[:, hh * DH_B:(hh + 1) * DH_B].astype(BF16)


def _proj_b(h, w, gq, gk, gmat, tm):
    b, n, d = h.shape
    vec = pl.BlockSpec((1, D_B), lambda bb, i: (0, 0))
    head_out = pl.BlockSpec((None, H_B, tm, DH_B), lambda bb, i: (bb, 0, i, 0))
    shp = jax.ShapeDtypeStruct((b, H_B, n, DH_B), BF16)
    return pl.pallas_call(
        _proj_b_kernel,
        name="proj_b",
        grid=(b, n // tm),
        in_specs=[
            pl.BlockSpec((None, tm, d), lambda bb, i: (bb, i, 0)),
            pl.BlockSpec((d, 3 * D_B), lambda bb, i: (0, 0)),
            vec, vec,
            pl.BlockSpec((D_B, D_B), lambda bb, i: (0, 0)),
        ],
        out_specs=[head_out, head_out, head_out],
        out_shape=[shp, shp, shp],
        compiler_params=_cparams("parallel", "parallel"),
    )(h, w, gq, gk, gmat)


def _na_kernel(q_ref, k_ref, v_ref, kc_ref, vc_ref, bias_ref, o_ref, *, rb, rows):
    i = pl.program_id(2)
    n_loc = NA_ROWS * GRID_W
    qcol = lax.broadcasted_iota(jnp.int32, (GRID_W, n_loc), 0)
    kcol = lax.broadcasted_iota(jnp.int32, (GRID_W, n_loc), 1) & (GRID_W - 1)
    cs = jnp.clip(qcol - NA_COLS // 2, 0, GRID_W - NA_COLS)
    col_ok = jnp.abs(2 * (kcol - cs) - (NA_COLS - 1)) <= NA_COLS - 1
    rws = [i * rb + rr for rr in range(rb)]
    starts = [jnp.clip(r - NA_ROWS // 2, 0, rows - NA_ROWS) for r in rws]
    variants = [st - r + NA_ROWS - 1 for st, r in zip(starts, rws)]
    tok0 = [pl.multiple_of(st * GRID_W, GRID_W) for st in starts]
    qs = [q_ref[rr * GRID_W:(rr + 1) * GRID_W, :] for rr in range(rb)]
    kc = kc_ref[...]
    vc = vc_ref[...]
    s = [_dot_nt(q, k_ref[pl.ds(t0, n_loc), :]) for q, t0 in zip(qs, tok0)]
    sc = [_dot_nt(q, kc) for q in qs]
    s = [jnp.where(col_ok, a + bias_ref[vr], NEG_INF) for a, vr in zip(s, variants)]
    m = [jnp.maximum(jnp.max(a, axis=-1, keepdims=True), jnp.max(b, axis=-1, keepdims=True)) for a, b in zip(s, sc)]
    p = [jnp.exp(a - mm) for a, mm in zip(s, m)]
    pc = [jnp.exp(b - mm) for b, mm in zip(sc, m)]
    l = [jnp.sum(a, axis=-1, keepdims=True) + jnp.sum(b, axis=-1, keepdims=True) for a, b in zip(p, pc)]
    o = [_dot(a.astype(BF16), v_ref[pl.ds(t0, n_loc), :]) + _dot(b.astype(BF16), vc) for a, b, t0 in zip(p, pc, tok0)]
    for rr in range(rb):
        o_ref[rr * GRID_W:(rr + 1) * GRID_W, :] = (o[rr] / l[rr]).astype(BF16)


def _na_attention(q, k, v, kc, vc, bias, rb):
    b, h, t, dh = q.shape
    n_ctx = kc.shape[2]
    rows = t // GRID_W
    full = pl.BlockSpec((None, None, t, dh), lambda bb, hh, i: (bb, hh, 0, 0))
    cfull = pl.BlockSpec((None, None, n_ctx, dh), lambda bb, hh, i: (bb, hh, 0, 0))
    tile = pl.BlockSpec((None, None, rb * GRID_W, dh), lambda bb, hh, i: (bb, hh, i, 0))
    return pl.pallas_call(
        functools.partial(_na_kernel, rb=rb, rows=rows),
        name="na_attn",
        grid=(b, h, rows // rb),
        in_specs=[tile, full, full, cfull, cfull,
                  pl.BlockSpec((None, NA_ROWS, GRID_W, NA_ROWS * GRID_W), lambda bb, hh, i: (hh, 0, 0, 0))],
        out_specs=tile,
        out_shape=jax.ShapeDtypeStruct((b, h, t, dh), BF16),
        compiler_params=_cparams("parallel", "parallel", "arbitrary"),
    )(q, k, v, kc, vc, bias)


def _ctx_attn_kernel(q_ref, k_ref, v_ref, o_ref):
    s = _dot_nt(q_ref[...], k_ref[...])
    m = jnp.max(s, axis=-1, keepdims=True)
    p = jnp.exp(s - m)
    l = jnp.sum(p, axis=-1, keepdims=True)
    o_ref[...] = (_dot(p.astype(BF16), v_ref[...]) / l).astype(BF16)


def _ctx_attention(q, k, v):
    b, h, n, dh = q.shape
    full = pl.BlockSpec((None, None, n, dh), lambda bb, hh: (bb, hh, 0, 0))
    return pl.pallas_call(
        _ctx_attn_kernel,
        name="ctx_attn",
        grid=(b, h),
        in_specs=[full, full, full],
        out_specs=full,
        out_shape=jax.ShapeDtypeStruct((b, h, n, dh), BF16),
        compiler_params=_cparams("parallel", "parallel"),
    )(q, k, v)


def _na_bias_table(rpb):
    cidx = np.arange(GRID_W)
    dc = np.clip(cidx[None, :] - cidx[:, None] + NA_COLS - 1, 0, 2 * NA_COLS - 2)
    onehot = (dc[None] == np.arange(2 * NA_COLS - 1)[:, None, None]).astype(np.float32)
    toep = jnp.einsum("hrd,dqk->hrqk", rpb, onehot, precision=lax.Precision.HIGHEST)
    tabs = [jnp.transpose(toep[:, v:v + NA_ROWS], (0, 2, 1, 3)).reshape(H_B, GRID_W, NA_ROWS * GRID_W)
            for v in range(NA_ROWS)]
    return jnp.stack(tabs, axis=1)


def _proj_c_kernel(hp_ref, h_ref, hn_ref, w_ref, wdb_ref, cw_ref, rc_ref, rs_ref,
                   gm_ref, gbp_ref, q_ref, k_ref, v_ref, og_ref, gb_ref, p_ref, *, tm, use_rope):
    i = pl.program_id(1)
    last = pl.num_programs(1) - 1
    hp = jnp.where(i > 0, hp_ref[...], jnp.zeros_like(hp_ref))
    hn = jnp.where(i < last, hn_ref[...], jnp.zeros_like(hn_ref))
    hc = h_ref[...]
    pad_l = SHORT_CONV // 2
    lane = lax.broadcasted_iota(jnp.int32, (tm, DK_C), 1)
    first_half = (lane & (DK_C // 2 - 1)) < DK_C // 4
    for sec, o_ref in enumerate((q_ref, k_ref, v_ref)):
        cols = slice(sec * D_CQK, (sec + 1) * D_CQK)
        p_ref[0:HALO] = _dot(hp, w_ref[:, cols])
        p_ref[HALO:HALO + tm] = _dot(hc, w_ref[:, cols])
        p_ref[HALO + tm:2 * HALO + tm] = _dot(hn, w_ref[:, cols])
        y = None
        for kk in range(SHORT_CONV):
            t = p_ref[pl.ds(HALO - pad_l + kk, tm), :] * cw_ref[kk:kk + 1, cols]
            y = t if y is None else y + t
        y = _silu(y)
        if sec < 2:
            ss = _dot_exact_rhs(y * y, gm_ref[...], 2)
            y = y * lax.rsqrt(ss + EPS)
            if use_rope:
                heads = []
                for hh in range(H_C):
                    yh = y[:, hh * DK_C:(hh + 1) * DK_C]
                    swapped = jnp.where(first_half, pltpu.roll(yh, DK_C - DK_C // 4, axis=1),
                                        pltpu.roll(yh, DK_C // 4, axis=1))
                    heads.append(yh * rc_ref[...] + swapped * rs_ref[...])
                y = jnp.concatenate(heads, axis=1)
            if sec == 0:
                y = y * (DK_C ** -0.5)
        o_ref[...] = y
    og_ref[...] = _dot(hc, w_ref[:, 3 * D_CQK:3 * D_CQK + D_CV])
    db = _dot(hc, wdb_ref[...])
    gval = -jnp.exp(gbp_ref[0:1, :]) * _softplus(db + gbp_ref[1:2, :])
    gb_ref[...] = gbp_ref[2:3, :] * gval + gbp_ref[3:4, :] * _sigmoid(db)


def _proj_c(h, w, wdb, cw, rc, rs, gmat, gbp, tm, use_rope):
    b, n, d = h.shape
    prev, nxt = _halo_specs(tm, n, d)
    tile512 = pl.BlockSpec((None, tm, D_CQK), lambda bb, i: (bb, i, 0))
    shp = jax.ShapeDtypeStruct((b, n, D_CQK), F32)
    rope_spec = pl.BlockSpec((tm, DK_C), lambda bb, i: (i, 0))
    return pl.pallas_call(
        functools.partial(_proj_c_kernel, tm=tm, use_rope=use_rope),
        name="proj_c",
        grid=(b, n // tm),
        in_specs=[
            prev, pl.BlockSpec((None, tm, d), lambda bb, i: (bb, i, 0)), nxt,
            pl.BlockSpec((d, 3 * D_CQK + D_CV), lambda bb, i: (0, 0)),
            pl.BlockSpec((d, DK_C), lambda bb, i: (0, 0)),
            pl.BlockSpec((SHORT_CONV, 3 * D_CQK), lambda bb, i: (0, 0)),
            rope_spec, rope_spec,
            pl.BlockSpec((D_CQK, D_CQK), lambda bb, i: (0, 0)),
            pl.BlockSpec((8, DK_C), lambda bb, i: (0, 0)),
        ],
        out_specs=[tile512] * 4 + [pl.BlockSpec((None, tm, DK_C), lambda bb, i: (bb, i, 0))],
        out_shape=[shp] * 4 + [jax.ShapeDtypeStruct((b, n, DK_C), F32)],
        scratch_shapes=[pltpu.VMEM((tm + 2 * HALO, D_CQK), F32)],
        compiler_params=_cparams("parallel", "parallel"),
    )(h, h, h, w, wdb, cw, rc, rs, gmat, gbp)


def _chunk_masks(n):
    r = np.arange(n)[:, None]
    c = np.arange(n)[None, :]
    same = lambda s: (r // s) == (c // s)
    chunk = same(CHUNK)
    ms = [chunk & (r >= c), chunk & (r > c), chunk & (r <= c), chunk & (r < c), same(8),
          same(16) & ~same(8), same(32) & ~same(16), same(64) & ~same(32)]
    return np.stack(ms).astype(np.float32)


def _tri_inv_all(lmats, eye, m_ref):
    m8 = m_ref[4]
    n0 = [-(l * m8) for l in lmats]
    n2 = [_mm(a, a) for a in n0]
    n4 = [_mm(a, a) for a in n2]
    n3 = [_mm(a, b) for a, b in zip(n0, n2)]
    t1 = [eye + a + b + c for a, b, c in zip(n0, n2, n3)]
    t1n4 = [_mm(a, b) for a, b in zip(t1, n4)]
    t = [a + b for a, b in zip(t1, t1n4)]
    for lvl in (5, 6, 7):
        off = m_ref[lvl]
        lt = [_mm(l * off, a) for l, a in zip(lmats, t)]
        tlt = [_mm(a, b) for a, b in zip(t, lt)]
        t = [a - b for a, b in zip(t, tlt)]
    return t


def _gdn_step(io, m_ref, cum_ref):
    nsub = GDN_BLOCK // GDN_SUB
    csub = GDN_SUB // CHUNK
    nch = nsub * csub
    r16, cols = [], []
    for d in range(2):
        g_t = io[d][3][...].T[0:4 * H_C]
        csum = _dot_exact_rhs(g_t, cum_ref[d], 3)
        rid = lax.broadcasted_iota(jnp.int32, g_t.shape, 0)
        r = jnp.where((rid & 3) == d, csum, g_t)
        r16.append(r)
        cols.append(jnp.concatenate([r, jnp.zeros((DK_C - 4 * H_C, GDN_BLOCK), F32)], axis=0).T)
    probs = [(d, h, sb) for d in range(2) for h in range(H_C) for sb in range(nsub)]
    incl = [m_ref[0], m_ref[2]]
    strict = [m_ref[1], m_ref[3]]
    eye = incl[0] - strict[0]

    def tile(ref, h, sb):
        return ref[sb * GDN_SUB:(sb + 1) * GDN_SUB, h * DK_C:(h + 1) * DK_C]

    q = [tile(io[d][0], h, sb) for d, h, sb in probs]
    k = [tile(io[d][1], h, sb) for d, h, sb in probs]
    v = [tile(io[d][2], h, sb) for d, h, sb in probs]
    gcol, bcol, decay = [], [], []
    for d, h, sb in probs:
        rows = slice(sb * GDN_SUB, (sb + 1) * GDN_SUB)
        c_g = h * 4 + d
        gc = jnp.broadcast_to(cols[d][rows, c_g:c_g + 1], (GDN_SUB, DK_C))
        gcol.append(gc)
        bcol.append(jnp.broadcast_to(cols[d][rows, c_g + 2:c_g + 3], (GDN_SUB, DK_C)))
        decay.append(jnp.exp(jnp.where(incl[d] > 0.5, gc - r16[d][c_g:c_g + 1, rows], NEG_INF)))
    k16 = [a.astype(BF16) for a in k]
    kk = [_dot_nt(a, a) for a in k16]
    qk = [_dot_nt(a.astype(BF16), b) for a, b in zip(q, k16)]
    lmats = [a * b * (c * strict[p[0]]) for a, b, c, p in zip(kk, bcol, decay, probs)]
    amat = [(a * c).astype(BF16) for a, c in zip(qk, decay)]
    tinv = _tri_inv_all(lmats, eye, m_ref)
    eg = [jnp.exp(a) for a in gcol]
    rhs = [jnp.concatenate([vv * b, kx * (b * e)], axis=1).astype(BF16) for vv, kx, b, e in zip(v, k, bcol, eg)]
    sol = [_dot(t.astype(BF16), r) for t, r in zip(tinv, rhs)]
    qd = [a * e for a, e in zip(q, eg)]
    gtot, kd_t = [], []
    for (d, h, sb), kx, gc in zip(probs, k, gcol):
        parts, gts = [], []
        for c in range(csub):
            end = c * CHUNK + (CHUNK - 1 if d == 0 else 0)
            gt = gc[end:end + 1, :]
            gts.append(gt)
            sl = slice(c * CHUNK, (c + 1) * CHUNK)
            parts.append(kx[sl] * jnp.exp(gt - gc[sl]))
        gtot.append(gts)
        kd_t.append(jnp.concatenate(parts, axis=0).T.astype(BF16))
    pidx = {p: n for n, p in enumerate(probs)}
    chains = [(d, h) for d in range(2) for h in range(H_C)]
    state = [io[d][4][h] for d, h in chains]
    vnew = [[None] * nch for _ in chains]
    ointer = [[None] * nch for _ in chains]
    zeros = jnp.zeros((CHUNK, DV_C), BF16)
    for step in range(nch):
        pos = [(step if d == 0 else nch - 1 - step) for d, _ in chains]
        loc = [(pidx[(d, h, n // csub)], n % csub) for (d, h), n in zip(chains, pos)]
        sls = [slice(c * CHUNK, (c + 1) * CHUNK) for _, c in loc]
        wq = [jnp.concatenate([sol[p][sl, DV_C:], qd[p][sl]], axis=0).astype(BF16) for (p, _), sl in zip(loc, sls)]
        r = [_dot(a, s.astype(BF16)) for a, s in zip(wq, state)]
        vn = [(sol[p][sl, :DV_C] - rr[0:CHUNK]).astype(BF16) for (p, _), sl, rr in zip(loc, sls, r)]
        vpad = [jnp.concatenate([x if m == c else zeros for m in range(csub)], axis=0) for x, (_, c) in zip(vn, loc)]
        upd = [_dot(kd_t[p], x) for (p, _), x in zip(loc, vpad)]
        state = [s * jnp.exp(gtot[p][c]) + u for s, (p, c), u in zip(state, loc, upd)]
        for ci, n in enumerate(pos):
            vnew[ci][n] = vn[ci]
            ointer[ci][n] = r[ci][CHUNK:2 * CHUNK]
    for (d, h), s in zip(chains, state):
        io[d][4][h] = s
    out = [[None] * H_C for _ in range(2)]
    for ci, (d, h) in enumerate(chains):
        parts = []
        for sb in range(nsub):
            p = pidx[(d, h, sb)]
            vn_sb = jnp.concatenate(vnew[ci][sb * csub:(sb + 1) * csub], axis=0)
            parts.append(jnp.concatenate(ointer[ci][sb * csub:(sb + 1) * csub], axis=0) + _dot(amat[p], vn_sb))
        out[d][h] = jnp.concatenate(parts, axis=0)
    return out


def _gdn_finalize(o_heads, gate_ref, gain):
    ys = []
    for h, o in enumerate(o_heads):
        ms = jnp.mean(o * o, axis=-1, keepdims=True)
        ys.append(o * lax.rsqrt(ms + EPS) * gain * _silu(gate_ref[:, h * DV_C:(h + 1) * DV_C]))
    return jnp.concatenate(ys, axis=1).astype(BF16)


def _gdn_kernel(qf_ref, kf_ref, vf_ref, gf_ref, ogf_ref, qb_ref, kb_ref, vb_ref, gb_ref, ogb_ref,
                s0f_ref, s0b_ref, m_ref, cum_ref, gain_ref,
                y_ref, sfo_ref, sbo_ref, oacc_ref, sf_ref, sb_ref, *, nb):
    i = pl.program_id(1)

    @pl.when(i == 0)
    def _():
        sf_ref[...] = s0f_ref[...]
        sb_ref[...] = s0b_ref[...]

    of, ob = _gdn_step(((qf_ref, kf_ref, vf_ref, gf_ref, sf_ref), (qb_ref, kb_ref, vb_ref, gb_ref, sb_ref)),
                       m_ref, cum_ref)
    rows_f = pl.ds(pl.multiple_of(i * GDN_BLOCK, GDN_BLOCK), GDN_BLOCK)
    rows_b = pl.ds(pl.multiple_of((nb - 1 - i) * GDN_BLOCK, GDN_BLOCK), GDN_BLOCK)
    gain = gain_ref[...]

    def split(x):
        return [x[:, h * DV_C:(h + 1) * DV_C] for h in range(H_C)]

    @pl.when(2 * i < nb - 1)
    def _():
        oacc_ref[rows_f, :] = jnp.concatenate(of, axis=1)
        oacc_ref[rows_b, :] = jnp.concatenate(ob, axis=1)

    if nb % 2 == 1:
        @pl.when(2 * i == nb - 1)
        def _():
            y_ref[rows_f, :] = _gdn_finalize([a + b for a, b in zip(of, ob)], ogf_ref, gain)

    @pl.when(2 * i > nb - 1)
    def _():
        y_ref[rows_f, :] = _gdn_finalize([a + b for a, b in zip(split(oacc_ref[rows_f, :]), of)], ogf_ref, gain)
        y_ref[rows_b, :] = _gdn_finalize([a + b for a, b in zip(split(oacc_ref[rows_b, :]), ob)], ogb_ref, gain)

    @pl.when(i == nb - 1)
    def _():
        sfo_ref[...] = sf_ref[...]
        sbo_ref[...] = sb_ref[...]


def _gdn(q, k, v, gb, og, s0f, s0b, masks, cums, gain):
    b, n, _ = q.shape
    nb = n // GDN_BLOCK
    fwd = pl.BlockSpec((None, GDN_BLOCK, D_CQK), lambda bb, i: (bb, i, 0))
    bwd = pl.BlockSpec((None, GDN_BLOCK, D_CQK), lambda bb, i: (bb, nb - 1 - i, 0))
    gfwd = pl.BlockSpec((None, GDN_BLOCK, DK_C), lambda bb, i: (bb, i, 0))
    gbwd = pl.BlockSpec((None, GDN_BLOCK, DK_C), lambda bb, i: (bb, nb - 1 - i, 0))
    st = pl.BlockSpec((None, H_C, DK_C, DV_C), lambda bb, i: (bb, 0, 0, 0))
    st_shape = jax.ShapeDtypeStruct((b, H_C, DK_C, DV_C), F32)
    return pl.pallas_call(
        functools.partial(_gdn_kernel, nb=nb),
        name="gdn_scan",
        grid=(b, nb),
        in_specs=[fwd, fwd, fwd, gfwd, fwd, bwd, bwd, bwd, gbwd, bwd,
                  st, st,
                  pl.BlockSpec((8, GDN_SUB, GDN_SUB), lambda bb, i: (0, 0, 0)),
                  pl.BlockSpec((2, GDN_BLOCK, GDN_BLOCK), lambda bb, i: (0, 0, 0)),
                  pl.BlockSpec((1, DV_C), lambda bb, i: (0, 0))],
        out_specs=[pl.BlockSpec((None, n, D_CV), lambda bb, i: (bb, 0, 0)), st, st],
        out_shape=[jax.ShapeDtypeStruct((b, n, D_CV), BF16), st_shape, st_shape],
        scratch_shapes=[pltpu.VMEM((n, D_CV), F32), pltpu.VMEM((H_C, DK_C, DV_C), F32),
                        pltpu.VMEM((H_C, DK_C, DV_C), F32)],
        compiler_params=_cparams("parallel", "arbitrary"),
    )(q, k, v, gb, og, q, k, v, gb, og, s0f, s0b, masks, cums, gain)


def _merge_kernel(h_ref, ya_ref, yb_ref, yc_ref, wg_ref, wbr_ref, z_ref):
    h = h_ref[...]
    gates = [_dot(h, wg_ref[br]) for br in range(3)]
    vals = [_dot(y_ref[...], wbr_ref[br]) for br, y_ref in enumerate((ya_ref, yb_ref, yc_ref))]
    z = None
    for gt, vl in zip(gates, vals):
        t = _sigmoid(gt) * vl
        z = t if z is None else z + t
    z_ref[...] = z.astype(BF16)


def _merge(h, ya, yb, yc, wg, wbr, tm, tn):
    b, n, d = h.shape
    yt = pl.BlockSpec((None, tm, D_A), lambda bb, i, j: (bb, i, 0))
    return pl.pallas_call(
        _merge_kernel,
        name="merge_gate",
        grid=(b, n // tm, d // tn),
        in_specs=[
            pl.BlockSpec((None, tm, d), lambda bb, i, j: (bb, i, 0)), yt, yt, yt,
            pl.BlockSpec((3, d, tn), lambda bb, i, j: (0, 0, j)),
            pl.BlockSpec((3, D_A, tn), lambda bb, i, j: (0, 0, j)),
        ],
        out_specs=pl.BlockSpec((None, tm, tn), lambda bb, i, j: (bb, i, j)),
        out_shape=jax.ShapeDtypeStruct((b, n, d), BF16),
        compiler_params=_cparams("parallel", "parallel", "parallel"),
    )(h, ya, yb, yc, wg, wbr)


def _resid_kernel(x_ref, a_ref, gt_ref, w_ref, *rest, with_norm):
    xn = x_ref[...] + gt_ref[...] * _dot(a_ref[...], w_ref[...])
    if with_norm:
        g_ref, sh_ref, sc_ref, o_ref, h_ref = rest
        h_ref[...] = _norm_mod(xn, g_ref[...], sh_ref[...], sc_ref[...]).astype(BF16)
    else:
        (o_ref,) = rest
    o_ref[...] = xn


def _resid_mm(x, a, gate, w, norm, tm):
    b, n, d = x.shape
    kdim = a.shape[-1]
    row = pl.BlockSpec((None, 1, d), lambda bb, i: (bb, 0, 0))
    xt = pl.BlockSpec((None, tm, d), lambda bb, i: (bb, i, 0))
    in_specs = [xt, pl.BlockSpec((None, tm, kdim), lambda bb, i: (bb, i, 0)), row,
                pl.BlockSpec((kdim, d), lambda bb, i: (0, 0))]
    args = [x, a, gate, w]
    out_specs, out_shape = [xt], [jax.ShapeDtypeStruct((b, n, d), F32)]
    if norm is not None:
        in_specs += [pl.BlockSpec((1, d), lambda bb, i: (0, 0)), row, row]
        args += list(norm)
        out_specs.append(xt)
        out_shape.append(jax.ShapeDtypeStruct((b, n, d), BF16))
    res = pl.pallas_call(
        functools.partial(_resid_kernel, with_norm=norm is not None),
        name="resid_mm",
        grid=(b, n // tm),
        in_specs=in_specs,
        out_specs=out_specs,
        out_shape=out_shape,
        compiler_params=_cparams("parallel", "parallel"),
    )(*args)
    return (res[0], res[1]) if norm is not None else (res[0], None)


def _ffn_up_kernel(hp_ref, h_ref, hn_ref, wug_ref, wuv_ref, cwg_ref, cwv_ref, cbg_ref, cbv_ref, o_ref,
                   ug_ref, uv_ref, *, tm):
    i = pl.program_id(1)
    last = pl.num_programs(1) - 1
    hp = jnp.where(i > 0, hp_ref[...], jnp.zeros_like(hp_ref))
    hn = jnp.where(i < last, hn_ref[...], jnp.zeros_like(hn_ref))
    hc = h_ref[...]
    for w_ref, u_ref in ((wug_ref, ug_ref), (wuv_ref, uv_ref)):
        u_ref[0:HALO] = _dot(hp, w_ref[...])
        u_ref[HALO:HALO + tm] = _dot(hc, w_ref[...])
        u_ref[HALO + tm:2 * HALO + tm] = _dot(hn, w_ref[...])
    pad = FFN_CONV // 2
    cg = cbg_ref[...]
    cv = cbv_ref[...]
    n_ext = tm + 2 * HALO
    ug = ug_ref[...]
    uv = uv_ref[...]
    for kk in range(FFN_CONV):
        sh = (pad - kk) % n_ext
        ugk = ug if sh == 0 else pltpu.roll(ug, sh, axis=0)
        uvk = uv if sh == 0 else pltpu.roll(uv, sh, axis=0)
        cg = cg + ugk[HALO:HALO + tm] * cwg_ref[kk:kk + 1, :]
        cv = cv + uvk[HALO:HALO + tm] * cwv_ref[kk:kk + 1, :]
    o_ref[...] = (_silu(cg) * cv).astype(BF16)


def _ffn_up(h, wup, cw, cb, tm, tf):
    b, n, d = h.shape
    nf = D_FF // tf
    prev, nxt = _halo_specs(tm, n, d)
    return pl.pallas_call(
        functools.partial(_ffn_up_kernel, tm=tm),
        name="ffn_up",
        grid=(b, n // tm, nf),
        in_specs=[
            prev, pl.BlockSpec((None, tm, d), lambda bb, i, j: (bb, i, 0)), nxt,
            pl.BlockSpec((d, tf), lambda bb, i, j: (0, j)),
            pl.BlockSpec((d, tf), lambda bb, i, j: (0, nf + j)),
            pl.BlockSpec((FFN_CONV, tf), lambda bb, i, j: (0, j)),
            pl.BlockSpec((FFN_CONV, tf), lambda bb, i, j: (0, nf + j)),
            pl.BlockSpec((1, tf), lambda bb, i, j: (0, j)),
            pl.BlockSpec((1, tf), lambda bb, i, j: (0, nf + j)),
        ],
        out_specs=pl.BlockSpec((None, tm, tf), lambda bb, i, j: (bb, i, j)),
        out_shape=jax.ShapeDtypeStruct((b, n, D_FF), BF16),
        scratch_shapes=[pltpu.VMEM((tm + 2 * HALO, tf), F32), pltpu.VMEM((tm + 2 * HALO, tf), F32)],
        compiler_params=_cparams("parallel", "parallel", "parallel"),
    )(h, h, h, wup, wup, cw, cw, cb, cb)


def _rope_tables(n_tok):
    t = jnp.arange(n_tok)
    row = (t // GRID_W).astype(F32)
    col = (t % GRID_W).astype(F32)
    n_freq = DK_C // 4
    inv = jnp.power(ROPE_BASE, -jnp.arange(n_freq, dtype=F32) / n_freq)
    ar = row[:, None] * inv
    ac = col[:, None] * inv
    cos = jnp.concatenate([jnp.cos(ar), jnp.cos(ar), jnp.cos(ac), jnp.cos(ac)], axis=-1)
    sin = jnp.concatenate([-jnp.sin(ar), jnp.sin(ar), -jnp.sin(ac), jnp.sin(ac)], axis=-1)
    return cos, sin


def _block_ones(n, blk):
    idx = np.arange(n) // blk
    return jnp.asarray((idx[:, None] == idx[None, :]).astype(np.float32), dtype=BF16)


def _heads_to_lanes(y):
    b, h, n, dh = y.shape
    return jnp.transpose(y, (0, 2, 1, 3)).reshape(b, n, h * dh)


def kernel(x, c, ctx, c_ctx, ada_w, ada_b, norm1_g, norm2_g, w_in, conv_a_w, conv_a_b, ln_a_g, ln_a_b, qn_g, kn_g,
           rpb, conv_c_w, a_log, dt_bias, onorm_g, w_branch, w_out, ffn_up, ffn_conv_w, ffn_conv_b, ffn_down):
    batch, n_lat, d = x.shape
    n_ctx = ctx.shape[1]
    depth = ada_w.shape[0]

    cc = jnp.zeros((8, d), F32).at[:batch].set(c).at[batch].set(c_ctx)
    mods = _ada(cc, ada_w, ada_b).reshape(depth, 8, 6, 1, d)

    rope_c, rope_s = _rope_tables(n_lat)
    ones_c = jnp.ones((n_ctx, DK_C), F32)
    zeros_c = jnp.zeros((n_ctx, DK_C), F32)
    gm64 = _block_ones(D_B, DH_B)
    gm128 = _block_ones(D_CQK, DK_C)
    masks = jnp.asarray(_chunk_masks(GDN_SUB))
    blk_masks = _chunk_masks(GDN_BLOCK)
    cums = jnp.asarray(np.stack([blk_masks[0].T, blk_masks[2].T]), dtype=BF16)
    s_zero = jnp.zeros((batch, H_C, DK_C, DV_C), F32)

    off_b = 2 * D_A
    off_c = off_b + 3 * D_B
    off_db = off_c + 3 * D_CQK + D_CV
    off_g = off_db + 4 * H_C

    def layer_mods(l):
        m_lat = [mods[l, :batch, j] for j in range(6)]
        m_ctx = [jnp.broadcast_to(mods[l, batch:batch + 1, j], (batch, 1, d)) for j in range(6)]
        return m_lat, m_ctx

    tl = _tiles(n_lat)
    tc = _tiles(n_ctx)
    x_lat, x_ctx = x, ctx
    m_lat, m_ctx = layer_mods(0)
    h_lat = _normmod(x_lat, norm1_g[0].reshape(1, d), m_lat[0], m_lat[1], tl["norm"])
    h_ctx = _normmod(x_ctx, norm1_g[0].reshape(1, d), m_ctx[0], m_ctx[1], tc["norm"])
    for l in range(depth):
        ctx_out = l < depth - 1
        m_lat, m_ctx = layer_mods(l)
        g2 = norm2_g[l].reshape(1, d)
        wl = w_in[l]
        wa = wl[:, :D_A].astype(BF16)
        wag = wl[:, D_A:2 * D_A].astype(BF16)
        wb = wl[:, off_b:off_c].astype(BF16)
        wc = wl[:, off_c:off_db].astype(BF16)
        wdb_cols = wl[:, off_db:off_g].reshape(d, 2, 2, H_C)
        wdb = jnp.zeros((d, DK_C), F32).at[:, :4 * H_C].set(
            jnp.transpose(wdb_cols, (0, 3, 1, 2)).reshape(d, 4 * H_C)).astype(BF16)
        wgates = jnp.transpose(wl[:, off_g:].reshape(d, 3, d), (1, 0, 2)).astype(BF16)
        gbp = jnp.zeros((8, H_C, 4), F32)
        gbp = gbp.at[0, :, :2].set(a_log[l].T).at[1, :, :2].set(dt_bias[l].T)
        gbp = gbp.at[2, :, :2].set(1.0).at[3, :, 2:4].set(1.0)
        gbp = jnp.zeros((8, DK_C), F32).at[:, :4 * H_C].set(gbp.reshape(8, 4 * H_C))
        gq = (jnp.tile(qn_g[l], H_B) * DH_B ** -0.5).reshape(1, D_B)
        gk = jnp.tile(kn_g[l], H_B).reshape(1, D_B)
        bias = _na_bias_table(rpb[l])
        wbr = w_branch[l].astype(BF16)
        wo = w_out[l].astype(BF16)
        wup = ffn_up[l].astype(BF16)
        wdn = ffn_down[l].astype(BF16)
        cwf = ffn_conv_w[l]
        cbf = ffn_conv_b[l].reshape(1, 2 * D_FF)
        gain = onorm_g[l].reshape(1, DV_C)

        if ctx_out:
            nm_lat, nm_ctx = layer_mods(l + 1)
            g1n = norm1_g[l + 1].reshape(1, d)
            next_lat, next_ctx = (g1n, nm_lat[0], nm_lat[1]), (g1n, nm_ctx[0], nm_ctx[1])
        else:
            next_lat = next_ctx = None

        qb_c, kb_c, vb_c = _proj_b(h_ctx, wb, gq, gk, gm64, tc["proj"])
        qc_c, kc_c, vc_c, og_c, gb_c = _proj_c(h_ctx, wc, wdb, conv_c_w[l], ones_c, zeros_c, gm128, gbp,
                                               tc["proj"], False)
        yc_c, sf_c, sb_c = _gdn(qc_c, kc_c, vc_c, gb_c, og_c, s_zero, s_zero, masks, cums, gain)

        ya_l = _conv_a(_proj_a(h_lat, wa, wag, tl["proj"]), conv_a_w[l], conv_a_b[l], ln_a_g[l], ln_a_b[l],
                       tl["conv_a"])
        qb_l, kb_l, vb_l = _proj_b(h_lat, wb, gq, gk, gm64, tl["proj"])
        yb_l = _heads_to_lanes(_na_attention(qb_l, kb_l, vb_l, kb_c, vb_c, bias, NA_ROWS_PER_STEP))
        qc_l, kc_l, vc_l, og_l, gb_l = _proj_c(h_lat, wc, wdb, conv_c_w[l], rope_c, rope_s, gm128, gbp,
                                               tl["proj"], True)
        yc_l, _, _ = _gdn(qc_l, kc_l, vc_l, gb_l, og_l, sf_c, sb_c, masks, cums, gain)
        z_l = _merge(h_lat, ya_l, yb_l, yc_l, wgates, wbr, tl["merge"], MERGE_TN)
        x_lat, h2_l = _resid_mm(x_lat, z_l, m_lat[2], wo, (g2, m_lat[3], m_lat[4]), tl["resid"])
        act_l = _ffn_up(h2_l, wup, cwf, cbf, tl["ffn"], FFN_TF)
        x_lat, h_lat = _resid_mm(x_lat, act_l, m_lat[5], wdn, next_lat, tl["resid"])

        if ctx_out:
            ya_c = _conv_a(_proj_a(h_ctx, wa, wag, tc["proj"]), conv_a_w[l], conv_a_b[l], ln_a_g[l], ln_a_b[l],
                           tc["conv_a"])
            yb_c = _heads_to_lanes(_ctx_attention(qb_c, kb_c, vb_c))
            z_c = _merge(h_ctx, ya_c, yb_c, yc_c, wgates, wbr, tc["merge"], MERGE_TN)
            x_ctx, h2_c = _resid_mm(x_ctx, z_c, m_ctx[2], wo, (g2, m_ctx[3], m_ctx[4]), tc["resid"])
            act_c = _ffn_up(h2_c, wup, cwf, cbf, tc["ffn"], FFN_TF)
            x_ctx, h_ctx = _resid_mm(x_ctx, act_c, m_ctx[5], wdn, next_ctx, tc["resid"])
    return x_lat
```

```python
import functools
import math

import numpy as np
import jax
import jax.numpy as jnp
from jax import lax
from jax.experimental import pallas as pl
from jax.experimental.pallas import tpu as pltpu

F32 = jnp.float32
BF16 = jnp.bfloat16

D_MODEL = 1024
GRID_W = 64
EPS = 1e-6
NEG_INF = -1e30
D_A = 512
CONV_A = 31
H_B = 8
DH_B = 64
D_B = H_B * DH_B
NA_ROWS = 8
NA_COLS = 16
H_C = 4
DK_C = 128
DV_C = 128
D_CQK = H_C * DK_C
D_CV = H_C * DV_C
SHORT_CONV = 4
CHUNK = 64
ROPE_BASE = 10000.0
D_FF = 2816
FFN_CONV = 3

VMEM_LIMIT_BYTES = 48 * 1024 * 1024
HALO = 16
SUBLANES = 8
GDN_BLOCK = 256
GDN_SUB = 128
NA_ROWS_PER_STEP = 8
MERGE_TN = 512
FFN_TF = 256


def _tiles(n_tok):
    cap = lambda t: min(t, n_tok)
    return {"norm": cap(1024), "proj": cap(512), "conv_a": cap(256), "merge": cap(512), "resid": cap(512),
            "ffn": cap(1024)}


def _cparams(*sem):
    return pltpu.CompilerParams(dimension_semantics=sem, vmem_limit_bytes=VMEM_LIMIT_BYTES)


def _dot(a, b):
    return jnp.dot(a, b, preferred_element_type=F32)


def _dot_nt(a, b):
    return lax.dot_general(a, b, (((1,), (1,)), ((), ())), preferred_element_type=F32)


def _split_bf16(x, n):
    parts = []
    r = x
    for idx in range(n):
        p = r.astype(BF16)
        parts.append(p)
        if idx + 1 < n:
            r = r - p.astype(F32)
    return parts


def _dot_exact_rhs(a, b_bf16, n):
    out = None
    for p in _split_bf16(a, n):
        t = _dot(p, b_bf16)
        out = t if out is None else out + t
    return out


def _dot_exact_lhs(a_bf16, b, n):
    out = None
    for p in _split_bf16(b, n):
        t = _dot(a_bf16, p)
        out = t if out is None else out + t
    return out


def _mm(a, b, passes=1):
    if passes == 1:
        return _dot(a.astype(BF16), b.astype(BF16))
    a_hi, a_lo = _split_bf16(a, 2)
    b_hi, b_lo = _split_bf16(b, 2)
    return _dot(a_hi, b_hi) + (_dot(a_lo, b_hi) + _dot(a_hi, b_lo))


def _sigmoid(x):
    return 1.0 / (1.0 + jnp.exp(-x))


def _silu(x):
    return x * _sigmoid(x)


def _softplus(x):
    return jnp.maximum(x, 0.0) + jnp.log(1.0 + jnp.exp(-jnp.abs(x)))


def _norm_mod(x, g, shift, scale):
    ms = jnp.mean(x * x, axis=-1, keepdims=True)
    y = x * lax.rsqrt(ms + EPS) * g
    return y * (1.0 + scale) + shift


def _layer_spec(shape, l, **kw):
    return pl.BlockSpec((None,) + tuple(shape), lambda *_: (l,) + (0,) * len(shape), **kw)


def _wcols_spec(rows, width, l, col_block):
    return pl.BlockSpec((None, rows, width), lambda *_: (l, 0, col_block))


def _mod_spec(d, l, j, row):
    if row is None:
        return pl.BlockSpec((None, None, None, 1, d), lambda bb, *_: (l, bb, j, 0, 0))
    return pl.BlockSpec((None, None, None, 1, d), lambda *_: (l, row, j, 0, 0))


def _halo_specs(tm, n_tok, width):
    per = tm // HALO
    nblk = n_tok // HALO
    prev = pl.BlockSpec((None, HALO, width), lambda b, i, *_: (b, jnp.maximum(i * per - 1, 0), 0))
    nxt = pl.BlockSpec((None, HALO, width), lambda b, i, *_: (b, jnp.minimum((i + 1) * per, nblk - 1), 0))
    return prev, nxt


def _ada_kernel(c_ref, w_ref, b_ref, o_ref):
    s = _silu(c_ref[...])
    o_ref[...] = _mm(s, w_ref[...], 3) + b_ref[...]


def _ada(cc, ada_w, ada_b):
    depth = ada_w.shape[0]
    d = D_MODEL
    return pl.pallas_call(
        _ada_kernel,
        name="ada_mod",
        grid=(depth, 6),
        in_specs=[
            pl.BlockSpec((8, d), lambda l, j: (0, 0)),
            pl.BlockSpec((None, d, d), lambda l, j: (l, 0, j)),
            pl.BlockSpec((None, 1, d), lambda l, j: (l, 0, j)),
        ],
        out_specs=pl.BlockSpec((None, 8, d), lambda l, j: (l, 0, j)),
        out_shape=jax.ShapeDtypeStruct((depth, 8, 6 * d), F32),
        compiler_params=_cparams("arbitrary", "arbitrary"),
    )(cc, ada_w, ada_b.reshape(depth, 1, 6 * d))


def _normmod_kernel(x_ref, g_ref, sh_ref, sc_ref, o_ref):
    o_ref[...] = _norm_mod(x_ref[...], g_ref[...], sh_ref[...], sc_ref[...]).astype(BF16)


def _normmod(x, g3, mods, l, row, j_shift, tm):
    b, n, d = x.shape
    tile = pl.BlockSpec((None, tm, d), lambda bb, i: (bb, i, 0))
    return pl.pallas_call(
        _normmod_kernel,
        name="norm_mod",
        grid=(b, n // tm),
        in_specs=[tile, _layer_spec((1, d), l), _mod_spec(d, l, j_shift, row), _mod_spec(d, l, j_shift + 1, row)],
        out_specs=tile,
        out_shape=jax.ShapeDtypeStruct((b, n, d), BF16),
        compiler_params=_cparams("parallel", "parallel"),
    )(x, g3, mods, mods)


def _proj_a_kernel(h_ref, wa_ref, wg_ref, o_ref):
    h = h_ref[...]
    a = _dot(h, wa_ref[...])
    gate = _dot(h, wg_ref[...])
    o_ref[...] = a * _sigmoid(gate)


def _proj_a(h, w_bf, l, tm):
    b, n, d = h.shape
    return pl.pallas_call(
        _proj_a_kernel,
        name="proj_a",
        grid=(b, n // tm),
        in_specs=[
            pl.BlockSpec((None, tm, d), lambda bb, i: (bb, i, 0)),
            _wcols_spec(d, D_A, l, 0), _wcols_spec(d, D_A, l, 1),
        ],
        out_specs=pl.BlockSpec((None, tm, D_A), lambda bb, i: (bb, i, 0)),
        out_shape=jax.ShapeDtypeStruct((b, n, D_A), F32),
        compiler_params=_cparams("parallel", "parallel"),
    )(h, w_bf, w_bf)


def _conv_a_kernel(yp_ref, y_ref, yn_ref, cw_ref, cb_ref, lg_ref, lb_ref, o_ref, ext_ref, *, tm):
    i = pl.program_id(1)
    last = pl.num_programs(1) - 1
    ext_ref[0, 0:HALO] = yp_ref[...] * (i > 0).astype(F32)
    ext_ref[0, HALO:HALO + tm] = y_ref[...]
    ext_ref[0, HALO + tm:2 * HALO + tm] = yn_ref[...] * (i < last).astype(F32)
    n_keep = tm + 2 * HALO - SUBLANES
    for r in range(1, SUBLANES):
        ext_ref[r, 0:n_keep] = ext_ref[0, pl.ds(r, n_keep), :]
    pad = CONV_A // 2
    acc = jnp.zeros((tm, D_A), F32) + cb_ref[...]
    for k in range(CONV_A):
        off = HALO - pad + k
        r = off % SUBLANES
        acc = acc + ext_ref[r, pl.ds(off - r, tm), :] * cw_ref[k:k + 1, :]
    mu = jnp.mean(acc, axis=-1, keepdims=True)
    cen = acc - mu
    var = jnp.mean(cen * cen, axis=-1, keepdims=True)
    y = cen * lax.rsqrt(var + EPS) * lg_ref[...] + lb_ref[...]
    o_ref[...] = _silu(y).astype(BF16)


def _conv_a(y, cw3, cb3, lg3, lb3, l, tm):
    b, n, c = y.shape
    prev, nxt = _halo_specs(tm, n, c)
    vec = _layer_spec((1, c), l)
    return pl.pallas_call(
        functools.partial(_conv_a_kernel, tm=tm),
        name="conv_a",
        grid=(b, n // tm),
        in_specs=[prev, pl.BlockSpec((None, tm, c), lambda bb, i: (bb, i, 0)), nxt,
                  _layer_spec((CONV_A, c), l), vec, vec, vec],
        out_specs=pl.BlockSpec((None, tm, c), lambda bb, i: (bb, i, 0)),
        out_shape=jax.ShapeDtypeStruct((b, n, c), BF16),
        scratch_shapes=[pltpu.VMEM((SUBLANES, tm + 2 * HALO, c), F32)],
        compiler_params=_cparams("parallel", "parallel"),
    )(y, y, y, cw3, cb3, lg3, lb3)


def _proj_b_kernel(h_ref, wq_ref, wk_ref, wv_ref, gq_ref, gk_ref, gm_ref, q_ref, k_ref, v_ref):
    h = h_ref[...]
    for w_ref, gain_ref, o_ref in ((wq_ref, gq_ref, q_ref), (wk_ref, gk_ref, k_ref), (wv_ref, None, v_ref)):
        acc = _dot(h, w_ref[...])
        if gain_ref is not None:
            ss = _dot((acc * acc).astype(BF16), gm_ref[...])
            acc = acc * lax.rsqrt(ss * (1.0 / DH_B) + EPS) * gain_ref[...]
        for hh in range(H_B):
            o_ref[hh] = acc[:, hh * DH_B:(hh + 1) * DH_B].astype(BF16)


def _proj_b(h, w_bf, gq3, gk3, gmat, l, tm):
    b, n, d = h.shape
    vec = _layer_spec((1, D_B), l)
    first = 2 * D_A // D_B
    head_out = pl.BlockSpec((None, H_B, tm, DH_B), lambda bb, i: (bb, 0, i, 0))
    shp = jax.ShapeDtypeStruct((b, H_B, n, DH_B), BF16)
    return pl.pallas_call(
        _proj_b_kernel,
        name="proj_b",
        grid=(b, n // tm),
        in_specs=[
            pl.BlockSpec((None, tm, d), lambda bb, i: (bb, i, 0)),
            _wcols_spec(d, D_B, l, first), _wcols_spec(d, D_B, l, first + 1), _wcols_spec(d, D_B, l, first + 2),
            vec, vec,
            pl.BlockSpec((D_B, D_B), lambda bb, i: (0, 0)),
        ],
        out_specs=[head_out, head_out, head_out],
        out_shape=[shp, shp, shp],
        compiler_params=_cparams("parallel", "parallel"),
    )(h, w_bf, w_bf, w_bf, gq3, gk3, gmat)


def _na_kernel(q_ref, k_ref, v_ref, kc_ref, vc_ref, bias_ref, o_ref, *, rb, rows):
    i = pl.program_id(2)
    n_loc = NA_ROWS * GRID_W
    rws = [i * rb + rr for rr in range(rb)]
    starts = [jnp.clip(r - NA_ROWS // 2, 0, rows - NA_ROWS) for r in rws]
    variants = [st - r + NA_ROWS - 1 for st, r in zip(starts, rws)]
    tok0 = [pl.multiple_of(st * GRID_W, GRID_W) for st in starts]
    qs = [q_ref[rr * GRID_W:(rr + 1) * GRID_W, :] for rr in range(rb)]
    kc = kc_ref[...]
    vc = vc_ref[...]
    s = [_dot_nt(q, k_ref[pl.ds(t0, n_loc), :]) for q, t0 in zip(qs, tok0)]
    sc = [_dot_nt(q, kc) for q in qs]
    s = [a + bias_ref[vr] for a, vr in zip(s, variants)]
    m = [jnp.maximum(jnp.max(a, axis=-1, keepdims=True), jnp.max(b, axis=-1, keepdims=True)) for a, b in zip(s, sc)]
    p = [jnp.exp(a - mm) for a, mm in zip(s, m)]
    pc = [jnp.exp(b - mm) for b, mm in zip(sc, m)]
    l = [jnp.sum(a, axis=-1, keepdims=True) + jnp.sum(b, axis=-1, keepdims=True) for a, b in zip(p, pc)]
    o = [_dot(a.astype(BF16), v_ref[pl.ds(t0, n_loc), :]) + _dot(b.astype(BF16), vc) for a, b, t0 in zip(p, pc, tok0)]
    for rr in range(rb):
        o_ref[rr * GRID_W:(rr + 1) * GRID_W, :] = (o[rr] / l[rr]).astype(BF16)


def _na_attention(q, k, v, kc, vc, bias, l, rb):
    b, h, t, dh = q.shape
    n_ctx = kc.shape[2]
    rows = t // GRID_W
    full = pl.BlockSpec((None, None, t, dh), lambda bb, hh, i: (bb, hh, 0, 0))
    cfull = pl.BlockSpec((None, None, n_ctx, dh), lambda bb, hh, i: (bb, hh, 0, 0))
    tile = pl.BlockSpec((None, None, rb * GRID_W, dh), lambda bb, hh, i: (bb, hh, i, 0))
    return pl.pallas_call(
        functools.partial(_na_kernel, rb=rb, rows=rows),
        name="na_attn",
        grid=(b, h, rows // rb),
        in_specs=[tile, full, full, cfull, cfull,
                  pl.BlockSpec((None, None, NA_ROWS, GRID_W, NA_ROWS * GRID_W),
                               lambda bb, hh, i: (l, hh, 0, 0, 0))],
        out_specs=tile,
        out_shape=jax.ShapeDtypeStruct((b, h, t, dh), BF16),
        compiler_params=_cparams("parallel", "parallel", "arbitrary"),
    )(q, k, v, kc, vc, bias)


def _ctx_attn_kernel(q_ref, k_ref, v_ref, o_ref):
    s = _dot_nt(q_ref[...], k_ref[...])
    m = jnp.max(s, axis=-1, keepdims=True)
    p = jnp.exp(s - m)
    l = jnp.sum(p, axis=-1, keepdims=True)
    o_ref[...] = (_dot(p.astype(BF16), v_ref[...]) / l).astype(BF16)


def _ctx_attention(q, k, v):
    b, h, n, dh = q.shape
    full = pl.BlockSpec((None, None, n, dh), lambda bb, hh: (bb, hh, 0, 0))
    return pl.pallas_call(
        _ctx_attn_kernel,
        name="ctx_attn",
        grid=(b, h),
        in_specs=[full, full, full],
        out_specs=full,
        out_shape=jax.ShapeDtypeStruct((b, h, n, dh), BF16),
        compiler_params=_cparams("parallel", "parallel"),
    )(q, k, v)


def _na_bias_table(rpb):
    depth = rpb.shape[0]
    cidx = np.arange(GRID_W)
    dc = np.clip(cidx[None, :] - cidx[:, None] + NA_COLS - 1, 0, 2 * NA_COLS - 2)
    onehot = (dc[None] == np.arange(2 * NA_COLS - 1)[:, None, None]).astype(np.float32)
    cs = np.clip(cidx - NA_COLS // 2, 0, GRID_W - NA_COLS)
    col_ok = (cidx[None, :] >= cs[:, None]) & (cidx[None, :] < cs[:, None] + NA_COLS)
    toep = jnp.einsum("lhrd,dqk->lhrqk", rpb, onehot, precision=lax.Precision.HIGHEST)
    toep = jnp.where(col_ok, toep, NEG_INF)
    tabs = [jnp.transpose(toep[:, :, v:v + NA_ROWS], (0, 1, 3, 2, 4)).reshape(depth, H_B, GRID_W, NA_ROWS * GRID_W)
            for v in range(NA_ROWS)]
    return jnp.stack(tabs, axis=2)


def _proj_c_kernel(hp_ref, h_ref, hn_ref, wq_ref, wk_ref, wv_ref, wo_ref, wdb_ref, cw_ref, rc_ref, rs_ref,
                   gm_ref, gbp_ref, q_ref, k_ref, v_ref, og_ref, gb_ref, p_ref, *, tm, use_rope):
    i = pl.program_id(1)
    last = pl.num_programs(1) - 1
    hp = jnp.where(i > 0, hp_ref[...], jnp.zeros_like(hp_ref))
    hn = jnp.where(i < last, hn_ref[...], jnp.zeros_like(hn_ref))
    hc = h_ref[...]
    pad_l = SHORT_CONV // 2
    n_ext = tm + 2 * HALO
    lane = lax.broadcasted_iota(jnp.int32, (tm, DK_C), 1)
    first_half = (lane & (DK_C // 2 - 1)) < DK_C // 4
    for sec, (w_ref, o_ref) in enumerate(((wq_ref, q_ref), (wk_ref, k_ref), (wv_ref, v_ref))):
        cols = slice(sec * D_CQK, (sec + 1) * D_CQK)
        p_ref[0:HALO] = _dot(hp, w_ref[...])
        p_ref[HALO:HALO + tm] = _dot(hc, w_ref[...])
        p_ref[HALO + tm:n_ext] = _dot(hn, w_ref[...])
        pv = p_ref[...]
        y = None
        for kk in range(SHORT_CONV):
            sh = (pad_l - kk) % n_ext
            pk = pv if sh == 0 else pltpu.roll(pv, sh, axis=0)
            t = pk[HALO:HALO + tm] * cw_ref[kk:kk + 1, cols]
            y = t if y is None else y + t
        y = _silu(y)
        if sec < 2:
            ss = _dot((y * y).astype(BF16), gm_ref[...])
            y = y * lax.rsqrt(ss + EPS)
            if use_rope:
                heads = []
                for hh in range(H_C):
                    yh = y[:, hh * DK_C:(hh + 1) * DK_C]
                    swapped = jnp.where(first_half, pltpu.roll(yh, DK_C - DK_C // 4, axis=1),
                                        pltpu.roll(yh, DK_C // 4, axis=1))
                    heads.append(yh * rc_ref[...] + swapped * rs_ref[...])
                y = jnp.concatenate(heads, axis=1)
            if sec == 0:
                y = y * (DK_C ** -0.5)
        o_ref[...] = y
    og_ref[...] = _dot(hc, wo_ref[...])
    db = _dot(hc, wdb_ref[...])
    gval = -jnp.exp(gbp_ref[0:1, :]) * _softplus(db + gbp_ref[1:2, :])
    gb_ref[...] = gbp_ref[2:3, :] * gval + gbp_ref[3:4, :] * _sigmoid(db)


def _proj_c(h, w_bf, cw3, rc, rs, gmat, gbp3, l, tm, use_rope):
    b, n, d = h.shape
    first = (2 * D_A + 3 * D_B) // D_CQK
    db_block = (2 * D_A + 3 * D_B + 3 * D_CQK + D_CV) // DK_C
    prev, nxt = _halo_specs(tm, n, d)
    tile512 = pl.BlockSpec((None, tm, D_CQK), lambda bb, i: (bb, i, 0))
    shp = jax.ShapeDtypeStruct((b, n, D_CQK), F32)
    rope_spec = pl.BlockSpec((tm, DK_C), lambda bb, i: (i, 0))
    return pl.pallas_call(
        functools.partial(_proj_c_kernel, tm=tm, use_rope=use_rope),
        name="proj_c",
        grid=(b, n // tm),
        in_specs=[
            prev, pl.BlockSpec((None, tm, d), lambda bb, i: (bb, i, 0)), nxt,
            _wcols_spec(d, D_CQK, l, first), _wcols_spec(d, D_CQK, l, first + 1),
            _wcols_spec(d, D_CQK, l, first + 2), _wcols_spec(d, D_CQK, l, first + 3),
            _wcols_spec(d, DK_C, l, db_block),
            _layer_spec((SHORT_CONV, 3 * D_CQK), l),
            rope_spec, rope_spec,
            pl.BlockSpec((D_CQK, D_CQK), lambda bb, i: (0, 0)),
            _layer_spec((8, DK_C), l),
        ],
        out_specs=[tile512] * 4 + [pl.BlockSpec((None, tm, DK_C), lambda bb, i: (bb, i, 0))],
        out_shape=[shp] * 4 + [jax.ShapeDtypeStruct((b, n, DK_C), F32)],
        scratch_shapes=[pltpu.VMEM((tm + 2 * HALO, D_CQK), F32)],
        compiler_params=_cparams("parallel", "parallel"),
    )(h, h, h, w_bf, w_bf, w_bf, w_bf, w_bf, cw3, rc, rs, gmat, gbp3)


def _chunk_masks(n):
    r = np.arange(n)[:, None]
    c = np.arange(n)[None, :]
    same = lambda s: (r // s) == (c // s)
    chunk = same(CHUNK)
    ms = [chunk & (r >= c), chunk & (r > c), chunk & (r <= c), chunk & (r < c), same(8),
          same(16) & ~same(8), same(32) & ~same(16), same(64) & ~same(32)]
    return np.stack(ms).astype(np.float32)


def _tri_inv_all(lmats, eye, m_ref):
    m8 = m_ref[4]
    n0 = [-(l * m8) for l in lmats]
    n2 = [_mm(a, a) for a in n0]
    n4 = [_mm(a, a) for a in n2]
    n3 = [_mm(a, b) for a, b in zip(n0, n2)]
    t1 = [eye + a + b + c for a, b, c in zip(n0, n2, n3)]
    t1n4 = [_mm(a, b) for a, b in zip(t1, n4)]
    t = [a + b for a, b in zip(t1, t1n4)]
    for lvl in (5, 6, 7):
        off = m_ref[lvl]
        lt = [_mm(l * off, a) for l, a in zip(lmats, t)]
        tlt = [_mm(a, b) for a, b in zip(t, lt)]
        t = [a - b for a, b in zip(t, tlt)]
    return t


def _gdn_step(io, m_ref, cum_ref):
    nsub = GDN_BLOCK // GDN_SUB
    csub = GDN_SUB // CHUNK
    nch = nsub * csub
    r16, cols = [], []
    for d in range(2):
        g_t = io[d][3][...].T[0:4 * H_C]
        csum = _dot_exact_rhs(g_t, cum_ref[d], 3)
        rid = lax.broadcasted_iota(jnp.int32, g_t.shape, 0)
        r = jnp.where(rid // H_C == d, csum, g_t)
        r16.append(r)
        cols.append(jnp.concatenate([r, jnp.zeros((DK_C - 4 * H_C, GDN_BLOCK), F32)], axis=0).T)
    probs = [(d, h, sb) for d in range(2) for h in range(H_C) for sb in range(nsub)]
    incl = [m_ref[0], m_ref[2]]
    strict = [m_ref[1], m_ref[3]]
    eye = incl[0] - strict[0]

    def tile(ref, h, sb):
        return ref[sb * GDN_SUB:(sb + 1) * GDN_SUB, h * DK_C:(h + 1) * DK_C]

    q = [tile(io[d][0], h, sb) for d, h, sb in probs]
    k = [tile(io[d][1], h, sb) for d, h, sb in probs]
    v = [tile(io[d][2], h, sb) for d, h, sb in probs]
    gcol, bcol, decay = [], [], []
    for d, h, sb in probs:
        rows = slice(sb * GDN_SUB, (sb + 1) * GDN_SUB)
        c_g = d * H_C + h
        c_b = 2 * H_C + c_g
        gc = jnp.broadcast_to(cols[d][rows, c_g:c_g + 1], (GDN_SUB, DK_C))
        gcol.append(gc)
        bcol.append(jnp.broadcast_to(cols[d][rows, c_b:c_b + 1], (GDN_SUB, DK_C)))
        decay.append(jnp.exp(jnp.where(incl[d] > 0.5, gc - r16[d][c_g:c_g + 1, rows], NEG_INF)))
    k16 = [a.astype(BF16) for a in k]
    kk = [_dot_nt(a, a) for a in k16]
    qk = [_dot_nt(a.astype(BF16), b) for a, b in zip(q, k16)]
    lmats = [a * b * (c * strict[p[0]]) for a, b, c, p in zip(kk, bcol, decay, probs)]
    amat = [(a * c).astype(BF16) for a, c in zip(qk, decay)]
    tinv = _tri_inv_all(lmats, eye, m_ref)
    eg = [jnp.exp(a) for a in gcol]
    rhs = [jnp.concatenate([vv * b, kx * (b * e)], axis=1).astype(BF16) for vv, kx, b, e in zip(v, k, bcol, eg)]
    sol = [_dot(t.astype(BF16), r) for t, r in zip(tinv, rhs)]
    qd = [a * e for a, e in zip(q, eg)]
    gtot, kd_t = [], []
    for (d, h, sb), kx, gc in zip(probs, k, gcol):
        parts, gts = [], []
        for c in range(csub):
            end = c * CHUNK + (CHUNK - 1 if d == 0 else 0)
            gt = gc[end:end + 1, :]
            gts.append(gt)
            sl = slice(c * CHUNK, (c + 1) * CHUNK)
            parts.append(kx[sl] * jnp.exp(gt - gc[sl]))
        gtot.append(gts)
        kd_t.append(jnp.concatenate(parts, axis=0).T.astype(BF16))
    pidx = {p: n for n, p in enumerate(probs)}
    chains = [(d, h) for d in range(2) for h in range(H_C)]
    state = [io[d][4][h] for d, h in chains]
    vnew = [[None] * nch for _ in chains]
    ointer = [[None] * nch for _ in chains]
    zeros = jnp.zeros((CHUNK, DV_C), BF16)
    for step in range(nch):
        pos = [(step if d == 0 else nch - 1 - step) for d, _ in chains]
        loc = [(pidx[(d, h, n // csub)], n % csub) for (d, h), n in zip(chains, pos)]
        sls = [slice(c * CHUNK, (c + 1) * CHUNK) for _, c in loc]
        wq = [jnp.concatenate([sol[p][sl, DV_C:], qd[p][sl]], axis=0).astype(BF16) for (p, _), sl in zip(loc, sls)]
        r = [_dot(a, s.astype(BF16)) for a, s in zip(wq, state)]
        vn = [(sol[p][sl, :DV_C] - rr[0:CHUNK]).astype(BF16) for (p, _), sl, rr in zip(loc, sls, r)]
        vpad = [jnp.concatenate([x if m == c else zeros for m in range(csub)], axis=0) for x, (_, c) in zip(vn, loc)]
        upd = [_dot(kd_t[p], x) for (p, _), x in zip(loc, vpad)]
        state = [s * jnp.exp(gtot[p][c]) + u for s, (p, c), u in zip(state, loc, upd)]
        for ci, n in enumerate(pos):
            vnew[ci][n] = vn[ci]
            ointer[ci][n] = r[ci][CHUNK:2 * CHUNK]
    for (d, h), s in zip(chains, state):
        io[d][4][h] = s
    out = [[None] * H_C for _ in range(2)]
    for ci, (d, h) in enumerate(chains):
        parts = []
        for sb in range(nsub):
            p = pidx[(d, h, sb)]
            vn_sb = jnp.concatenate(vnew[ci][sb * csub:(sb + 1) * csub], axis=0)
            parts.append(jnp.concatenate(ointer[ci][sb * csub:(sb + 1) * csub], axis=0) + _dot(amat[p], vn_sb))
        out[d][h] = jnp.concatenate(parts, axis=0)
    return out


def _gdn_finalize(o_heads, gate_ref, gain):
    ys = []
    for h, o in enumerate(o_heads):
        ms = jnp.mean(o * o, axis=-1, keepdims=True)
        ys.append(o * lax.rsqrt(ms + EPS) * gain * _silu(gate_ref[:, h * DV_C:(h + 1) * DV_C]))
    return jnp.concatenate(ys, axis=1).astype(BF16)


def _gdn_kernel(qf_ref, kf_ref, vf_ref, gf_ref, ogf_ref, qb_ref, kb_ref, vb_ref, gb_ref, ogb_ref,
                s0f_ref, s0b_ref, m_ref, cum_ref, gain_ref,
                y_ref, sfo_ref, sbo_ref, oacc_ref, sf_ref, sb_ref, *, nb):
    i = pl.program_id(1)

    @pl.when(i == 0)
    def _():
        sf_ref[...] = s0f_ref[...]
        sb_ref[...] = s0b_ref[...]

    of, ob = _gdn_step(((qf_ref, kf_ref, vf_ref, gf_ref, sf_ref), (qb_ref, kb_ref, vb_ref, gb_ref, sb_ref)),
                       m_ref, cum_ref)
    rows_f = pl.ds(pl.multiple_of(i * GDN_BLOCK, GDN_BLOCK), GDN_BLOCK)
    rows_b = pl.ds(pl.multiple_of((nb - 1 - i) * GDN_BLOCK, GDN_BLOCK), GDN_BLOCK)
    gain = gain_ref[...]

    def split(x):
        return [x[:, h * DV_C:(h + 1) * DV_C] for h in range(H_C)]

    @pl.when(2 * i < nb - 1)
    def _():
        oacc_ref[rows_f, :] = jnp.concatenate(of, axis=1)
        oacc_ref[rows_b, :] = jnp.concatenate(ob, axis=1)

    if nb % 2 == 1:
        @pl.when(2 * i == nb - 1)
        def _():
            y_ref[rows_f, :] = _gdn_finalize([a + b for a, b in zip(of, ob)], ogf_ref, gain)

    @pl.when(2 * i > nb - 1)
    def _():
        y_ref[rows_f, :] = _gdn_finalize([a + b for a, b in zip(split(oacc_ref[rows_f, :]), of)], ogf_ref, gain)
        y_ref[rows_b, :] = _gdn_finalize([a + b for a, b in zip(split(oacc_ref[rows_b, :]), ob)], ogb_ref, gain)

    @pl.when(i == nb - 1)
    def _():
        sfo_ref[...] = sf_ref[...]
        sbo_ref[...] = sb_ref[...]


def _gdn(q, k, v, gb, og, s0f, s0b, masks, cums, gain3, l):
    b, n, _ = q.shape
    nb = n // GDN_BLOCK
    fwd = pl.BlockSpec((None, GDN_BLOCK, D_CQK), lambda bb, i: (bb, i, 0))
    bwd = pl.BlockSpec((None, GDN_BLOCK, D_CQK), lambda bb, i: (bb, nb - 1 - i, 0))
    gfwd = pl.BlockSpec((None, GDN_BLOCK, DK_C), lambda bb, i: (bb, i, 0))
    gbwd = pl.BlockSpec((None, GDN_BLOCK, DK_C), lambda bb, i: (bb, nb - 1 - i, 0))
    st = pl.BlockSpec((None, H_C, DK_C, DV_C), lambda bb, i: (bb, 0, 0, 0))
    st_shape = jax.ShapeDtypeStruct((b, H_C, DK_C, DV_C), F32)
    return pl.pallas_call(
        functools.partial(_gdn_kernel, nb=nb),
        name="gdn_scan",
        grid=(b, nb),
        in_specs=[fwd, fwd, fwd, gfwd, fwd, bwd, bwd, bwd, gbwd, bwd,
                  st, st,
                  pl.BlockSpec((8, GDN_SUB, GDN_SUB), lambda bb, i: (0, 0, 0)),
                  pl.BlockSpec((2, GDN_BLOCK, GDN_BLOCK), lambda bb, i: (0, 0, 0)),
                  _layer_spec((1, DV_C), l)],
        out_specs=[pl.BlockSpec((None, n, D_CV), lambda bb, i: (bb, 0, 0)), st, st],
        out_shape=[jax.ShapeDtypeStruct((b, n, D_CV), BF16), st_shape, st_shape],
        scratch_shapes=[pltpu.VMEM((n, D_CV), F32), pltpu.VMEM((H_C, DK_C, DV_C), F32),
                        pltpu.VMEM((H_C, DK_C, DV_C), F32)],
        compiler_params=_cparams("parallel", "arbitrary"),
    )(q, k, v, gb, og, q, k, v, gb, og, s0f, s0b, masks, cums, gain3)


def _merge_kernel(h_ref, ya_ref, yb_ref, yc_ref, wga_ref, wgb_ref, wgc_ref, wbr_ref, z_ref):
    h = h_ref[...]
    gates = [_dot(h, w_ref[...]) for w_ref in (wga_ref, wgb_ref, wgc_ref)]
    vals = [_dot(y_ref[...], wbr_ref[br]) for br, y_ref in enumerate((ya_ref, yb_ref, yc_ref))]
    z = None
    for gt, vl in zip(gates, vals):
        t = _sigmoid(gt) * vl
        z = t if z is None else z + t
    z_ref[...] = z.astype(BF16)


def _merge(h, ya, yb, yc, wgates, wbr4, l, tm, tn):
    b, n, d = h.shape
    gate_spec = lambda br: pl.BlockSpec((None, d, tn), lambda bb, i, j: (l, 0, br * (d // tn) + j))
    yt = pl.BlockSpec((None, tm, D_A), lambda bb, i, j: (bb, i, 0))
    return pl.pallas_call(
        _merge_kernel,
        name="merge_gate",
        grid=(b, n // tm, d // tn),
        in_specs=[
            pl.BlockSpec((None, tm, d), lambda bb, i, j: (bb, i, 0)), yt, yt, yt,
            gate_spec(0), gate_spec(1), gate_spec(2),
            pl.BlockSpec((None, 3, D_A, tn), lambda bb, i, j: (l, 0, 0, j)),
        ],
        out_specs=pl.BlockSpec((None, tm, tn), lambda bb, i, j: (bb, i, j)),
        out_shape=jax.ShapeDtypeStruct((b, n, d), BF16),
        compiler_params=_cparams("parallel", "parallel", "parallel"),
    )(h, ya, yb, yc, wgates, wgates, wgates, wbr4)


def _resid_kernel(x_ref, a_ref, gt_ref, w_ref, *rest, with_norm):
    xn = x_ref[...] + gt_ref[...] * _dot(a_ref[...], w_ref[...])
    if with_norm:
        g_ref, sh_ref, sc_ref, o_ref, h_ref = rest
        h_ref[...] = _norm_mod(xn, g_ref[...], sh_ref[...], sc_ref[...]).astype(BF16)
    else:
        (o_ref,) = rest
    o_ref[...] = xn


def _resid_mm(x, a, mods, l, row, j_gate, w3, norm, tm):
    b, n, d = x.shape
    kdim = a.shape[-1]
    xt = pl.BlockSpec((None, tm, d), lambda bb, i: (bb, i, 0))
    in_specs = [xt, pl.BlockSpec((None, tm, kdim), lambda bb, i: (bb, i, 0)), _mod_spec(d, l, j_gate, row),
                _layer_spec((kdim, d), l)]
    args = [x, a, mods, w3]
    out_specs, out_shape = [xt], [jax.ShapeDtypeStruct((b, n, d), F32)]
    if norm is not None:
        g3, ln, j_shift = norm
        in_specs += [_layer_spec((1, d), ln), _mod_spec(d, ln, j_shift, row), _mod_spec(d, ln, j_shift + 1, row)]
        args += [g3, mods, mods]
        out_specs.append(xt)
        out_shape.append(jax.ShapeDtypeStruct((b, n, d), BF16))
    res = pl.pallas_call(
        functools.partial(_resid_kernel, with_norm=norm is not None),
        name="resid_mm",
        grid=(b, n // tm),
        in_specs=in_specs,
        out_specs=out_specs,
        out_shape=out_shape,
        compiler_params=_cparams("parallel", "parallel"),
    )(*args)
    return (res[0], res[1]) if norm is not None else (res[0], None)


def _ffn_up_kernel(hp_ref, h_ref, hn_ref, wup_ref, cw_ref, cb_ref, o_ref, ug_ref, uv_ref, *, tm, tf):
    i = pl.program_id(1)
    last = pl.num_programs(1) - 1
    hp = jnp.where(i > 0, hp_ref[...], jnp.zeros_like(hp_ref))
    hn = jnp.where(i < last, hn_ref[...], jnp.zeros_like(hn_ref))
    hc = h_ref[...]
    pad = FFN_CONV // 2
    n_ext = tm + 2 * HALO
    nf = D_FF // tf

    def project(j):
        slot = j % 2
        for u_ref, c0 in ((ug_ref, j * tf), (uv_ref, D_FF + j * tf)):
            w = wup_ref[:, c0:c0 + tf]
            u_ref[slot, 0:HALO] = _dot(hp, w)
            u_ref[slot, HALO:HALO + tm] = _dot(hc, w)
            u_ref[slot, HALO + tm:n_ext] = _dot(hn, w)

    def conv(j):
        slot = j % 2
        ug = ug_ref[slot]
        uv = uv_ref[slot]
        gc = slice(j * tf, (j + 1) * tf)
        vc = slice(D_FF + j * tf, D_FF + (j + 1) * tf)
        cg = cb_ref[:, gc]
        cv = cb_ref[:, vc]
        for kk in range(FFN_CONV):
            sh = (pad - kk) % n_ext
            ugk = ug if sh == 0 else pltpu.roll(ug, sh, axis=0)
            uvk = uv if sh == 0 else pltpu.roll(uv, sh, axis=0)
            cg = cg + ugk[HALO:HALO + tm] * cw_ref[kk:kk + 1, gc]
            cv = cv + uvk[HALO:HALO + tm] * cw_ref[kk:kk + 1, vc]
        o_ref[:, gc] = (_silu(cg) * cv).astype(BF16)

    project(0)
    for j in range(1, nf):
        project(j)
        conv(j - 1)
    conv(nf - 1)


def _ffn_up(h, wup3, cw3, cb3, l, tm, tf):
    b, n, d = h.shape
    prev, nxt = _halo_specs(tm, n, d)
    const = lambda shape: _layer_spec(shape, l, pipeline_mode=pl.Buffered(1))
    return pl.pallas_call(
        functools.partial(_ffn_up_kernel, tm=tm, tf=tf),
        name="ffn_up",
        grid=(b, n // tm),
        in_specs=[
            prev, pl.BlockSpec((None, tm, d), lambda bb, i: (bb, i, 0)), nxt,
            const((d, 2 * D_FF)), const((FFN_CONV, 2 * D_FF)), const((1, 2 * D_FF)),
        ],
        out_specs=pl.BlockSpec((None, tm, D_FF), lambda bb, i: (bb, i, 0)),
        out_shape=jax.ShapeDtypeStruct((b, n, D_FF), BF16),
        scratch_shapes=[pltpu.VMEM((2, tm + 2 * HALO, tf), F32), pltpu.VMEM((2, tm + 2 * HALO, tf), F32)],
        compiler_params=_cparams("parallel", "parallel"),
    )(h, h, h, wup3, cw3, cb3)


def _rope_tables(n_tok):
    t = jnp.arange(n_tok)
    row = (t // GRID_W).astype(F32)
    col = (t % GRID_W).astype(F32)
    n_freq = DK_C // 4
    inv = jnp.power(ROPE_BASE, -jnp.arange(n_freq, dtype=F32) / n_freq)
    ar = row[:, None] * inv
    ac = col[:, None] * inv
    cos = jnp.concatenate([jnp.cos(ar), jnp.cos(ar), jnp.cos(ac), jnp.cos(ac)], axis=-1)
    sin = jnp.concatenate([-jnp.sin(ar), jnp.sin(ar), -jnp.sin(ac), jnp.sin(ac)], axis=-1)
    return cos, sin


def _block_ones(n, blk):
    idx = np.arange(n) // blk
    return jnp.asarray((idx[:, None] == idx[None, :]).astype(np.float32), dtype=BF16)


def _heads_to_lanes(y):
    b, h, n, dh = y.shape
    return jnp.transpose(y, (0, 2, 1, 3)).reshape(b, n, h * dh)


def kernel(x, c, ctx, c_ctx, ada_w, ada_b, norm1_g, norm2_g, w_in, conv_a_w, conv_a_b, ln_a_g, ln_a_b, qn_g, kn_g,
           rpb, conv_c_w, a_log, dt_bias, onorm_g, w_branch, w_out, ffn_up, ffn_conv_w, ffn_conv_b, ffn_down):
    batch, n_lat, d = x.shape
    n_ctx = ctx.shape[1]
    depth = ada_w.shape[0]

    cc = jnp.zeros((8, d), F32).at[:batch].set(c).at[batch].set(c_ctx)
    mods = _ada(cc, ada_w, ada_b).reshape(depth, 8, 6, 1, d)
    ctx_row = batch

    rope_c, rope_s = _rope_tables(n_lat)
    ones_c = jnp.ones((n_ctx, DK_C), F32)
    zeros_c = jnp.zeros((n_ctx, DK_C), F32)
    gm64 = _block_ones(D_B, DH_B)
    gm128 = _block_ones(D_CQK, DK_C)
    masks = jnp.asarray(_chunk_masks(GDN_SUB))
    blk_masks = _chunk_masks(GDN_BLOCK)
    cums = jnp.asarray(np.stack([blk_masks[0].T, blk_masks[2].T]), dtype=BF16)
    s_zero = jnp.zeros((batch, H_C, DK_C, DV_C), F32)

    off_g = 2 * D_A + 3 * D_B + 3 * D_CQK + D_CV + 4 * H_C
    w_bf = w_in.astype(BF16)
    wgates = w_bf[:, :, off_g:]
    wbr4 = w_branch.astype(BF16)
    wo3 = w_out.astype(BF16)
    wup3 = ffn_up.astype(BF16)
    wdn3 = ffn_down.astype(BF16)
    g1_3 = norm1_g.reshape(depth, 1, d)
    g2_3 = norm2_g.reshape(depth, 1, d)
    gq3 = (jnp.tile(qn_g, (1, H_B)) * DH_B ** -0.5).reshape(depth, 1, D_B)
    gk3 = jnp.tile(kn_g, (1, H_B)).reshape(depth, 1, D_B)
    gbp3 = jnp.zeros((depth, 8, DK_C), F32)
    gbp3 = gbp3.at[:, 0, :2 * H_C].set(a_log.reshape(depth, 2 * H_C)).at[:, 1, :2 * H_C].set(
        dt_bias.reshape(depth, 2 * H_C)).at[:, 2, :2 * H_C].set(1.0).at[:, 3, 2 * H_C:4 * H_C].set(1.0)
    bias5 = _na_bias_table(rpb)
    cb_a3 = conv_a_b.reshape(depth, 1, D_A)
    lg_a3 = ln_a_g.reshape(depth, 1, D_A)
    lb_a3 = ln_a_b.reshape(depth, 1, D_A)
    cbf3 = ffn_conv_b.reshape(depth, 1, 2 * D_FF)
    gain3 = onorm_g.reshape(depth, 1, DV_C)

    tl = _tiles(n_lat)
    tc = _tiles(n_ctx)
    x_lat, x_ctx = x, ctx
    h_lat = _normmod(x_lat, g1_3, mods, 0, None, 0, tl["norm"])
    h_ctx = _normmod(x_ctx, g1_3, mods, 0, ctx_row, 0, tc["norm"])
    for l in range(depth):
        ctx_out = l < depth - 1
        next_norm = (g1_3, l + 1, 0) if ctx_out else None

        qb_c, kb_c, vb_c = _proj_b(h_ctx, w_bf, gq3, gk3, gm64, l, tc["proj"])
        qc_c, kc_c, vc_c, og_c, gb_c = _proj_c(h_ctx, w_bf, conv_c_w, ones_c, zeros_c, gm128, gbp3, l,
                                               tc["proj"], False)
        yc_c, sf_c, sb_c = _gdn(qc_c, kc_c, vc_c, gb_c, og_c, s_zero, s_zero, masks, cums, gain3, l)

        ya_l = _conv_a(_proj_a(h_lat, w_bf, l, tl["proj"]), conv_a_w, cb_a3, lg_a3, lb_a3, l, tl["conv_a"])
        qb_l, kb_l, vb_l = _proj_b(h_lat, w_bf, gq3, gk3, gm64, l, tl["proj"])
        yb_l = _heads_to_lanes(_na_attention(qb_l, kb_l, vb_l, kb_c, vb_c, bias5, l, NA_ROWS_PER_STEP))
        qc_l, kc_l, vc_l, og_l, gb_l = _proj_c(h_lat, w_bf, conv_c_w, rope_c, rope_s, gm128, gbp3, l,
                                               tl["proj"], True)
        yc_l, _, _ = _gdn(qc_l, kc_l, vc_l, gb_l, og_l, sf_c, sb_c, masks, cums, gain3, l)
        z_l = _merge(h_lat, ya_l, yb_l, yc_l, wgates, wbr4, l, tl["merge"], MERGE_TN)
        x_lat, h2_l = _resid_mm(x_lat, z_l, mods, l, None, 2, wo3, (g2_3, l, 3), tl["resid"])
        act_l = _ffn_up(h2_l, wup3, ffn_conv_w, cbf3, l, tl["ffn"], FFN_TF)
        x_lat, h_lat = _resid_mm(x_lat, act_l, mods, l, None, 5, wdn3, next_norm, tl["resid"])

        if ctx_out:
            ya_c = _conv_a(_proj_a(h_ctx, w_bf, l, tc["proj"]), conv_a_w, cb_a3, lg_a3, lb_a3, l, tc["conv_a"])
            yb_c = _heads_to_lanes(_ctx_attention(qb_c, kb_c, vb_c))
            z_c = _merge(h_ctx, ya_c, yb_c, yc_c, wgates, wbr4, l, tc["merge"], MERGE_TN)
            x_ctx, h2_c = _resid_mm(x_ctx, z_c, mods, l, ctx_row, 2, wo3, (g2_3, l, 3), tc["resid"])
            act_c = _ffn_up(h2_c, wup3, ffn_conv_w, cbf3, l, tc["ffn"], FFN_TF)
            x_ctx, h_ctx = _resid_mm(x_ctx, act_c, mods, l, ctx_row, 5, wdn3, next_norm, tc["resid"])
    return x_lat
```

```python
import functools
import math

import numpy as np
import jax
import jax.numpy as jnp
from jax import lax
from jax.experimental import pallas as pl
from jax.experimental.pallas import tpu as pltpu

F32 = jnp.float32
BF16 = jnp.bfloat16

D_MODEL = 1024
GRID_W = 64
EPS = 1e-6
NEG_INF = -1e30
LOG2E = math.log2(math.e)
D_A = 512
CONV_A = 31
H_B = 8
DH_B = 64
D_B = H_B * DH_B
NA_ROWS = 8
NA_COLS = 16
H_C = 4
DK_C = 128
DV_C = 128
D_CQK = H_C * DK_C
D_CV = H_C * DV_C
SHORT_CONV = 4
CHUNK = 64
ROPE_BASE = 10000.0
D_FF = 2816
FFN_CONV = 3

VMEM_LIMIT_BYTES = 48 * 1024 * 1024
HALO = 16
SUBLANES = 8
GDN_BLOCK = 256
GDN_SUB = 128
NA_ROWS_PER_STEP = 8
HEADS_PER_STEP = 2
MERGE_TN = 512
FFN_TF = 256


def _tiles(n_tok):
    cap = lambda t: min(t, n_tok)
    return {"norm": cap(1024), "proj": cap(512), "conv_a": cap(256), "merge": cap(512), "resid": cap(512),
            "ffn": cap(1024)}


def _cparams(*sem):
    return pltpu.CompilerParams(dimension_semantics=sem, vmem_limit_bytes=VMEM_LIMIT_BYTES)


def _dot(a, b):
    return jnp.dot(a, b, preferred_element_type=F32)


def _dot_nt(a, b):
    return lax.dot_general(a, b, (((1,), (1,)), ((), ())), preferred_element_type=F32)


def _split_bf16(x, n):
    parts = []
    r = x
    for idx in range(n):
        p = r.astype(BF16)
        parts.append(p)
        if idx + 1 < n:
            r = r - p.astype(F32)
    return parts


def _dot_exact_rhs(a, b_bf16, n):
    out = None
    for p in _split_bf16(a, n):
        t = _dot(p, b_bf16)
        out = t if out is None else out + t
    return out


def _dot_exact_lhs(a_bf16, b, n):
    out = None
    for p in _split_bf16(b, n):
        t = _dot(a_bf16, p)
        out = t if out is None else out + t
    return out


def _mm(a, b, passes=1):
    if passes == 1:
        return _dot(a.astype(BF16), b.astype(BF16))
    a_hi, a_lo = _split_bf16(a, 2)
    b_hi, b_lo = _split_bf16(b, 2)
    return _dot(a_hi, b_hi) + (_dot(a_lo, b_hi) + _dot(a_hi, b_lo))


def _sigmoid(x):
    return 1.0 / (1.0 + jnp.exp(-x))


def _silu(x):
    return x * _sigmoid(x)


def _softplus(x):
    return jnp.maximum(x, 0.0) + jnp.log(1.0 + jnp.exp(-jnp.abs(x)))


def _norm_mod(x, g, shift, scale):
    ms = jnp.mean(x * x, axis=-1, keepdims=True)
    y = x * lax.rsqrt(ms + EPS) * g
    return y * (1.0 + scale) + shift


def _layer_spec(shape, l, **kw):
    return pl.BlockSpec((None,) + tuple(shape), lambda *_: (l,) + (0,) * len(shape), **kw)


def _wcols_spec(rows, width, l, col_block):
    return pl.BlockSpec((None, rows, width), lambda *_: (l, 0, col_block))


def _mod_spec(d, l, j, row):
    if row is None:
        return pl.BlockSpec((None, None, None, 1, d), lambda bb, *_: (l, bb, j, 0, 0))
    return pl.BlockSpec((None, None, None, 1, d), lambda *_: (l, row, j, 0, 0))


def _halo_specs(tm, n_tok, width):
    per = tm // HALO
    nblk = n_tok // HALO
    prev = pl.BlockSpec((None, HALO, width), lambda b, i, *_: (b, jnp.maximum(i * per - 1, 0), 0))
    nxt = pl.BlockSpec((None, HALO, width), lambda b, i, *_: (b, jnp.minimum((i + 1) * per, nblk - 1), 0))
    return prev, nxt


def _ada_kernel(c_ref, w_ref, b_ref, o_ref):
    s = _silu(c_ref[...])
    o_ref[...] = _mm(s, w_ref[...], 3) + b_ref[...]


def _ada(cc, ada_w, ada_b):
    depth = ada_w.shape[0]
    d = D_MODEL
    return pl.pallas_call(
        _ada_kernel,
        name="ada_mod",
        grid=(depth, 6),
        in_specs=[
            pl.BlockSpec((8, d), lambda l, j: (0, 0)),
            pl.BlockSpec((None, d, d), lambda l, j: (l, 0, j)),
            pl.BlockSpec((None, 1, d), lambda l, j: (l, 0, j)),
        ],
        out_specs=pl.BlockSpec((None, 8, d), lambda l, j: (l, 0, j)),
        out_shape=jax.ShapeDtypeStruct((depth, 8, 6 * d), F32),
        compiler_params=_cparams("arbitrary", "arbitrary"),
    )(cc, ada_w, ada_b.reshape(depth, 1, 6 * d))


def _normmod_kernel(x_ref, g_ref, sh_ref, sc_ref, o_ref):
    o_ref[...] = _norm_mod(x_ref[...], g_ref[...], sh_ref[...], sc_ref[...]).astype(BF16)


def _normmod(x, g3, mods, l, row, j_shift, tm):
    b, n, d = x.shape
    tile = pl.BlockSpec((None, tm, d), lambda bb, i: (bb, i, 0))
    return pl.pallas_call(
        _normmod_kernel,
        name="norm_mod",
        grid=(b, n // tm),
        in_specs=[tile, _layer_spec((1, d), l), _mod_spec(d, l, j_shift, row), _mod_spec(d, l, j_shift + 1, row)],
        out_specs=tile,
        out_shape=jax.ShapeDtypeStruct((b, n, d), BF16),
        compiler_params=_cparams("parallel", "parallel"),
    )(x, g3, mods, mods)


def _proj_a_kernel(h_ref, wa_ref, wg_ref, o_ref):
    h = h_ref[...]
    a = _dot(h, wa_ref[...])
    gate = _dot(h, wg_ref[...])
    o_ref[...] = a * _sigmoid(gate)


def _proj_a(h, w_bf, l, tm):
    b, n, d = h.shape
    return pl.pallas_call(
        _proj_a_kernel,
        name="proj_a",
        grid=(b, n // tm),
        in_specs=[
            pl.BlockSpec((None, tm, d), lambda bb, i: (bb, i, 0)),
            _wcols_spec(d, D_A, l, 0), _wcols_spec(d, D_A, l, 1),
        ],
        out_specs=pl.BlockSpec((None, tm, D_A), lambda bb, i: (bb, i, 0)),
        out_shape=jax.ShapeDtypeStruct((b, n, D_A), F32),
        compiler_params=_cparams("parallel", "parallel"),
    )(h, w_bf, w_bf)


def _conv_a_kernel(yp_ref, y_ref, yn_ref, cw_ref, cb_ref, lg_ref, lb_ref, o_ref, ext_ref, *, tm):
    i = pl.program_id(1)
    last = pl.num_programs(1) - 1
    ext_ref[0, 0:HALO] = yp_ref[...] * (i > 0).astype(F32)
    ext_ref[0, HALO:HALO + tm] = y_ref[...]
    ext_ref[0, HALO + tm:2 * HALO + tm] = yn_ref[...] * (i < last).astype(F32)
    n_keep = tm + 2 * HALO - SUBLANES
    for r in range(1, SUBLANES):
        ext_ref[r, 0:n_keep] = ext_ref[0, pl.ds(r, n_keep), :]
    pad = CONV_A // 2
    acc = jnp.zeros((tm, D_A), F32) + cb_ref[...]
    for k in range(CONV_A):
        off = HALO - pad + k
        r = off % SUBLANES
        acc = acc + ext_ref[r, pl.ds(off - r, tm), :] * cw_ref[k:k + 1, :]
    mu = jnp.mean(acc, axis=-1, keepdims=True)
    cen = acc - mu
    var = jnp.mean(cen * cen, axis=-1, keepdims=True)
    y = cen * lax.rsqrt(var + EPS) * lg_ref[...] + lb_ref[...]
    o_ref[...] = _silu(y).astype(BF16)


def _conv_a(y, cw3, cb3, lg3, lb3, l, tm):
    b, n, c = y.shape
    prev, nxt = _halo_specs(tm, n, c)
    vec = _layer_spec((1, c), l)
    return pl.pallas_call(
        functools.partial(_conv_a_kernel, tm=tm),
        name="conv_a",
        grid=(b, n // tm),
        in_specs=[prev, pl.BlockSpec((None, tm, c), lambda bb, i: (bb, i, 0)), nxt,
                  _layer_spec((CONV_A, c), l), vec, vec, vec],
        out_specs=pl.BlockSpec((None, tm, c), lambda bb, i: (bb, i, 0)),
        out_shape=jax.ShapeDtypeStruct((b, n, c), BF16),
        scratch_shapes=[pltpu.VMEM((SUBLANES, tm + 2 * HALO, c), F32)],
        compiler_params=_cparams("parallel", "parallel"),
    )(y, y, y, cw3, cb3, lg3, lb3)


def _proj_b_kernel(h_ref, wq_ref, wk_ref, wv_ref, gq_ref, gk_ref, gm_ref, q_ref, k_ref, v_ref):
    h = h_ref[...]
    for w_ref, gain_ref, o_ref in ((wq_ref, gq_ref, q_ref), (wk_ref, gk_ref, k_ref), (wv_ref, None, v_ref)):
        acc = _dot(h, w_ref[...])
        if gain_ref is not None:
            ss = _dot((acc * acc).astype(BF16), gm_ref[...])
            acc = acc * lax.rsqrt(ss * (1.0 / DH_B) + EPS) * gain_ref[...]
        for hh in range(H_B):
            o_ref[hh] = acc[:, hh * DH_B:(hh + 1) * DH_B].astype(BF16)


def _proj_b(h, w_bf, gq3, gk3, gmat, l, tm):
    b, n, d = h.shape
    vec = _layer_spec((1, D_B), l)
    first = 2 * D_A // D_B
    head_out = pl.BlockSpec((None, H_B, tm, DH_B), lambda bb, i: (bb, 0, i, 0))
    shp = jax.ShapeDtypeStruct((b, H_B, n, DH_B), BF16)
    return pl.pallas_call(
        _proj_b_kernel,
        name="proj_b",
        grid=(b, n // tm),
        in_specs=[
            pl.BlockSpec((None, tm, d), lambda bb, i: (bb, i, 0)),
            _wcols_spec(d, D_B, l, first), _wcols_spec(d, D_B, l, first + 1), _wcols_spec(d, D_B, l, first + 2),
            vec, vec,
            pl.BlockSpec((D_B, D_B), lambda bb, i: (0, 0)),
        ],
        out_specs=[head_out, head_out, head_out],
        out_shape=[shp, shp, shp],
        compiler_params=_cparams("parallel", "parallel"),
    )(h, w_bf, w_bf, w_bf, gq3, gk3, gmat)


def _na_kernel(q_ref, k_ref, v_ref, kc_ref, vc_ref, bias_ref, o_ref, *, rb, rows):
    i = pl.program_id(2)
    n_loc = NA_ROWS * GRID_W
    rws = [i * rb + rr for rr in range(rb)]
    starts = [jnp.clip(r - NA_ROWS // 2, 0, rows - NA_ROWS) for r in rws]
    variants = [st - r + NA_ROWS - 1 for st, r in zip(starts, rws)]
    tok0 = [pl.multiple_of(st * GRID_W, GRID_W) for st in starts]
    probs = [(hh, rr) for hh in range(HEADS_PER_STEP) for rr in range(rb)]
    qs = [q_ref[hh, rr * GRID_W:(rr + 1) * GRID_W, :] for hh, rr in probs]
    s = [_dot_nt(q, k_ref[hh, pl.ds(tok0[rr], n_loc), :]) for q, (hh, rr) in zip(qs, probs)]
    sc = [_dot_nt(q, kc_ref[hh]) for q, (hh, rr) in zip(qs, probs)]
    s = [a + bias_ref[hh, variants[rr]] for a, (hh, rr) in zip(s, probs)]
    m = [jnp.maximum(jnp.max(a, axis=-1, keepdims=True), jnp.max(b, axis=-1, keepdims=True)) for a, b in zip(s, sc)]
    p = [jnp.exp2(a - mm) for a, mm in zip(s, m)]
    pc = [jnp.exp2(b - mm) for b, mm in zip(sc, m)]
    l = [jnp.sum(a, axis=-1, keepdims=True) + jnp.sum(b, axis=-1, keepdims=True) for a, b in zip(p, pc)]
    o = [_dot(a.astype(BF16), v_ref[hh, pl.ds(tok0[rr], n_loc), :]) + _dot(b.astype(BF16), vc_ref[hh])
         for a, b, (hh, rr) in zip(p, pc, probs)]
    o = [a / b for a, b in zip(o, l)]
    for rr in range(rb):
        o_ref[rr * GRID_W:(rr + 1) * GRID_W, :] = jnp.concatenate(
            [o[probs.index((hh, rr))] for hh in range(HEADS_PER_STEP)], axis=1).astype(BF16)


def _na_attention(q, k, v, kc, vc, bias, l, rb):
    b, h, t, dh = q.shape
    n_ctx = kc.shape[2]
    rows = t // GRID_W
    hs = HEADS_PER_STEP
    full = pl.BlockSpec((None, hs, t, dh), lambda bb, hp, i: (bb, hp, 0, 0))
    cfull = pl.BlockSpec((None, hs, n_ctx, dh), lambda bb, hp, i: (bb, hp, 0, 0))
    return pl.pallas_call(
        functools.partial(_na_kernel, rb=rb, rows=rows),
        name="na_attn",
        grid=(b, h // hs, rows // rb),
        in_specs=[pl.BlockSpec((None, hs, rb * GRID_W, dh), lambda bb, hp, i: (bb, hp, i, 0)),
                  full, full, cfull, cfull,
                  pl.BlockSpec((None, hs, NA_ROWS, GRID_W, NA_ROWS * GRID_W),
                               lambda bb, hp, i: (l, hp, 0, 0, 0))],
        out_specs=pl.BlockSpec((None, rb * GRID_W, hs * dh), lambda bb, hp, i: (bb, i, hp)),
        out_shape=jax.ShapeDtypeStruct((b, t, h * dh), BF16),
        compiler_params=_cparams("parallel", "parallel", "arbitrary"),
    )(q, k, v, kc, vc, bias)


def _ctx_attn_kernel(q_ref, k_ref, v_ref, o_ref):
    outs = []
    for hh in range(HEADS_PER_STEP):
        s = _dot_nt(q_ref[hh], k_ref[hh])
        m = jnp.max(s, axis=-1, keepdims=True)
        p = jnp.exp2(s - m)
        l = jnp.sum(p, axis=-1, keepdims=True)
        outs.append(_dot(p.astype(BF16), v_ref[hh]) / l)
    o_ref[...] = jnp.concatenate(outs, axis=1).astype(BF16)


def _ctx_attention(q, k, v):
    b, h, n, dh = q.shape
    hs = HEADS_PER_STEP
    full = pl.BlockSpec((None, hs, n, dh), lambda bb, hp: (bb, hp, 0, 0))
    return pl.pallas_call(
        _ctx_attn_kernel,
        name="ctx_attn",
        grid=(b, h // hs),
        in_specs=[full, full, full],
        out_specs=pl.BlockSpec((None, n, hs * dh), lambda bb, hp: (bb, 0, hp)),
        out_shape=jax.ShapeDtypeStruct((b, n, h * dh), BF16),
        compiler_params=_cparams("parallel", "parallel"),
    )(q, k, v)


def _na_bias_table(rpb):
    depth = rpb.shape[0]
    cidx = np.arange(GRID_W)
    dc = np.clip(cidx[None, :] - cidx[:, None] + NA_COLS - 1, 0, 2 * NA_COLS - 2)
    onehot = (dc[None] == np.arange(2 * NA_COLS - 1)[:, None, None]).astype(np.float32)
    cs = np.clip(cidx - NA_COLS // 2, 0, GRID_W - NA_COLS)
    col_ok = (cidx[None, :] >= cs[:, None]) & (cidx[None, :] < cs[:, None] + NA_COLS)
    toep = jnp.einsum("lhrd,dqk->lhrqk", rpb, onehot, precision=lax.Precision.HIGHEST)
    toep = jnp.where(col_ok, toep * LOG2E, NEG_INF)
    tabs = [jnp.transpose(toep[:, :, v:v + NA_ROWS], (0, 1, 3, 2, 4)).reshape(depth, H_B, GRID_W, NA_ROWS * GRID_W)
            for v in range(NA_ROWS)]
    return jnp.stack(tabs, axis=2)


def _proj_c_kernel(hp_ref, h_ref, hn_ref, wq_ref, wk_ref, wv_ref, wo_ref, wdb_ref, cw_ref, rc_ref, rs_ref,
                   gm_ref, gbp_ref, q_ref, k_ref, v_ref, og_ref, gb_ref, p_ref, *, tm, use_rope):
    i = pl.program_id(1)
    last = pl.num_programs(1) - 1
    hp = jnp.where(i > 0, hp_ref[...], jnp.zeros_like(hp_ref))
    hn = jnp.where(i < last, hn_ref[...], jnp.zeros_like(hn_ref))
    hc = h_ref[...]
    pad_l = SHORT_CONV // 2
    n_ext = tm + 2 * HALO
    lane = lax.broadcasted_iota(jnp.int32, (tm, DK_C), 1)
    first_half = (lane & (DK_C // 2 - 1)) < DK_C // 4
    for sec, (w_ref, o_ref) in enumerate(((wq_ref, q_ref), (wk_ref, k_ref), (wv_ref, v_ref))):
        cols = slice(sec * D_CQK, (sec + 1) * D_CQK)
        p_ref[0:HALO] = _dot(hp, w_ref[...])
        p_ref[HALO:HALO + tm] = _dot(hc, w_ref[...])
        p_ref[HALO + tm:n_ext] = _dot(hn, w_ref[...])
        pv = p_ref[...]
        y = None
        for kk in range(SHORT_CONV):
            sh = (pad_l - kk) % n_ext
            pk = pv if sh == 0 else pltpu.roll(pv, sh, axis=0)
            t = pk[HALO:HALO + tm] * cw_ref[kk:kk + 1, cols]
            y = t if y is None else y + t
        y = _silu(y)
        if sec < 2:
            ss = _dot((y * y).astype(BF16), gm_ref[...])
            y = y * lax.rsqrt(ss + EPS)
            if use_rope:
                heads = []
                for hh in range(H_C):
                    yh = y[:, hh * DK_C:(hh + 1) * DK_C]
                    swapped = jnp.where(first_half, pltpu.roll(yh, DK_C - DK_C // 4, axis=1),
                                        pltpu.roll(yh, DK_C // 4, axis=1))
                    heads.append(yh * rc_ref[...] + swapped * rs_ref[...])
                y = jnp.concatenate(heads, axis=1)
            if sec == 0:
                y = y * (DK_C ** -0.5)
        o_ref[...] = y
    og_ref[...] = _dot(hc, wo_ref[...])
    db = _dot(hc, wdb_ref[...])
    gval = -jnp.exp(gbp_ref[0:1, :]) * _softplus(db + gbp_ref[1:2, :])
    gb_ref[...] = gbp_ref[2:3, :] * gval + gbp_ref[3:4, :] * _sigmoid(db)


def _proj_c(h, w_bf, cw3, rc, rs, gmat, gbp3, l, tm, use_rope):
    b, n, d = h.shape
    first = (2 * D_A + 3 * D_B) // D_CQK
    db_block = (2 * D_A + 3 * D_B + 3 * D_CQK + D_CV) // DK_C
    prev, nxt = _halo_specs(tm, n, d)
    tile512 = pl.BlockSpec((None, tm, D_CQK), lambda bb, i: (bb, i, 0))
    shp = jax.ShapeDtypeStruct((b, n, D_CQK), F32)
    rope_spec = pl.BlockSpec((tm, DK_C), lambda bb, i: (i, 0))
    return pl.pallas_call(
        functools.partial(_proj_c_kernel, tm=tm, use_rope=use_rope),
        name="proj_c",
        grid=(b, n // tm),
        in_specs=[
            prev, pl.BlockSpec((None, tm, d), lambda bb, i: (bb, i, 0)), nxt,
            _wcols_spec(d, D_CQK, l, first), _wcols_spec(d, D_CQK, l, first + 1),
            _wcols_spec(d, D_CQK, l, first + 2), _wcols_spec(d, D_CQK, l, first + 3),
            _wcols_spec(d, DK_C, l, db_block),
            _layer_spec((SHORT_CONV, 3 * D_CQK), l),
            rope_spec, rope_spec,
            pl.BlockSpec((D_CQK, D_CQK), lambda bb, i: (0, 0)),
            _layer_spec((8, DK_C), l),
        ],
        out_specs=[tile512] * 4 + [pl.BlockSpec((None, tm, DK_C), lambda bb, i: (bb, i, 0))],
        out_shape=[shp] * 4 + [jax.ShapeDtypeStruct((b, n, DK_C), F32)],
        scratch_shapes=[pltpu.VMEM((tm + 2 * HALO, D_CQK), F32)],
        compiler_params=_cparams("parallel", "parallel"),
    )(h, h, h, w_bf, w_bf, w_bf, w_bf, w_bf, cw3, rc, rs, gmat, gbp3)


def _chunk_masks(n):
    r = np.arange(n)[:, None]
    c = np.arange(n)[None, :]
    same = lambda s: (r // s) == (c // s)
    chunk = same(CHUNK)
    ms = [chunk & (r >= c), chunk & (r > c), chunk & (r <= c), chunk & (r < c), same(8),
          same(16) & ~same(8), same(32) & ~same(16), same(64) & ~same(32)]
    return np.stack(ms).astype(np.float32)


def _tri_inv_all(lmats, eye, m_ref):
    m8 = m_ref[4]
    n0 = [-(l * m8) for l in lmats]
    n2 = [_mm(a, a) for a in n0]
    n4 = [_mm(a, a) for a in n2]
    n3 = [_mm(a, b) for a, b in zip(n0, n2)]
    t1 = [eye + a + b + c for a, b, c in zip(n0, n2, n3)]
    t1n4 = [_mm(a, b) for a, b in zip(t1, n4)]
    t = [a + b for a, b in zip(t1, t1n4)]
    for lvl in (5, 6, 7):
        off = m_ref[lvl]
        lt = [_mm(l * off, a) for l, a in zip(lmats, t)]
        tlt = [_mm(a, b) for a, b in zip(t, lt)]
        t = [a - b for a, b in zip(t, tlt)]
    return t


def _gdn_step(io, m_ref, cum_ref):
    nsub = GDN_BLOCK // GDN_SUB
    csub = GDN_SUB // CHUNK
    nch = nsub * csub
    r16, cols = [], []
    for d in range(2):
        g_t = io[d][3][...].T[0:4 * H_C]
        csum = _dot_exact_rhs(g_t, cum_ref[d], 3)
        rid = lax.broadcasted_iota(jnp.int32, g_t.shape, 0)
        r = jnp.where(rid // H_C == d, csum, g_t)
        r16.append(r)
        cols.append(jnp.concatenate([r, jnp.zeros((DK_C - 4 * H_C, GDN_BLOCK), F32)], axis=0).T)
    probs = [(d, h, sb) for d in range(2) for h in range(H_C) for sb in range(nsub)]
    incl = [m_ref[0], m_ref[2]]
    strict = [m_ref[1], m_ref[3]]
    eye = incl[0] - strict[0]

    def tile(ref, h, sb):
        return ref[sb * GDN_SUB:(sb + 1) * GDN_SUB, h * DK_C:(h + 1) * DK_C]

    q = [tile(io[d][0], h, sb) for d, h, sb in probs]
    k = [tile(io[d][1], h, sb) for d, h, sb in probs]
    v = [tile(io[d][2], h, sb) for d, h, sb in probs]
    gcol, bcol, decay = [], [], []
    for d, h, sb in probs:
        rows = slice(sb * GDN_SUB, (sb + 1) * GDN_SUB)
        c_g = d * H_C + h
        c_b = 2 * H_C + c_g
        gc = jnp.broadcast_to(cols[d][rows, c_g:c_g + 1], (GDN_SUB, DK_C))
        gcol.append(gc)
        bcol.append(jnp.broadcast_to(cols[d][rows, c_b:c_b + 1], (GDN_SUB, DK_C)))
        decay.append(jnp.exp(jnp.where(incl[d] > 0.5, gc - r16[d][c_g:c_g + 1, rows], NEG_INF)))
    k16 = [a.astype(BF16) for a in k]
    qkk = [_dot_nt(jnp.concatenate([a.astype(BF16), b], axis=0), b) for a, b in zip(q, k16)]
    qk = [a[:GDN_SUB] for a in qkk]
    kk = [a[GDN_SUB:] for a in qkk]
    lmats = [a * b * (c * strict[p[0]]) for a, b, c, p in zip(kk, bcol, decay, probs)]
    amat = [(a * c).astype(BF16) for a, c in zip(qk, decay)]
    tinv = _tri_inv_all(lmats, eye, m_ref)
    eg = [jnp.exp(a) for a in gcol]
    rhs = [jnp.concatenate([vv * b, kx * (b * e)], axis=1).astype(BF16) for vv, kx, b, e in zip(v, k, bcol, eg)]
    sol = [_dot(t.astype(BF16), r) for t, r in zip(tinv, rhs)]
    qd = [a * e for a, e in zip(q, eg)]
    gtot, kd_t = [], []
    for (d, h, sb), kx, gc in zip(probs, k, gcol):
        parts, gts = [], []
        for c in range(csub):
            end = c * CHUNK + (CHUNK - 1 if d == 0 else 0)
            gt = gc[end:end + 1, :]
            gts.append(gt)
            sl = slice(c * CHUNK, (c + 1) * CHUNK)
            parts.append(kx[sl] * jnp.exp(gt - gc[sl]))
        gtot.append(gts)
        kd_t.append(jnp.concatenate(parts, axis=0).T.astype(BF16))
    pidx = {p: n for n, p in enumerate(probs)}
    chains = [(d, h) for d in range(2) for h in range(H_C)]
    state = [io[d][4][h] for d, h in chains]
    vnew = [[None] * nch for _ in chains]
    ointer = [[None] * nch for _ in chains]
    zeros = jnp.zeros((CHUNK, DV_C), BF16)
    for step in range(nch):
        pos = [(step if d == 0 else nch - 1 - step) for d, _ in chains]
        loc = [(pidx[(d, h, n // csub)], n % csub) for (d, h), n in zip(chains, pos)]
        sls = [slice(c * CHUNK, (c + 1) * CHUNK) for _, c in loc]
        wq = [jnp.concatenate([sol[p][sl, DV_C:], qd[p][sl]], axis=0).astype(BF16) for (p, _), sl in zip(loc, sls)]
        r = [_dot(a, s.astype(BF16)) for a, s in zip(wq, state)]
        vn = [(sol[p][sl, :DV_C] - rr[0:CHUNK]).astype(BF16) for (p, _), sl, rr in zip(loc, sls, r)]
        vpad = [jnp.concatenate([x if m == c else zeros for m in range(csub)], axis=0) for x, (_, c) in zip(vn, loc)]
        upd = [_dot(kd_t[p], x) for (p, _), x in zip(loc, vpad)]
        state = [s * jnp.exp(gtot[p][c]) + u for s, (p, c), u in zip(state, loc, upd)]
        for ci, n in enumerate(pos):
            vnew[ci][n] = vn[ci]
            ointer[ci][n] = r[ci][CHUNK:2 * CHUNK]
    for (d, h), s in zip(chains, state):
        io[d][4][h] = s
    out = [[None] * H_C for _ in range(2)]
    for ci, (d, h) in enumerate(chains):
        parts = []
        for sb in range(nsub):
            p = pidx[(d, h, sb)]
            vn_sb = jnp.concatenate(vnew[ci][sb * csub:(sb + 1) * csub], axis=0)
            parts.append(jnp.concatenate(ointer[ci][sb * csub:(sb + 1) * csub], axis=0) + _dot(amat[p], vn_sb))
        out[d][h] = jnp.concatenate(parts, axis=0)
    return out


def _gdn_finalize(o_heads, gate_ref, gain):
    ys = []
    for h, o in enumerate(o_heads):
        ms = jnp.mean(o * o, axis=-1, keepdims=True)
        ys.append(o * lax.rsqrt(ms + EPS) * gain * _silu(gate_ref[:, h * DV_C:(h + 1) * DV_C]))
    return jnp.concatenate(ys, axis=1).astype(BF16)


def _gdn_kernel(qf_ref, kf_ref, vf_ref, gf_ref, ogf_ref, qb_ref, kb_ref, vb_ref, gb_ref, ogb_ref,
                s0f_ref, s0b_ref, m_ref, cum_ref, gain_ref,
                y_ref, sfo_ref, sbo_ref, oacc_ref, sf_ref, sb_ref, *, nb):
    i = pl.program_id(1)

    @pl.when(i == 0)
    def _():
        sf_ref[...] = s0f_ref[...]
        sb_ref[...] = s0b_ref[...]

    of, ob = _gdn_step(((qf_ref, kf_ref, vf_ref, gf_ref, sf_ref), (qb_ref, kb_ref, vb_ref, gb_ref, sb_ref)),
                       m_ref, cum_ref)
    rows_f = pl.ds(pl.multiple_of(i * GDN_BLOCK, GDN_BLOCK), GDN_BLOCK)
    rows_b = pl.ds(pl.multiple_of((nb - 1 - i) * GDN_BLOCK, GDN_BLOCK), GDN_BLOCK)
    gain = gain_ref[...]

    def split(x):
        return [x[:, h * DV_C:(h + 1) * DV_C] for h in range(H_C)]

    @pl.when(2 * i < nb - 1)
    def _():
        oacc_ref[rows_f, :] = jnp.concatenate(of, axis=1)
        oacc_ref[rows_b, :] = jnp.concatenate(ob, axis=1)

    if nb % 2 == 1:
        @pl.when(2 * i == nb - 1)
        def _():
            y_ref[rows_f, :] = _gdn_finalize([a + b for a, b in zip(of, ob)], ogf_ref, gain)

    @pl.when(2 * i > nb - 1)
    def _():
        y_ref[rows_f, :] = _gdn_finalize([a + b for a, b in zip(split(oacc_ref[rows_f, :]), of)], ogf_ref, gain)
        y_ref[rows_b, :] = _gdn_finalize([a + b for a, b in zip(split(oacc_ref[rows_b, :]), ob)], ogb_ref, gain)

    @pl.when(i == nb - 1)
    def _():
        sfo_ref[...] = sf_ref[...]
        sbo_ref[...] = sb_ref[...]


def _gdn(q, k, v, gb, og, s0f, s0b, masks, cums, gain3, l):
    b, n, _ = q.shape
    nb = n // GDN_BLOCK
    fwd = pl.BlockSpec((None, GDN_BLOCK, D_CQK), lambda bb, i: (bb, i, 0))
    bwd = pl.BlockSpec((None, GDN_BLOCK, D_CQK), lambda bb, i: (bb, nb - 1 - i, 0))
    gfwd = pl.BlockSpec((None, GDN_BLOCK, DK_C), lambda bb, i: (bb, i, 0))
    gbwd = pl.BlockSpec((None, GDN_BLOCK, DK_C), lambda bb, i: (bb, nb - 1 - i, 0))
    st = pl.BlockSpec((None, H_C, DK_C, DV_C), lambda bb, i: (bb, 0, 0, 0))
    st_shape = jax.ShapeDtypeStruct((b, H_C, DK_C, DV_C), F32)
    return pl.pallas_call(
        functools.partial(_gdn_kernel, nb=nb),
        name="gdn_scan",
        grid=(b, nb),
        in_specs=[fwd, fwd, fwd, gfwd, fwd, bwd, bwd, bwd, gbwd, bwd,
                  st, st,
                  pl.BlockSpec((8, GDN_SUB, GDN_SUB), lambda bb, i: (0, 0, 0)),
                  pl.BlockSpec((2, GDN_BLOCK, GDN_BLOCK), lambda bb, i: (0, 0, 0)),
                  _layer_spec((1, DV_C), l)],
        out_specs=[pl.BlockSpec((None, n, D_CV), lambda bb, i: (bb, 0, 0)), st, st],
        out_shape=[jax.ShapeDtypeStruct((b, n, D_CV), BF16), st_shape, st_shape],
        scratch_shapes=[pltpu.VMEM((n, D_CV), F32), pltpu.VMEM((H_C, DK_C, DV_C), F32),
                        pltpu.VMEM((H_C, DK_C, DV_C), F32)],
        compiler_params=_cparams("parallel", "arbitrary"),
    )(q, k, v, gb, og, q, k, v, gb, og, s0f, s0b, masks, cums, gain3)


def _merge_kernel(h_ref, ya_ref, yb_ref, yc_ref, wga_ref, wgb_ref, wgc_ref, wbr_ref, z_ref):
    h = h_ref[...]
    gates = [_dot(h, w_ref[...]) for w_ref in (wga_ref, wgb_ref, wgc_ref)]
    vals = [_dot(y_ref[...], wbr_ref[br]) for br, y_ref in enumerate((ya_ref, yb_ref, yc_ref))]
    z = None
    for gt, vl in zip(gates, vals):
        t = _sigmoid(gt) * vl
        z = t if z is None else z + t
    z_ref[...] = z.astype(BF16)


def _merge(h, ya, yb, yc, wgates, wbr4, l, tm, tn):
    b, n, d = h.shape
    gate_spec = lambda br: pl.BlockSpec((None, d, tn), lambda bb, i, j: (l, 0, br * (d // tn) + j))
    yt = pl.BlockSpec((None, tm, D_A), lambda bb, i, j: (bb, i, 0))
    return pl.pallas_call(
        _merge_kernel,
        name="merge_gate",
        grid=(b, n // tm, d // tn),
        in_specs=[
            pl.BlockSpec((None, tm, d), lambda bb, i, j: (bb, i, 0)), yt, yt, yt,
            gate_spec(0), gate_spec(1), gate_spec(2),
            pl.BlockSpec((None, 3, D_A, tn), lambda bb, i, j: (l, 0, 0, j)),
        ],
        out_specs=pl.BlockSpec((None, tm, tn), lambda bb, i, j: (bb, i, j)),
        out_shape=jax.ShapeDtypeStruct((b, n, d), BF16),
        compiler_params=_cparams("parallel", "parallel", "parallel"),
    )(h, ya, yb, yc, wgates, wgates, wgates, wbr4)


def _resid_kernel(x_ref, a_ref, gt_ref, w_ref, *rest, with_norm):
    xn = x_ref[...] + gt_ref[...] * _dot(a_ref[...], w_ref[...])
    if with_norm:
        g_ref, sh_ref, sc_ref, o_ref, h_ref = rest
        h_ref[...] = _norm_mod(xn, g_ref[...], sh_ref[...], sc_ref[...]).astype(BF16)
    else:
        (o_ref,) = rest
    o_ref[...] = xn


def _resid_mm(x, a, mods, l, row, j_gate, w3, norm, tm):
    b, n, d = x.shape
    kdim = a.shape[-1]
    xt = pl.BlockSpec((None, tm, d), lambda bb, i: (bb, i, 0))
    in_specs = [xt, pl.BlockSpec((None, tm, kdim), lambda bb, i: (bb, i, 0)), _mod_spec(d, l, j_gate, row),
                _layer_spec((kdim, d), l)]
    args = [x, a, mods, w3]
    out_specs, out_shape = [xt], [jax.ShapeDtypeStruct((b, n, d), F32)]
    if norm is not None:
        g3, ln, j_shift = norm
        in_specs += [_layer_spec((1, d), ln), _mod_spec(d, ln, j_shift, row), _mod_spec(d, ln, j_shift + 1, row)]
        args += [g3, mods, mods]
        out_specs.append(xt)
        out_shape.append(jax.ShapeDtypeStruct((b, n, d), BF16))
    res = pl.pallas_call(
        functools.partial(_resid_kernel, with_norm=norm is not None),
        name="resid_mm",
        grid=(b, n // tm),
        in_specs=in_specs,
        out_specs=out_specs,
        out_shape=out_shape,
        compiler_params=_cparams("parallel", "parallel"),
    )(*args)
    return (res[0], res[1]) if norm is not None else (res[0], None)


def _ffn_up_kernel(hp_ref, h_ref, hn_ref, wup_ref, cw_ref, cb_ref, o_ref, ug_ref, uv_ref, *, tm, tf):
    i = pl.program_id(1)
    last = pl.num_programs(1) - 1
    hp = jnp.where(i > 0, hp_ref[...], jnp.zeros_like(hp_ref))
    hn = jnp.where(i < last, hn_ref[...], jnp.zeros_like(hn_ref))
    hc = h_ref[...]
    pad = FFN_CONV // 2
    n_ext = tm + 2 * HALO
    nf = D_FF // tf

    def project(j):
        slot = j % 2
        for u_ref, c0 in ((ug_ref, j * tf), (uv_ref, D_FF + j * tf)):
            w = wup_ref[:, c0:c0 + tf]
            u_ref[slot, 0:HALO] = _dot(hp, w)
            u_ref[slot, HALO:HALO + tm] = _dot(hc, w)
            u_ref[slot, HALO + tm:n_ext] = _dot(hn, w)

    def conv(j):
        slot = j % 2
        ug = ug_ref[slot]
        uv = uv_ref[slot]
        gc = slice(j * tf, (j + 1) * tf)
        vc = slice(D_FF + j * tf, D_FF + (j + 1) * tf)
        cg = cb_ref[:, gc]
        cv = cb_ref[:, vc]
        for kk in range(FFN_CONV):
            sh = (pad - kk) % n_ext
            ugk = ug if sh == 0 else pltpu.roll(ug, sh, axis=0)
            uvk = uv if sh == 0 else pltpu.roll(uv, sh, axis=0)
            cg = cg + ugk[HALO:HALO + tm] * cw_ref[kk:kk + 1, gc]
            cv = cv + uvk[HALO:HALO + tm] * cw_ref[kk:kk + 1, vc]
        o_ref[:, gc] = (_silu(cg) * cv).astype(BF16)

    project(0)
    for j in range(1, nf):
        project(j)
        conv(j - 1)
    conv(nf - 1)


def _ffn_up(h, wup3, cw3, cb3, l, tm, tf):
    b, n, d = h.shape
    prev, nxt = _halo_specs(tm, n, d)
    const = lambda shape: _layer_spec(shape, l, pipeline_mode=pl.Buffered(1))
    return pl.pallas_call(
        functools.partial(_ffn_up_kernel, tm=tm, tf=tf),
        name="ffn_up",
        grid=(b, n // tm),
        in_specs=[
            prev, pl.BlockSpec((None, tm, d), lambda bb, i: (bb, i, 0)), nxt,
            const((d, 2 * D_FF)), const((FFN_CONV, 2 * D_FF)), const((1, 2 * D_FF)),
        ],
        out_specs=pl.BlockSpec((None, tm, D_FF), lambda bb, i: (bb, i, 0)),
        out_shape=jax.ShapeDtypeStruct((b, n, D_FF), BF16),
        scratch_shapes=[pltpu.VMEM((2, tm + 2 * HALO, tf), F32), pltpu.VMEM((2, tm + 2 * HALO, tf), F32)],
        compiler_params=_cparams("parallel", "parallel"),
    )(h, h, h, wup3, cw3, cb3)


def _rope_tables(n_tok):
    t = jnp.arange(n_tok)
    row = (t // GRID_W).astype(F32)
    col = (t % GRID_W).astype(F32)
    n_freq = DK_C // 4
    inv = jnp.power(ROPE_BASE, -jnp.arange(n_freq, dtype=F32) / n_freq)
    ar = row[:, None] * inv
    ac = col[:, None] * inv
    cos = jnp.concatenate([jnp.cos(ar), jnp.cos(ar), jnp.cos(ac), jnp.cos(ac)], axis=-1)
    sin = jnp.concatenate([-jnp.sin(ar), jnp.sin(ar), -jnp.sin(ac), jnp.sin(ac)], axis=-1)
    return cos, sin


def _block_ones(n, blk):
    idx = np.arange(n) // blk
    return jnp.asarray((idx[:, None] == idx[None, :]).astype(np.float32), dtype=BF16)


def kernel(x, c, ctx, c_ctx, ada_w, ada_b, norm1_g, norm2_g, w_in, conv_a_w, conv_a_b, ln_a_g, ln_a_b, qn_g, kn_g,
           rpb, conv_c_w, a_log, dt_bias, onorm_g, w_branch, w_out, ffn_up, ffn_conv_w, ffn_conv_b, ffn_down):
    batch, n_lat, d = x.shape
    n_ctx = ctx.shape[1]
    depth = ada_w.shape[0]

    cc = jnp.zeros((8, d), F32).at[:batch].set(c).at[batch].set(c_ctx)
    mods = _ada(cc, ada_w, ada_b).reshape(depth, 8, 6, 1, d)
    ctx_row = batch

    rope_c, rope_s = _rope_tables(n_lat)
    ones_c = jnp.ones((n_ctx, DK_C), F32)
    zeros_c = jnp.zeros((n_ctx, DK_C), F32)
    gm64 = _block_ones(D_B, DH_B)
    gm128 = _block_ones(D_CQK, DK_C)
    masks = jnp.asarray(_chunk_masks(GDN_SUB))
    blk_masks = _chunk_masks(GDN_BLOCK)
    cums = jnp.asarray(np.stack([blk_masks[0].T, blk_masks[2].T]), dtype=BF16)
    s_zero = jnp.zeros((batch, H_C, DK_C, DV_C), F32)

    off_g = 2 * D_A + 3 * D_B + 3 * D_CQK + D_CV + 4 * H_C
    w_bf = w_in.astype(BF16)
    wgates = w_bf[:, :, off_g:]
    wbr4 = w_branch.astype(BF16)
    wo3 = w_out.astype(BF16)
    wup3 = ffn_up.astype(BF16)
    wdn3 = ffn_down.astype(BF16)
    g1_3 = norm1_g.reshape(depth, 1, d)
    g2_3 = norm2_g.reshape(depth, 1, d)
    gq3 = (jnp.tile(qn_g, (1, H_B)) * (DH_B ** -0.5 * LOG2E)).reshape(depth, 1, D_B)
    gk3 = jnp.tile(kn_g, (1, H_B)).reshape(depth, 1, D_B)
    gbp3 = jnp.zeros((depth, 8, DK_C), F32)
    gbp3 = gbp3.at[:, 0, :2 * H_C].set(a_log.reshape(depth, 2 * H_C)).at[:, 1, :2 * H_C].set(
        dt_bias.reshape(depth, 2 * H_C)).at[:, 2, :2 * H_C].set(1.0).at[:, 3, 2 * H_C:4 * H_C].set(1.0)
    bias5 = _na_bias_table(rpb)
    cb_a3 = conv_a_b.reshape(depth, 1, D_A)
    lg_a3 = ln_a_g.reshape(depth, 1, D_A)
    lb_a3 = ln_a_b.reshape(depth, 1, D_A)
    cbf3 = ffn_conv_b.reshape(depth, 1, 2 * D_FF)
    gain3 = onorm_g.reshape(depth, 1, DV_C)

    tl = _tiles(n_lat)
    tc = _tiles(n_ctx)
    x_lat, x_ctx = x, ctx
    h_lat = _normmod(x_lat, g1_3, mods, 0, None, 0, tl["norm"])
    h_ctx = _normmod(x_ctx, g1_3, mods, 0, ctx_row, 0, tc["norm"])
    for l in range(depth):
        ctx_out = l < depth - 1
        next_norm = (g1_3, l + 1, 0) if ctx_out else None

        qb_c, kb_c, vb_c = _proj_b(h_ctx, w_bf, gq3, gk3, gm64, l, tc["proj"])
        qc_c, kc_c, vc_c, og_c, gb_c = _proj_c(h_ctx, w_bf, conv_c_w, ones_c, zeros_c, gm128, gbp3, l,
                                               tc["proj"], False)
        yc_c, sf_c, sb_c = _gdn(qc_c, kc_c, vc_c, gb_c, og_c, s_zero, s_zero, masks, cums, gain3, l)

        ya_l = _conv_a(_proj_a(h_lat, w_bf, l, tl["proj"]), conv_a_w, cb_a3, lg_a3, lb_a3, l, tl["conv_a"])
        qb_l, kb_l, vb_l = _proj_b(h_lat, w_bf, gq3, gk3, gm64, l, tl["proj"])
        yb_l = _na_attention(qb_l, kb_l, vb_l, kb_c, vb_c, bias5, l, NA_ROWS_PER_STEP)
        qc_l, kc_l, vc_l, og_l, gb_l = _proj_c(h_lat, w_bf, conv_c_w, rope_c, rope_s, gm128, gbp3, l,
                                               tl["proj"], True)
        yc_l, _, _ = _gdn(qc_l, kc_l, vc_l, gb_l, og_l, sf_c, sb_c, masks, cums, gain3, l)
        z_l = _merge(h_lat, ya_l, yb_l, yc_l, wgates, wbr4, l, tl["merge"], MERGE_TN)
        x_lat, h2_l = _resid_mm(x_lat, z_l, mods, l, None, 2, wo3, (g2_3, l, 3), tl["resid"])
        act_l = _ffn_up(h2_l, wup3, ffn_conv_w, cbf3, l, tl["ffn"], FFN_TF)
        x_lat, h_lat = _resid_mm(x_lat, act_l, mods, l, None, 5, wdn3, next_norm, tl["resid"])

        if ctx_out:
            ya_c = _conv_a(_proj_a(h_ctx, w_bf, l, tc["proj"]), conv_a_w, cb_a3, lg_a3, lb_a3, l, tc["conv_a"])
            yb_c = _ctx_attention(qb_c, kb_c, vb_c)
            z_c = _merge(h_ctx, ya_c, yb_c, yc_c, wgates, wbr4, l, tc["merge"], MERGE_TN)
            x_ctx, h2_c = _resid_mm(x_ctx, z_c, mods, l, ctx_row, 2, wo3, (g2_3, l, 3), tc["resid"])
            act_c = _ffn_up(h2_c, wup3, ffn_conv_w, cbf3, l, tc["ffn"], FFN_TF)
            x_ctx, h_ctx = _resid_mm(x_ctx, act_c, mods, l, ctx_row, 5, wdn3, next_norm, tc["resid"])
    return x_lat
```

```python
import functools
import math

import numpy as np
import jax
import jax.numpy as jnp
from jax import lax
from jax.experimental import pallas as pl
from jax.experimental.pallas import tpu as pltpu

F32 = jnp.float32
BF16 = jnp.bfloat16

D_MODEL = 1024
GRID_W = 64
EPS = 1e-6
NEG_INF = -1e30
LOG2E = math.log2(math.e)
D_A = 512
CONV_A = 31
H_B = 8
DH_B = 64
D_B = H_B * DH_B
NA_ROWS = 8
NA_COLS = 16
H_C = 4
DK_C = 128
DV_C = 128
D_CQK = H_C * DK_C
D_CV = H_C * DV_C
SHORT_CONV = 4
CHUNK = 64
ROPE_BASE = 10000.0
D_FF = 2816
FFN_CONV = 3

VMEM_LIMIT_BYTES = 48 * 1024 * 1024
HALO = 16
SUBLANES = 8
GDN_BLOCK = 256
GDN_SUB = 128
NA_ROWS_PER_STEP = 8
HEADS_PER_STEP = 2
MERGE_TN = 512
FFN_TF = 256


def _tiles(n_tok):
    cap = lambda t: min(t, n_tok)
    return {"norm": cap(1024), "proj": cap(1024), "conv_a": cap(512), "merge": cap(1024), "resid": cap(1024),
            "ffn": cap(1024)}


def _cparams(*sem):
    return pltpu.CompilerParams(dimension_semantics=sem, vmem_limit_bytes=VMEM_LIMIT_BYTES)


def _dot(a, b):
    return jnp.dot(a, b, preferred_element_type=F32)


def _dot_nt(a, b):
    return lax.dot_general(a, b, (((1,), (1,)), ((), ())), preferred_element_type=F32)


def _split_bf16(x, n):
    parts = []
    r = x
    for idx in range(n):
        p = r.astype(BF16)
        parts.append(p)
        if idx + 1 < n:
            r = r - p.astype(F32)
    return parts


def _dot_exact_rhs(a, b_bf16, n):
    out = None
    for p in _split_bf16(a, n):
        t = _dot(p, b_bf16)
        out = t if out is None else out + t
    return out


def _dot_exact_lhs(a_bf16, b, n):
    out = None
    for p in _split_bf16(b, n):
        t = _dot(a_bf16, p)
        out = t if out is None else out + t
    return out


def _mm(a, b, passes=1):
    if passes == 1:
        return _dot(a.astype(BF16), b.astype(BF16))
    a_hi, a_lo = _split_bf16(a, 2)
    b_hi, b_lo = _split_bf16(b, 2)
    return _dot(a_hi, b_hi) + (_dot(a_lo, b_hi) + _dot(a_hi, b_lo))


def _sigmoid(x):
    return 1.0 / (1.0 + jnp.exp(-x))


def _silu(x):
    return x * _sigmoid(x)


def _softplus(x):
    return jnp.maximum(x, 0.0) + jnp.log(1.0 + jnp.exp(-jnp.abs(x)))


def _norm_mod(x, g, shift, scale):
    ms = jnp.mean(x * x, axis=-1, keepdims=True)
    y = x * lax.rsqrt(ms + EPS) * g
    return y * (1.0 + scale) + shift


def _layer_spec(shape, l, **kw):
    return pl.BlockSpec((None,) + tuple(shape), lambda *_: (l,) + (0,) * len(shape), **kw)


def _wcols_spec(rows, width, l, col_block):
    return pl.BlockSpec((None, rows, width), lambda *_: (l, 0, col_block))


def _mod_spec(d, l, j, row):
    if row is None:
        return pl.BlockSpec((None, None, None, 1, d), lambda bb, *_: (l, bb, j, 0, 0))
    return pl.BlockSpec((None, None, None, 1, d), lambda *_: (l, row, j, 0, 0))


def _halo_specs(tm, n_tok, width):
    per = tm // HALO
    nblk = n_tok // HALO
    prev = pl.BlockSpec((None, HALO, width), lambda b, i, *_: (b, jnp.maximum(i * per - 1, 0), 0))
    nxt = pl.BlockSpec((None, HALO, width), lambda b, i, *_: (b, jnp.minimum((i + 1) * per, nblk - 1), 0))
    return prev, nxt


def _ada_kernel(c_ref, w_ref, b_ref, o_ref):
    s = _silu(c_ref[...])
    o_ref[...] = _mm(s, w_ref[...], 3) + b_ref[...]


def _ada(cc, ada_w, ada_b):
    depth = ada_w.shape[0]
    d = D_MODEL
    return pl.pallas_call(
        _ada_kernel,
        name="ada_mod",
        grid=(depth, 6),
        in_specs=[
            pl.BlockSpec((8, d), lambda l, j: (0, 0)),
            pl.BlockSpec((None, d, d), lambda l, j: (l, 0, j)),
            pl.BlockSpec((None, 1, d), lambda l, j: (l, 0, j)),
        ],
        out_specs=pl.BlockSpec((None, 8, d), lambda l, j: (l, 0, j)),
        out_shape=jax.ShapeDtypeStruct((depth, 8, 6 * d), F32),
        compiler_params=_cparams("arbitrary", "arbitrary"),
    )(cc, ada_w, ada_b.reshape(depth, 1, 6 * d))


def _normmod_kernel(x_ref, g_ref, sh_ref, sc_ref, o_ref):
    o_ref[...] = _norm_mod(x_ref[...], g_ref[...], sh_ref[...], sc_ref[...]).astype(BF16)


def _normmod(x, g3, mods, l, row, j_shift, tm):
    b, n, d = x.shape
    tile = pl.BlockSpec((None, tm, d), lambda bb, i: (bb, i, 0))
    return pl.pallas_call(
        _normmod_kernel,
        name="norm_mod",
        grid=(b, n // tm),
        in_specs=[tile, _layer_spec((1, d), l), _mod_spec(d, l, j_shift, row), _mod_spec(d, l, j_shift + 1, row)],
        out_specs=tile,
        out_shape=jax.ShapeDtypeStruct((b, n, d), BF16),
        compiler_params=_cparams("parallel", "parallel"),
    )(x, g3, mods, mods)


def _proj_a_kernel(h_ref, wa_ref, wg_ref, o_ref):
    h = h_ref[...]
    a = _dot(h, wa_ref[...])
    gate = _dot(h, wg_ref[...])
    o_ref[...] = a * _sigmoid(gate)


def _proj_a(h, w_bf, l, tm):
    b, n, d = h.shape
    return pl.pallas_call(
        _proj_a_kernel,
        name="proj_a",
        grid=(b, n // tm),
        in_specs=[
            pl.BlockSpec((None, tm, d), lambda bb, i: (bb, i, 0)),
            _wcols_spec(d, D_A, l, 0), _wcols_spec(d, D_A, l, 1),
        ],
        out_specs=pl.BlockSpec((None, tm, D_A), lambda bb, i: (bb, i, 0)),
        out_shape=jax.ShapeDtypeStruct((b, n, D_A), F32),
        compiler_params=_cparams("parallel", "parallel"),
    )(h, w_bf, w_bf)


def _conv_a_kernel(yp_ref, y_ref, yn_ref, cw_ref, cb_ref, lg_ref, lb_ref, o_ref, ext_ref, *, tm):
    i = pl.program_id(1)
    last = pl.num_programs(1) - 1
    ext_ref[0, 0:HALO] = yp_ref[...] * (i > 0).astype(F32)
    ext_ref[0, HALO:HALO + tm] = y_ref[...]
    ext_ref[0, HALO + tm:2 * HALO + tm] = yn_ref[...] * (i < last).astype(F32)
    n_keep = tm + 2 * HALO - SUBLANES
    for r in range(1, SUBLANES):
        ext_ref[r, 0:n_keep] = ext_ref[0, pl.ds(r, n_keep), :]
    pad = CONV_A // 2
    acc = jnp.zeros((tm, D_A), F32) + cb_ref[...]
    for k in range(CONV_A):
        off = HALO - pad + k
        r = off % SUBLANES
        acc = acc + ext_ref[r, pl.ds(off - r, tm), :] * cw_ref[k:k + 1, :]
    mu = jnp.mean(acc, axis=-1, keepdims=True)
    cen = acc - mu
    var = jnp.mean(cen * cen, axis=-1, keepdims=True)
    y = cen * lax.rsqrt(var + EPS) * lg_ref[...] + lb_ref[...]
    o_ref[...] = _silu(y).astype(BF16)


def _conv_a(y, cw3, cb3, lg3, lb3, l, tm):
    b, n, c = y.shape
    prev, nxt = _halo_specs(tm, n, c)
    vec = _layer_spec((1, c), l)
    return pl.pallas_call(
        functools.partial(_conv_a_kernel, tm=tm),
        name="conv_a",
        grid=(b, n // tm),
        in_specs=[prev, pl.BlockSpec((None, tm, c), lambda bb, i: (bb, i, 0)), nxt,
                  _layer_spec((CONV_A, c), l), vec, vec, vec],
        out_specs=pl.BlockSpec((None, tm, c), lambda bb, i: (bb, i, 0)),
        out_shape=jax.ShapeDtypeStruct((b, n, c), BF16),
        scratch_shapes=[pltpu.VMEM((SUBLANES, tm + 2 * HALO, c), F32)],
        compiler_params=_cparams("parallel", "parallel"),
    )(y, y, y, cw3, cb3, lg3, lb3)


def _proj_b_kernel(h_ref, wq_ref, wk_ref, wv_ref, gq_ref, gk_ref, gm_ref, q_ref, k_ref, v_ref):
    h = h_ref[...]
    for w_ref, gain_ref, o_ref in ((wq_ref, gq_ref, q_ref), (wk_ref, gk_ref, k_ref), (wv_ref, None, v_ref)):
        acc = _dot(h, w_ref[...])
        if gain_ref is not None:
            ss = _dot((acc * acc).astype(BF16), gm_ref[...])
            acc = acc * lax.rsqrt(ss * (1.0 / DH_B) + EPS) * gain_ref[...]
        for hh in range(H_B):
            o_ref[hh] = acc[:, hh * DH_B:(hh + 1) * DH_B].astype(BF16)


def _proj_b(h, w_bf, gq3, gk3, gmat, l, tm):
    b, n, d = h.shape
    vec = _layer_spec((1, D_B), l)
    first = 2 * D_A // D_B
    head_out = pl.BlockSpec((None, H_B, tm, DH_B), lambda bb, i: (bb, 0, i, 0))
    shp = jax.ShapeDtypeStruct((b, H_B, n, DH_B), BF16)
    return pl.pallas_call(
        _proj_b_kernel,
        name="proj_b",
        grid=(b, n // tm),
        in_specs=[
            pl.BlockSpec((None, tm, d), lambda bb, i: (bb, i, 0)),
            _wcols_spec(d, D_B, l, first), _wcols_spec(d, D_B, l, first + 1), _wcols_spec(d, D_B, l, first + 2),
            vec, vec,
            pl.BlockSpec((D_B, D_B), lambda bb, i: (0, 0)),
        ],
        out_specs=[head_out, head_out, head_out],
        out_shape=[shp, shp, shp],
        compiler_params=_cparams("parallel", "parallel"),
    )(h, w_bf, w_bf, w_bf, gq3, gk3, gmat)


def _na_kernel(q_ref, k_ref, v_ref, kc_ref, vc_ref, bias_ref, o_ref, *, rb, rows):
    i = pl.program_id(2)
    n_loc = NA_ROWS * GRID_W
    rws = [i * rb + rr for rr in range(rb)]
    starts = [jnp.clip(r - NA_ROWS // 2, 0, rows - NA_ROWS) for r in rws]
    variants = [st - r + NA_ROWS - 1 for st, r in zip(starts, rws)]
    tok0 = [pl.multiple_of(st * GRID_W, GRID_W) for st in starts]
    probs = [(hh, rr) for hh in range(HEADS_PER_STEP) for rr in range(rb)]
    qs = [q_ref[hh, rr * GRID_W:(rr + 1) * GRID_W, :] for hh, rr in probs]
    s = [_dot_nt(q, k_ref[hh, pl.ds(tok0[rr], n_loc), :]) for q, (hh, rr) in zip(qs, probs)]
    sc = [_dot_nt(q, kc_ref[hh]) for q, (hh, rr) in zip(qs, probs)]
    bias = {(hh, rr): jnp.concatenate([bias_ref[hh, variants[rr] + kk] for kk in range(0, NA_ROWS, 2)], axis=1)
            for hh, rr in probs}
    s = [a + bias[pr] for a, pr in zip(s, probs)]
    m = [jnp.maximum(jnp.max(a, axis=-1, keepdims=True), jnp.max(b, axis=-1, keepdims=True)) for a, b in zip(s, sc)]
    p = [jnp.exp2(a - mm) for a, mm in zip(s, m)]
    pc = [jnp.exp2(b - mm) for b, mm in zip(sc, m)]
    l = [jnp.sum(a, axis=-1, keepdims=True) + jnp.sum(b, axis=-1, keepdims=True) for a, b in zip(p, pc)]
    o = [_dot(a.astype(BF16), v_ref[hh, pl.ds(tok0[rr], n_loc), :]) + _dot(b.astype(BF16), vc_ref[hh])
         for a, b, (hh, rr) in zip(p, pc, probs)]
    o = [a / b for a, b in zip(o, l)]
    for rr in range(rb):
        o_ref[rr * GRID_W:(rr + 1) * GRID_W, :] = jnp.concatenate(
            [o[probs.index((hh, rr))] for hh in range(HEADS_PER_STEP)], axis=1).astype(BF16)


def _na_attention(q, k, v, kc, vc, bias, l, rb):
    b, h, t, dh = q.shape
    n_ctx = kc.shape[2]
    rows = t // GRID_W
    hs = HEADS_PER_STEP
    full = pl.BlockSpec((None, hs, t, dh), lambda bb, hp, i: (bb, hp, 0, 0))
    cfull = pl.BlockSpec((None, hs, n_ctx, dh), lambda bb, hp, i: (bb, hp, 0, 0))
    return pl.pallas_call(
        functools.partial(_na_kernel, rb=rb, rows=rows),
        name="na_attn",
        grid=(b, h // hs, rows // rb),
        in_specs=[pl.BlockSpec((None, hs, rb * GRID_W, dh), lambda bb, hp, i: (bb, hp, i, 0)),
                  full, full, cfull, cfull,
                  pl.BlockSpec((None, hs, 2 * NA_ROWS - 2, GRID_W, 2 * GRID_W),
                               lambda bb, hp, i: (l, hp, 0, 0, 0))],
        out_specs=pl.BlockSpec((None, rb * GRID_W, hs * dh), lambda bb, hp, i: (bb, i, hp)),
        out_shape=jax.ShapeDtypeStruct((b, t, h * dh), BF16),
        compiler_params=_cparams("parallel", "parallel", "arbitrary"),
    )(q, k, v, kc, vc, bias)


def _ctx_attn_kernel(q_ref, k_ref, v_ref, o_ref):
    outs = []
    for hh in range(HEADS_PER_STEP):
        s = _dot_nt(q_ref[hh], k_ref[hh])
        m = jnp.max(s, axis=-1, keepdims=True)
        p = jnp.exp2(s - m)
        l = jnp.sum(p, axis=-1, keepdims=True)
        outs.append(_dot(p.astype(BF16), v_ref[hh]) / l)
    o_ref[...] = jnp.concatenate(outs, axis=1).astype(BF16)


def _ctx_attention(q, k, v):
    b, h, n, dh = q.shape
    hs = HEADS_PER_STEP
    full = pl.BlockSpec((None, hs, n, dh), lambda bb, hp: (bb, hp, 0, 0))
    return pl.pallas_call(
        _ctx_attn_kernel,
        name="ctx_attn",
        grid=(b, h // hs),
        in_specs=[full, full, full],
        out_specs=pl.BlockSpec((None, n, hs * dh), lambda bb, hp: (bb, 0, hp)),
        out_shape=jax.ShapeDtypeStruct((b, n, h * dh), BF16),
        compiler_params=_cparams("parallel", "parallel"),
    )(q, k, v)


def _na_bias_table(rpb):
    cidx = np.arange(GRID_W)
    dc = np.clip(cidx[None, :] - cidx[:, None] + NA_COLS - 1, 0, 2 * NA_COLS - 2)
    onehot = (dc[None] == np.arange(2 * NA_COLS - 1)[:, None, None]).astype(np.float32)
    cs = np.clip(cidx - NA_COLS // 2, 0, GRID_W - NA_COLS)
    col_ok = (cidx[None, :] >= cs[:, None]) & (cidx[None, :] < cs[:, None] + NA_COLS)
    toep = jnp.einsum("lhrd,dqk->lhrqk", rpb, onehot, precision=lax.Precision.HIGHEST)
    toep = jnp.where(col_ok, toep * LOG2E, NEG_INF)
    return jnp.concatenate([toep[:, :, :-1], toep[:, :, 1:]], axis=-1)


def _proj_c_kernel(hp_ref, h_ref, hn_ref, wq_ref, wk_ref, wv_ref, wo_ref, wdb_ref, cw_ref, rc_ref, rs_ref,
                   gm_ref, gbp_ref, q_ref, k_ref, v_ref, og_ref, gb_ref, p_ref, *, tm, use_rope):
    i = pl.program_id(1)
    last = pl.num_programs(1) - 1
    hp = jnp.where(i > 0, hp_ref[...], jnp.zeros_like(hp_ref))
    hn = jnp.where(i < last, hn_ref[...], jnp.zeros_like(hn_ref))
    hc = h_ref[...]
    pad_l = SHORT_CONV // 2
    n_ext = tm + 2 * HALO
    lane = lax.broadcasted_iota(jnp.int32, (tm, DK_C), 1)
    first_half = (lane & (DK_C // 2 - 1)) < DK_C // 4
    for sec, (w_ref, o_ref) in enumerate(((wq_ref, q_ref), (wk_ref, k_ref), (wv_ref, v_ref))):
        cols = slice(sec * D_CQK, (sec + 1) * D_CQK)
        p_ref[0:HALO] = _dot(hp, w_ref[...])
        p_ref[HALO:HALO + tm] = _dot(hc, w_ref[...])
        p_ref[HALO + tm:n_ext] = _dot(hn, w_ref[...])
        pv = p_ref[...]
        y = None
        for kk in range(SHORT_CONV):
            sh = (pad_l - kk) % n_ext
            pk = pv if sh == 0 else pltpu.roll(pv, sh, axis=0)
            t = pk[HALO:HALO + tm] * cw_ref[kk:kk + 1, cols]
            y = t if y is None else y + t
        y = _silu(y)
        if sec < 2:
            ss = _dot((y * y).astype(BF16), gm_ref[...])
            y = y * lax.rsqrt(ss + EPS)
            if use_rope:
                heads = []
                for hh in range(H_C):
                    yh = y[:, hh * DK_C:(hh + 1) * DK_C]
                    swapped = jnp.where(first_half, pltpu.roll(yh, DK_C - DK_C // 4, axis=1),
                                        pltpu.roll(yh, DK_C // 4, axis=1))
                    heads.append(yh * rc_ref[...] + swapped * rs_ref[...])
                y = jnp.concatenate(heads, axis=1)
            if sec == 0:
                y = y * (DK_C ** -0.5)
        o_ref[...] = y
    og_ref[...] = _dot(hc, wo_ref[...])
    db = _dot(hc, wdb_ref[...])
    gval = -jnp.exp(gbp_ref[0:1, :]) * _softplus(db + gbp_ref[1:2, :])
    gb_ref[...] = gbp_ref[2:3, :] * gval + gbp_ref[3:4, :] * _sigmoid(db)


def _proj_c(h, w_bf, cw3, rc, rs, gmat, gbp3, l, tm, use_rope):
    b, n, d = h.shape
    first = (2 * D_A + 3 * D_B) // D_CQK
    db_block = (2 * D_A + 3 * D_B + 3 * D_CQK + D_CV) // DK_C
    prev, nxt = _halo_specs(tm, n, d)
    tile512 = pl.BlockSpec((None, tm, D_CQK), lambda bb, i: (bb, i, 0))
    shp = jax.ShapeDtypeStruct((b, n, D_CQK), F32)
    rope_spec = pl.BlockSpec((tm, DK_C), lambda bb, i: (i, 0))
    return pl.pallas_call(
        functools.partial(_proj_c_kernel, tm=tm, use_rope=use_rope),
        name="proj_c",
        grid=(b, n // tm),
        in_specs=[
            prev, pl.BlockSpec((None, tm, d), lambda bb, i: (bb, i, 0)), nxt,
            _wcols_spec(d, D_CQK, l, first), _wcols_spec(d, D_CQK, l, first + 1),
            _wcols_spec(d, D_CQK, l, first + 2), _wcols_spec(d, D_CQK, l, first + 3),
            _wcols_spec(d, DK_C, l, db_block),
            _layer_spec((SHORT_CONV, 3 * D_CQK), l),
            rope_spec, rope_spec,
            pl.BlockSpec((D_CQK, D_CQK), lambda bb, i: (0, 0)),
            _layer_spec((8, DK_C), l),
        ],
        out_specs=[tile512] * 4 + [pl.BlockSpec((None, tm, DK_C), lambda bb, i: (bb, i, 0))],
        out_shape=[shp] * 4 + [jax.ShapeDtypeStruct((b, n, DK_C), F32)],
        scratch_shapes=[pltpu.VMEM((tm + 2 * HALO, D_CQK), F32)],
        compiler_params=_cparams("parallel", "parallel"),
    )(h, h, h, w_bf, w_bf, w_bf, w_bf, w_bf, cw3, rc, rs, gmat, gbp3)


def _chunk_masks(n):
    r = np.arange(n)[:, None]
    c = np.arange(n)[None, :]
    same = lambda s: (r // s) == (c // s)
    chunk = same(CHUNK)
    ms = [chunk & (r >= c), chunk & (r > c), chunk & (r <= c), chunk & (r < c), same(8),
          same(16) & ~same(8), same(32) & ~same(16), same(64) & ~same(32)]
    return np.stack(ms).astype(np.float32)


def _tri_inv_all(lmats, eye, m_ref):
    m8 = m_ref[4]
    n0 = [-(l * m8) for l in lmats]
    n2 = [_mm(a, a) for a in n0]
    n4 = [_mm(a, a) for a in n2]
    n3 = [_mm(a, b) for a, b in zip(n0, n2)]
    t1 = [eye + a + b + c for a, b, c in zip(n0, n2, n3)]
    t1n4 = [_mm(a, b) for a, b in zip(t1, n4)]
    t = [a + b for a, b in zip(t1, t1n4)]
    for lvl in (5, 6, 7):
        off = m_ref[lvl]
        lt = [_mm(l * off, a) for l, a in zip(lmats, t)]
        tlt = [_mm(a, b) for a, b in zip(t, lt)]
        t = [a - b for a, b in zip(t, tlt)]
    return t


def _gdn_step(io, m_ref, cum_ref):
    nsub = GDN_BLOCK // GDN_SUB
    csub = GDN_SUB // CHUNK
    nch = nsub * csub
    r16, cols = [], []
    for d in range(2):
        g_t = io[d][3][...].T[0:4 * H_C]
        csum = _dot_exact_rhs(g_t, cum_ref[d], 3)
        rid = lax.broadcasted_iota(jnp.int32, g_t.shape, 0)
        r = jnp.where(rid // H_C == d, csum, g_t)
        r16.append(r)
        cols.append(jnp.concatenate([r, jnp.zeros((DK_C - 4 * H_C, GDN_BLOCK), F32)], axis=0).T)
    probs = [(d, h, sb) for d in range(2) for h in range(H_C) for sb in range(nsub)]
    incl = [m_ref[0], m_ref[2]]
    strict = [m_ref[1], m_ref[3]]
    eye = incl[0] - strict[0]

    def tile(ref, h, sb):
        return ref[sb * GDN_SUB:(sb + 1) * GDN_SUB, h * DK_C:(h + 1) * DK_C]

    q = [tile(io[d][0], h, sb) for d, h, sb in probs]
    k = [tile(io[d][1], h, sb) for d, h, sb in probs]
    v = [tile(io[d][2], h, sb) for d, h, sb in probs]
    gcol, bcol, decay = [], [], []
    for d, h, sb in probs:
        rows = slice(sb * GDN_SUB, (sb + 1) * GDN_SUB)
        c_g = d * H_C + h
        c_b = 2 * H_C + c_g
        gc = jnp.broadcast_to(cols[d][rows, c_g:c_g + 1], (GDN_SUB, DK_C))
        gcol.append(gc)
        bcol.append(jnp.broadcast_to(cols[d][rows, c_b:c_b + 1], (GDN_SUB, DK_C)))
        decay.append(jnp.exp(jnp.where(incl[d] > 0.5, gc - r16[d][c_g:c_g + 1, rows], NEG_INF)))
    k16 = [a.astype(BF16) for a in k]
    qkk = [_dot_nt(jnp.concatenate([a.astype(BF16), b], axis=0), b) for a, b in zip(q, k16)]
    qk = [a[:GDN_SUB] for a in qkk]
    kk = [a[GDN_SUB:] for a in qkk]
    lmats = [a * b * (c * strict[p[0]]) for a, b, c, p in zip(kk, bcol, decay, probs)]
    amat = [(a * c).astype(BF16) for a, c in zip(qk, decay)]
    tinv = _tri_inv_all(lmats, eye, m_ref)
    eg = [jnp.exp(a) for a in gcol]
    rhs = [jnp.concatenate([vv * b, kx * (b * e)], axis=1).astype(BF16) for vv, kx, b, e in zip(v, k, bcol, eg)]
    sol = [_dot(t.astype(BF16), r) for t, r in zip(tinv, rhs)]
    qd = [a * e for a, e in zip(q, eg)]
    gtot, kd_t = [], []
    for (d, h, sb), kx, gc in zip(probs, k, gcol):
        parts, gts = [], []
        for c in range(csub):
            end = c * CHUNK + (CHUNK - 1 if d == 0 else 0)
            gt = gc[end:end + 1, :]
            gts.append(gt)
            sl = slice(c * CHUNK, (c + 1) * CHUNK)
            parts.append(kx[sl] * jnp.exp(gt - gc[sl]))
        gtot.append(gts)
        kd_t.append(jnp.concatenate(parts, axis=0).T.astype(BF16))
    pidx = {p: n for n, p in enumerate(probs)}
    chains = [(d, h) for d in range(2) for h in range(H_C)]
    state = [io[d][4][h] for d, h in chains]
    vnew = [[None] * nch for _ in chains]
    ointer = [[None] * nch for _ in chains]
    zeros = jnp.zeros((CHUNK, DV_C), BF16)
    for step in range(nch):
        pos = [(step if d == 0 else nch - 1 - step) for d, _ in chains]
        loc = [(pidx[(d, h, n // csub)], n % csub) for (d, h), n in zip(chains, pos)]
        sls = [slice(c * CHUNK, (c + 1) * CHUNK) for _, c in loc]
        wq = [jnp.concatenate([sol[p][sl, DV_C:], qd[p][sl]], axis=0).astype(BF16) for (p, _), sl in zip(loc, sls)]
        r = [_dot(a, s.astype(BF16)) for a, s in zip(wq, state)]
        vn = [(sol[p][sl, :DV_C] - rr[0:CHUNK]).astype(BF16) for (p, _), sl, rr in zip(loc, sls, r)]
        vpad = [jnp.concatenate([x if m == c else zeros for m in range(csub)], axis=0) for x, (_, c) in zip(vn, loc)]
        upd = [_dot(kd_t[p], x) for (p, _), x in zip(loc, vpad)]
        state = [s * jnp.exp(gtot[p][c]) + u for s, (p, c), u in zip(state, loc, upd)]
        for ci, n in enumerate(pos):
            vnew[ci][n] = vn[ci]
            ointer[ci][n] = r[ci][CHUNK:2 * CHUNK]
    for (d, h), s in zip(chains, state):
        io[d][4][h] = s
    out = [[None] * H_C for _ in range(2)]
    for ci, (d, h) in enumerate(chains):
        parts = []
        for sb in range(nsub):
            p = pidx[(d, h, sb)]
            vn_sb = jnp.concatenate(vnew[ci][sb * csub:(sb + 1) * csub], axis=0)
            parts.append(jnp.concatenate(ointer[ci][sb * csub:(sb + 1) * csub], axis=0) + _dot(amat[p], vn_sb))
        out[d][h] = jnp.concatenate(parts, axis=0)
    return out


def _gdn_finalize(o_heads, gate_ref, gain):
    ys = []
    for h, o in enumerate(o_heads):
        ms = jnp.mean(o * o, axis=-1, keepdims=True)
        ys.append(o * lax.rsqrt(ms + EPS) * gain * _silu(gate_ref[:, h * DV_C:(h + 1) * DV_C]))
    return jnp.concatenate(ys, axis=1).astype(BF16)


def _gdn_kernel(qf_ref, kf_ref, vf_ref, gf_ref, ogf_ref, qb_ref, kb_ref, vb_ref, gb_ref, ogb_ref,
                s0f_ref, s0b_ref, m_ref, cum_ref, gain_ref,
                y_ref, sfo_ref, sbo_ref, oacc_ref, sf_ref, sb_ref, *, nb):
    i = pl.program_id(1)

    @pl.when(i == 0)
    def _():
        sf_ref[...] = s0f_ref[...]
        sb_ref[...] = s0b_ref[...]

    of, ob = _gdn_step(((qf_ref, kf_ref, vf_ref, gf_ref, sf_ref), (qb_ref, kb_ref, vb_ref, gb_ref, sb_ref)),
                       m_ref, cum_ref)
    rows_f = pl.ds(pl.multiple_of(i * GDN_BLOCK, GDN_BLOCK), GDN_BLOCK)
    rows_b = pl.ds(pl.multiple_of((nb - 1 - i) * GDN_BLOCK, GDN_BLOCK), GDN_BLOCK)
    gain = gain_ref[...]

    def split(x):
        return [x[:, h * DV_C:(h + 1) * DV_C] for h in range(H_C)]

    @pl.when(2 * i < nb - 1)
    def _():
        oacc_ref[rows_f, :] = jnp.concatenate(of, axis=1)
        oacc_ref[rows_b, :] = jnp.concatenate(ob, axis=1)

    if nb % 2 == 1:
        @pl.when(2 * i == nb - 1)
        def _():
            y_ref[rows_f, :] = _gdn_finalize([a + b for a, b in zip(of, ob)], ogf_ref, gain)

    @pl.when(2 * i > nb - 1)
    def _():
        y_ref[rows_f, :] = _gdn_finalize([a + b for a, b in zip(split(oacc_ref[rows_f, :]), of)], ogf_ref, gain)
        y_ref[rows_b, :] = _gdn_finalize([a + b for a, b in zip(split(oacc_ref[rows_b, :]), ob)], ogb_ref, gain)

    @pl.when(i == nb - 1)
    def _():
        sfo_ref[...] = sf_ref[...]
        sbo_ref[...] = sb_ref[...]


def _gdn(q, k, v, gb, og, s0f, s0b, masks, cums, gain3, l):
    b, n, _ = q.shape
    nb = n // GDN_BLOCK
    fwd = pl.BlockSpec((None, GDN_BLOCK, D_CQK), lambda bb, i: (bb, i, 0))
    bwd = pl.BlockSpec((None, GDN_BLOCK, D_CQK), lambda bb, i: (bb, nb - 1 - i, 0))
    gfwd = pl.BlockSpec((None, GDN_BLOCK, DK_C), lambda bb, i: (bb, i, 0))
    gbwd = pl.BlockSpec((None, GDN_BLOCK, DK_C), lambda bb, i: (bb, nb - 1 - i, 0))
    st = pl.BlockSpec((None, H_C, DK_C, DV_C), lambda bb, i: (bb, 0, 0, 0))
    st_shape = jax.ShapeDtypeStruct((b, H_C, DK_C, DV_C), F32)
    return pl.pallas_call(
        functools.partial(_gdn_kernel, nb=nb),
        name="gdn_scan",
        grid=(b, nb),
        in_specs=[fwd, fwd, fwd, gfwd, fwd, bwd, bwd, bwd, gbwd, bwd,
                  st, st,
                  pl.BlockSpec((8, GDN_SUB, GDN_SUB), lambda bb, i: (0, 0, 0)),
                  pl.BlockSpec((2, GDN_BLOCK, GDN_BLOCK), lambda bb, i: (0, 0, 0)),
                  _layer_spec((1, DV_C), l)],
        out_specs=[pl.BlockSpec((None, n, D_CV), lambda bb, i: (bb, 0, 0)), st, st],
        out_shape=[jax.ShapeDtypeStruct((b, n, D_CV), BF16), st_shape, st_shape],
        scratch_shapes=[pltpu.VMEM((n, D_CV), F32), pltpu.VMEM((H_C, DK_C, DV_C), F32),
                        pltpu.VMEM((H_C, DK_C, DV_C), F32)],
        compiler_params=_cparams("parallel", "arbitrary"),
    )(q, k, v, gb, og, q, k, v, gb, og, s0f, s0b, masks, cums, gain3)


def _merge_kernel(h_ref, ya_ref, yb_ref, yc_ref, wga_ref, wgb_ref, wgc_ref, wbr_ref, z_ref):
    h = h_ref[...]
    gates = [_dot(h, w_ref[...]) for w_ref in (wga_ref, wgb_ref, wgc_ref)]
    vals = [_dot(y_ref[...], wbr_ref[br]) for br, y_ref in enumerate((ya_ref, yb_ref, yc_ref))]
    z = None
    for gt, vl in zip(gates, vals):
        t = _sigmoid(gt) * vl
        z = t if z is None else z + t
    z_ref[...] = z.astype(BF16)


def _merge(h, ya, yb, yc, wgates, wbr4, l, tm, tn):
    b, n, d = h.shape
    gate_spec = lambda br: pl.BlockSpec((None, d, tn), lambda bb, i, j: (l, 0, br * (d // tn) + j))
    yt = pl.BlockSpec((None, tm, D_A), lambda bb, i, j: (bb, i, 0))
    return pl.pallas_call(
        _merge_kernel,
        name="merge_gate",
        grid=(b, n // tm, d // tn),
        in_specs=[
            pl.BlockSpec((None, tm, d), lambda bb, i, j: (bb, i, 0)), yt, yt, yt,
            gate_spec(0), gate_spec(1), gate_spec(2),
            pl.BlockSpec((None, 3, D_A, tn), lambda bb, i, j: (l, 0, 0, j)),
        ],
        out_specs=pl.BlockSpec((None, tm, tn), lambda bb, i, j: (bb, i, j)),
        out_shape=jax.ShapeDtypeStruct((b, n, d), BF16),
        compiler_params=_cparams("parallel", "parallel", "parallel"),
    )(h, ya, yb, yc, wgates, wgates, wgates, wbr4)


def _resid_kernel(x_ref, a_ref, gt_ref, w_ref, *rest, with_norm):
    xn = x_ref[...] + gt_ref[...] * _dot(a_ref[...], w_ref[...])
    if with_norm:
        g_ref, sh_ref, sc_ref, o_ref, h_ref = rest
        h_ref[...] = _norm_mod(xn, g_ref[...], sh_ref[...], sc_ref[...]).astype(BF16)
    else:
        (o_ref,) = rest
    o_ref[...] = xn


def _resid_mm(x, a, mods, l, row, j_gate, w3, norm, tm):
    b, n, d = x.shape
    kdim = a.shape[-1]
    xt = pl.BlockSpec((None, tm, d), lambda bb, i: (bb, i, 0))
    in_specs = [xt, pl.BlockSpec((None, tm, kdim), lambda bb, i: (bb, i, 0)), _mod_spec(d, l, j_gate, row),
                _layer_spec((kdim, d), l)]
    args = [x, a, mods, w3]
    out_specs, out_shape = [xt], [jax.ShapeDtypeStruct((b, n, d), F32)]
    if norm is not None:
        g3, ln, j_shift = norm
        in_specs += [_layer_spec((1, d), ln), _mod_spec(d, ln, j_shift, row), _mod_spec(d, ln, j_shift + 1, row)]
        args += [g3, mods, mods]
        out_specs.append(xt)
        out_shape.append(jax.ShapeDtypeStruct((b, n, d), BF16))
    res = pl.pallas_call(
        functools.partial(_resid_kernel, with_norm=norm is not None),
        name="resid_mm",
        grid=(b, n // tm),
        in_specs=in_specs,
        out_specs=out_specs,
        out_shape=out_shape,
        compiler_params=_cparams("parallel", "parallel"),
    )(*args)
    return (res[0], res[1]) if norm is not None else (res[0], None)


def _ffn_up_kernel(hp_ref, h_ref, hn_ref, wup_ref, cw_ref, cb_ref, o_ref, ug_ref, uv_ref, *, tm, tf):
    i = pl.program_id(1)
    last = pl.num_programs(1) - 1
    hp = jnp.where(i > 0, hp_ref[...], jnp.zeros_like(hp_ref))
    hn = jnp.where(i < last, hn_ref[...], jnp.zeros_like(hn_ref))
    hc = h_ref[...]
    pad = FFN_CONV // 2
    n_ext = tm + 2 * HALO
    nf = D_FF // tf

    def project(j):
        slot = j % 2
        for u_ref, c0 in ((ug_ref, j * tf), (uv_ref, D_FF + j * tf)):
            w = wup_ref[:, c0:c0 + tf]
            u_ref[slot, 0:HALO] = _dot(hp, w)
            u_ref[slot, HALO:HALO + tm] = _dot(hc, w)
            u_ref[slot, HALO + tm:n_ext] = _dot(hn, w)

    def conv(j):
        slot = j % 2
        ug = ug_ref[slot]
        uv = uv_ref[slot]
        gc = slice(j * tf, (j + 1) * tf)
        vc = slice(D_FF + j * tf, D_FF + (j + 1) * tf)
        cg = cb_ref[:, gc]
        cv = cb_ref[:, vc]
        for kk in range(FFN_CONV):
            sh = (pad - kk) % n_ext
            ugk = ug if sh == 0 else pltpu.roll(ug, sh, axis=0)
            uvk = uv if sh == 0 else pltpu.roll(uv, sh, axis=0)
            cg = cg + ugk[HALO:HALO + tm] * cw_ref[kk:kk + 1, gc]
            cv = cv + uvk[HALO:HALO + tm] * cw_ref[kk:kk + 1, vc]
        o_ref[:, gc] = (_silu(cg) * cv).astype(BF16)

    project(0)
    for j in range(1, nf):
        project(j)
        conv(j - 1)
    conv(nf - 1)


def _ffn_up(h, wup3, cw3, cb3, l, tm, tf):
    b, n, d = h.shape
    prev, nxt = _halo_specs(tm, n, d)
    const = lambda shape: _layer_spec(shape, l, pipeline_mode=pl.Buffered(1))
    return pl.pallas_call(
        functools.partial(_ffn_up_kernel, tm=tm, tf=tf),
        name="ffn_up",
        grid=(b, n // tm),
        in_specs=[
            prev, pl.BlockSpec((None, tm, d), lambda bb, i: (bb, i, 0)), nxt,
            const((d, 2 * D_FF)), const((FFN_CONV, 2 * D_FF)), const((1, 2 * D_FF)),
        ],
        out_specs=pl.BlockSpec((None, tm, D_FF), lambda bb, i: (bb, i, 0)),
        out_shape=jax.ShapeDtypeStruct((b, n, D_FF), BF16),
        scratch_shapes=[pltpu.VMEM((2, tm + 2 * HALO, tf), F32), pltpu.VMEM((2, tm + 2 * HALO, tf), F32)],
        compiler_params=_cparams("parallel", "parallel"),
    )(h, h, h, wup3, cw3, cb3)


def _rope_tables(n_tok):
    t = jnp.arange(n_tok)
    row = (t // GRID_W).astype(F32)
    col = (t % GRID_W).astype(F32)
    n_freq = DK_C // 4
    inv = jnp.power(ROPE_BASE, -jnp.arange(n_freq, dtype=F32) / n_freq)
    ar = row[:, None] * inv
    ac = col[:, None] * inv
    cos = jnp.concatenate([jnp.cos(ar), jnp.cos(ar), jnp.cos(ac), jnp.cos(ac)], axis=-1)
    sin = jnp.concatenate([-jnp.sin(ar), jnp.sin(ar), -jnp.sin(ac), jnp.sin(ac)], axis=-1)
    return cos, sin


def _block_ones(n, blk):
    idx = np.arange(n) // blk
    return jnp.asarray((idx[:, None] == idx[None, :]).astype(np.float32), dtype=BF16)


def kernel(x, c, ctx, c_ctx, ada_w, ada_b, norm1_g, norm2_g, w_in, conv_a_w, conv_a_b, ln_a_g, ln_a_b, qn_g, kn_g,
           rpb, conv_c_w, a_log, dt_bias, onorm_g, w_branch, w_out, ffn_up, ffn_conv_w, ffn_conv_b, ffn_down):
    batch, n_lat, d = x.shape
    n_ctx = ctx.shape[1]
    depth = ada_w.shape[0]

    cc = jnp.zeros((8, d), F32).at[:batch].set(c).at[batch].set(c_ctx)
    mods = _ada(cc, ada_w, ada_b).reshape(depth, 8, 6, 1, d)
    ctx_row = batch

    rope_c, rope_s = _rope_tables(n_lat)
    ones_c = jnp.ones((n_ctx, DK_C), F32)
    zeros_c = jnp.zeros((n_ctx, DK_C), F32)
    gm64 = _block_ones(D_B, DH_B)
    gm128 = _block_ones(D_CQK, DK_C)
    masks = jnp.asarray(_chunk_masks(GDN_SUB))
    blk_masks = _chunk_masks(GDN_BLOCK)
    cums = jnp.asarray(np.stack([blk_masks[0].T, blk_masks[2].T]), dtype=BF16)
    s_zero = jnp.zeros((batch, H_C, DK_C, DV_C), F32)

    off_g = 2 * D_A + 3 * D_B + 3 * D_CQK + D_CV + 4 * H_C
    w_bf = w_in.astype(BF16)
    wgates = w_bf[:, :, off_g:]
    wbr4 = w_branch.astype(BF16)
    wo3 = w_out.astype(BF16)
    wup3 = ffn_up.astype(BF16)
    wdn3 = ffn_down.astype(BF16)
    g1_3 = norm1_g.reshape(depth, 1, d)
    g2_3 = norm2_g.reshape(depth, 1, d)
    gq3 = (jnp.tile(qn_g, (1, H_B)) * (DH_B ** -0.5 * LOG2E)).reshape(depth, 1, D_B)
    gk3 = jnp.tile(kn_g, (1, H_B)).reshape(depth, 1, D_B)
    gbp3 = jnp.zeros((depth, 8, DK_C), F32)
    gbp3 = gbp3.at[:, 0, :2 * H_C].set(a_log.reshape(depth, 2 * H_C)).at[:, 1, :2 * H_C].set(
        dt_bias.reshape(depth, 2 * H_C)).at[:, 2, :2 * H_C].set(1.0).at[:, 3, 2 * H_C:4 * H_C].set(1.0)
    bias5 = _na_bias_table(rpb)
    cb_a3 = conv_a_b.reshape(depth, 1, D_A)
    lg_a3 = ln_a_g.reshape(depth, 1, D_A)
    lb_a3 = ln_a_b.reshape(depth, 1, D_A)
    cbf3 = ffn_conv_b.reshape(depth, 1, 2 * D_FF)
    gain3 = onorm_g.reshape(depth, 1, DV_C)

    tl = _tiles(n_lat)
    tc = _tiles(n_ctx)
    x_lat, x_ctx = x, ctx
    h_lat = _normmod(x_lat, g1_3, mods, 0, None, 0, tl["norm"])
    h_ctx = _normmod(x_ctx, g1_3, mods, 0, ctx_row, 0, tc["norm"])
    for l in range(depth):
        ctx_out = l < depth - 1
        next_norm = (g1_3, l + 1, 0) if ctx_out else None

        qb_c, kb_c, vb_c = _proj_b(h_ctx, w_bf, gq3, gk3, gm64, l, tc["proj"])
        qc_c, kc_c, vc_c, og_c, gb_c = _proj_c(h_ctx, w_bf, conv_c_w, ones_c, zeros_c, gm128, gbp3, l,
                                               tc["proj"], False)
        yc_c, sf_c, sb_c = _gdn(qc_c, kc_c, vc_c, gb_c, og_c, s_zero, s_zero, masks, cums, gain3, l)

        ya_l = _conv_a(_proj_a(h_lat, w_bf, l, tl["proj"]), conv_a_w, cb_a3, lg_a3, lb_a3, l, tl["conv_a"])
        qb_l, kb_l, vb_l = _proj_b(h_lat, w_bf, gq3, gk3, gm64, l, tl["proj"])
        yb_l = _na_attention(qb_l, kb_l, vb_l, kb_c, vb_c, bias5, l, NA_ROWS_PER_STEP)
        qc_l, kc_l, vc_l, og_l, gb_l = _proj_c(h_lat, w_bf, conv_c_w, rope_c, rope_s, gm128, gbp3, l,
                                               tl["proj"], True)
        yc_l, _, _ = _gdn(qc_l, kc_l, vc_l, gb_l, og_l, sf_c, sb_c, masks, cums, gain3, l)
        z_l = _merge(h_lat, ya_l, yb_l, yc_l, wgates, wbr4, l, tl["merge"], MERGE_TN)
        x_lat, h2_l = _resid_mm(x_lat, z_l, mods, l, None, 2, wo3, (g2_3, l, 3), tl["resid"])
        act_l = _ffn_up(h2_l, wup3, ffn_conv_w, cbf3, l, tl["ffn"], FFN_TF)
        x_lat, h_lat = _resid_mm(x_lat, act_l, mods, l, None, 5, wdn3, next_norm, tl["resid"])

        if ctx_out:
            ya_c = _conv_a(_proj_a(h_ctx, w_bf, l, tc["proj"]), conv_a_w, cb_a3, lg_a3, lb_a3, l, tc["conv_a"])
            yb_c = _ctx_attention(qb_c, kb_c, vb_c)
            z_c = _merge(h_ctx, ya_c, yb_c, yc_c, wgates, wbr4, l, tc["merge"], MERGE_TN)
            x_ctx, h2_c = _resid_mm(x_ctx, z_c, mods, l, ctx_row, 2, wo3, (g2_3, l, 3), tc["resid"])
            act_c = _ffn_up(h2_c, wup3, ffn_conv_w, cbf3, l, tc["ffn"], FFN_TF)
            x_ctx, h_ctx = _resid_mm(x_ctx, act_c, mods, l, ctx_row, 5, wdn3, next_norm, tc["resid"])
    return x_lat
```

```python
import functools
import math

import numpy as np
import jax
import jax.numpy as jnp
from jax import lax
from jax.experimental import pallas as pl
from jax.experimental.pallas import tpu as pltpu

F32 = jnp.float32
BF16 = jnp.bfloat16

D_MODEL = 1024
GRID_W = 64
EPS = 1e-6
NEG_INF = -1e30
LOG2E = math.log2(math.e)
D_A = 512
CONV_A = 31
H_B = 8
DH_B = 64
D_B = H_B * DH_B
NA_ROWS = 8
NA_COLS = 16
H_C = 4
DK_C = 128
DV_C = 128
D_CQK = H_C * DK_C
D_CV = H_C * DV_C
SHORT_CONV = 4
CHUNK = 64
ROPE_BASE = 10000.0
D_FF = 2816
FFN_CONV = 3

VMEM_LIMIT_BYTES = 48 * 1024 * 1024
HALO = 16
SUBLANES = 8
GDN_BLOCK = 256
GDN_SUB = 128
NA_ROWS_PER_STEP = 16
HEADS_PER_STEP = 2
MERGE_TN = 1024
FFN_TF = 256


def _tiles(n_tok):
    cap = lambda t: min(t, n_tok)
    return {"norm": cap(1024), "proj": cap(1024), "conv_a": cap(512), "merge": cap(1024), "resid": cap(1024),
            "ffn": cap(1024)}


def _cparams(*sem):
    return pltpu.CompilerParams(dimension_semantics=sem, vmem_limit_bytes=VMEM_LIMIT_BYTES)


def _dot(a, b):
    return jnp.dot(a, b, preferred_element_type=F32)


def _dot_nt(a, b):
    return lax.dot_general(a, b, (((1,), (1,)), ((), ())), preferred_element_type=F32)


def _split_bf16(x, n):
    parts = []
    r = x
    for idx in range(n):
        p = r.astype(BF16)
        parts.append(p)
        if idx + 1 < n:
            r = r - p.astype(F32)
    return parts


def _dot_exact_rhs(a, b_bf16, n):
    out = None
    for p in _split_bf16(a, n):
        t = _dot(p, b_bf16)
        out = t if out is None else out + t
    return out


def _dot_exact_lhs(a_bf16, b, n):
    out = None
    for p in _split_bf16(b, n):
        t = _dot(a_bf16, p)
        out = t if out is None else out + t
    return out


def _mm(a, b, passes=1):
    if passes == 1:
        return _dot(a.astype(BF16), b.astype(BF16))
    a_hi, a_lo = _split_bf16(a, 2)
    b_hi, b_lo = _split_bf16(b, 2)
    return _dot(a_hi, b_hi) + (_dot(a_lo, b_hi) + _dot(a_hi, b_lo))


def _sigmoid(x):
    return 1.0 / (1.0 + jnp.exp(-x))


def _silu(x):
    return x * _sigmoid(x)


def _softplus(x):
    return jnp.maximum(x, 0.0) + jnp.log(1.0 + jnp.exp(-jnp.abs(x)))


def _norm_mod(x, g, shift, scale):
    ms = jnp.mean(x * x, axis=-1, keepdims=True)
    y = x * lax.rsqrt(ms + EPS) * g
    return y * (1.0 + scale) + shift


def _layer_spec(shape, l, **kw):
    return pl.BlockSpec((None,) + tuple(shape), lambda *_: (l,) + (0,) * len(shape), **kw)


def _wcols_spec(rows, width, l, col_block):
    return pl.BlockSpec((None, rows, width), lambda *_: (l, 0, col_block))


def _mod_spec(d, l, j, row):
    if row is None:
        return pl.BlockSpec((None, None, None, 1, d), lambda bb, *_: (l, bb, j, 0, 0))
    return pl.BlockSpec((None, None, None, 1, d), lambda *_: (l, row, j, 0, 0))


def _halo_specs(tm, n_tok, width):
    per = tm // HALO
    nblk = n_tok // HALO
    prev = pl.BlockSpec((None, HALO, width), lambda b, i, *_: (b, jnp.maximum(i * per - 1, 0), 0))
    nxt = pl.BlockSpec((None, HALO, width), lambda b, i, *_: (b, jnp.minimum((i + 1) * per, nblk - 1), 0))
    return prev, nxt


def _ada_kernel(c_ref, w_ref, b_ref, o_ref):
    s = _silu(c_ref[...])
    o_ref[...] = _mm(s, w_ref[...], 3) + b_ref[...]


def _ada(cc, ada_w, ada_b):
    depth = ada_w.shape[0]
    d = D_MODEL
    return pl.pallas_call(
        _ada_kernel,
        name="ada_mod",
        grid=(depth, 6),
        in_specs=[
            pl.BlockSpec((8, d), lambda l, j: (0, 0)),
            pl.BlockSpec((None, d, d), lambda l, j: (l, 0, j)),
            pl.BlockSpec((None, 1, d), lambda l, j: (l, 0, j)),
        ],
        out_specs=pl.BlockSpec((None, 8, d), lambda l, j: (l, 0, j)),
        out_shape=jax.ShapeDtypeStruct((depth, 8, 6 * d), F32),
        compiler_params=_cparams("arbitrary", "arbitrary"),
    )(cc, ada_w, ada_b.reshape(depth, 1, 6 * d))


def _normmod_kernel(x_ref, g_ref, sh_ref, sc_ref, o_ref):
    o_ref[...] = _norm_mod(x_ref[...], g_ref[...], sh_ref[...], sc_ref[...]).astype(BF16)


def _normmod(x, g3, mods, l, row, j_shift, tm):
    b, n, d = x.shape
    tile = pl.BlockSpec((None, tm, d), lambda bb, i: (bb, i, 0))
    return pl.pallas_call(
        _normmod_kernel,
        name="norm_mod",
        grid=(b, n // tm),
        in_specs=[tile, _layer_spec((1, d), l), _mod_spec(d, l, j_shift, row), _mod_spec(d, l, j_shift + 1, row)],
        out_specs=tile,
        out_shape=jax.ShapeDtypeStruct((b, n, d), BF16),
        compiler_params=_cparams("parallel", "parallel"),
    )(x, g3, mods, mods)


def _proj_a_kernel(h_ref, wa_ref, wg_ref, o_ref):
    h = h_ref[...]
    a = _dot(h, wa_ref[...])
    gate = _dot(h, wg_ref[...])
    o_ref[...] = a * _sigmoid(gate)


def _proj_a(h, w_bf, l, tm):
    b, n, d = h.shape
    return pl.pallas_call(
        _proj_a_kernel,
        name="proj_a",
        grid=(b, n // tm),
        in_specs=[
            pl.BlockSpec((None, tm, d), lambda bb, i: (bb, i, 0)),
            _wcols_spec(d, D_A, l, 0), _wcols_spec(d, D_A, l, 1),
        ],
        out_specs=pl.BlockSpec((None, tm, D_A), lambda bb, i: (bb, i, 0)),
        out_shape=jax.ShapeDtypeStruct((b, n, D_A), F32),
        compiler_params=_cparams("parallel", "parallel"),
    )(h, w_bf, w_bf)


def _conv_a_kernel(yp_ref, y_ref, yn_ref, cw_ref, cb_ref, lg_ref, lb_ref, o_ref, ext_ref, *, tm):
    i = pl.program_id(1)
    last = pl.num_programs(1) - 1
    ext_ref[0, 0:HALO] = yp_ref[...] * (i > 0).astype(F32)
    ext_ref[0, HALO:HALO + tm] = y_ref[...]
    ext_ref[0, HALO + tm:2 * HALO + tm] = yn_ref[...] * (i < last).astype(F32)
    n_keep = tm + 2 * HALO - SUBLANES
    for r in range(1, SUBLANES):
        ext_ref[r, 0:n_keep] = ext_ref[0, pl.ds(r, n_keep), :]
    pad = CONV_A // 2
    acc = jnp.zeros((tm, D_A), F32) + cb_ref[...]
    for k in range(CONV_A):
        off = HALO - pad + k
        r = off % SUBLANES
        acc = acc + ext_ref[r, pl.ds(off - r, tm), :] * cw_ref[k:k + 1, :]
    mu = jnp.mean(acc, axis=-1, keepdims=True)
    cen = acc - mu
    var = jnp.mean(cen * cen, axis=-1, keepdims=True)
    y = cen * lax.rsqrt(var + EPS) * lg_ref[...] + lb_ref[...]
    o_ref[...] = _silu(y).astype(BF16)


def _conv_a(y, cw3, cb3, lg3, lb3, l, tm):
    b, n, c = y.shape
    prev, nxt = _halo_specs(tm, n, c)
    vec = _layer_spec((1, c), l)
    return pl.pallas_call(
        functools.partial(_conv_a_kernel, tm=tm),
        name="conv_a",
        grid=(b, n // tm),
        in_specs=[prev, pl.BlockSpec((None, tm, c), lambda bb, i: (bb, i, 0)), nxt,
                  _layer_spec((CONV_A, c), l), vec, vec, vec],
        out_specs=pl.BlockSpec((None, tm, c), lambda bb, i: (bb, i, 0)),
        out_shape=jax.ShapeDtypeStruct((b, n, c), BF16),
        scratch_shapes=[pltpu.VMEM((SUBLANES, tm + 2 * HALO, c), F32)],
        compiler_params=_cparams("parallel", "parallel"),
    )(y, y, y, cw3, cb3, lg3, lb3)


def _proj_b_kernel(h_ref, wq_ref, wk_ref, wv_ref, gq_ref, gk_ref, gm_ref, q_ref, k_ref, v_ref):
    h = h_ref[...]
    for w_ref, gain_ref, o_ref in ((wq_ref, gq_ref, q_ref), (wk_ref, gk_ref, k_ref), (wv_ref, None, v_ref)):
        acc = _dot(h, w_ref[...])
        if gain_ref is not None:
            ss = _dot((acc * acc).astype(BF16), gm_ref[...])
            acc = acc * lax.rsqrt(ss * (1.0 / DH_B) + EPS) * gain_ref[...]
        for hh in range(H_B):
            o_ref[hh] = acc[:, hh * DH_B:(hh + 1) * DH_B].astype(BF16)


def _proj_b(h, w_bf, gq3, gk3, gmat, l, tm):
    b, n, d = h.shape
    vec = _layer_spec((1, D_B), l)
    first = 2 * D_A // D_B
    head_out = pl.BlockSpec((None, H_B, tm, DH_B), lambda bb, i: (bb, 0, i, 0))
    shp = jax.ShapeDtypeStruct((b, H_B, n, DH_B), BF16)
    return pl.pallas_call(
        _proj_b_kernel,
        name="proj_b",
        grid=(b, n // tm),
        in_specs=[
            pl.BlockSpec((None, tm, d), lambda bb, i: (bb, i, 0)),
            _wcols_spec(d, D_B, l, first), _wcols_spec(d, D_B, l, first + 1), _wcols_spec(d, D_B, l, first + 2),
            vec, vec,
            pl.BlockSpec((D_B, D_B), lambda bb, i: (0, 0)),
        ],
        out_specs=[head_out, head_out, head_out],
        out_shape=[shp, shp, shp],
        compiler_params=_cparams("parallel", "parallel"),
    )(h, w_bf, w_bf, w_bf, gq3, gk3, gmat)


def _na_kernel(q_ref, k_ref, v_ref, kc_ref, vc_ref, bias_ref, o_ref, *, rb, rows):
    i = pl.program_id(2)
    n_loc = NA_ROWS * GRID_W
    rws = [i * rb + rr for rr in range(rb)]
    starts = [jnp.clip(r - NA_ROWS // 2, 0, rows - NA_ROWS) for r in rws]
    variants = [st - r + NA_ROWS - 1 for st, r in zip(starts, rws)]
    tok0 = [pl.multiple_of(st * GRID_W, GRID_W) for st in starts]
    probs = [(hh, rr) for hh in range(HEADS_PER_STEP) for rr in range(rb)]
    qs = [q_ref[hh, rr * GRID_W:(rr + 1) * GRID_W, :] for hh, rr in probs]
    s = [_dot_nt(q, k_ref[hh, pl.ds(tok0[rr], n_loc), :]) for q, (hh, rr) in zip(qs, probs)]
    sc = [_dot_nt(q, kc_ref[hh]) for q, (hh, rr) in zip(qs, probs)]
    bias = {(hh, rr): jnp.concatenate([bias_ref[hh, variants[rr] + kk] for kk in range(0, NA_ROWS, 2)], axis=1)
            for hh, rr in probs}
    s = [a + bias[pr] for a, pr in zip(s, probs)]
    m = [jnp.maximum(jnp.max(a, axis=-1, keepdims=True), jnp.max(b, axis=-1, keepdims=True)) for a, b in zip(s, sc)]
    p = [jnp.exp2(a - mm) for a, mm in zip(s, m)]
    pc = [jnp.exp2(b - mm) for b, mm in zip(sc, m)]
    l = [jnp.sum(a, axis=-1, keepdims=True) + jnp.sum(b, axis=-1, keepdims=True) for a, b in zip(p, pc)]
    o = [_dot(a.astype(BF16), v_ref[hh, pl.ds(tok0[rr], n_loc), :]) + _dot(b.astype(BF16), vc_ref[hh])
         for a, b, (hh, rr) in zip(p, pc, probs)]
    o = [a / b for a, b in zip(o, l)]
    for rr in range(rb):
        o_ref[rr * GRID_W:(rr + 1) * GRID_W, :] = jnp.concatenate(
            [o[probs.index((hh, rr))] for hh in range(HEADS_PER_STEP)], axis=1).astype(BF16)


def _na_attention(q, k, v, kc, vc, bias, l, rb):
    b, h, t, dh = q.shape
    n_ctx = kc.shape[2]
    rows = t // GRID_W
    hs = HEADS_PER_STEP
    full = pl.BlockSpec((None, hs, t, dh), lambda bb, hp, i: (bb, hp, 0, 0))
    cfull = pl.BlockSpec((None, hs, n_ctx, dh), lambda bb, hp, i: (bb, hp, 0, 0))
    return pl.pallas_call(
        functools.partial(_na_kernel, rb=rb, rows=rows),
        name="na_attn",
        grid=(b, h // hs, rows // rb),
        in_specs=[pl.BlockSpec((None, hs, rb * GRID_W, dh), lambda bb, hp, i: (bb, hp, i, 0)),
                  full, full, cfull, cfull,
                  pl.BlockSpec((None, hs, 2 * NA_ROWS - 2, GRID_W, 2 * GRID_W),
                               lambda bb, hp, i: (l, hp, 0, 0, 0))],
        out_specs=pl.BlockSpec((None, rb * GRID_W, hs * dh), lambda bb, hp, i: (bb, i, hp)),
        out_shape=jax.ShapeDtypeStruct((b, t, h * dh), BF16),
        compiler_params=_cparams("parallel", "parallel", "arbitrary"),
    )(q, k, v, kc, vc, bias)


def _ctx_attn_kernel(q_ref, k_ref, v_ref, o_ref):
    outs = []
    for hh in range(HEADS_PER_STEP):
        s = _dot_nt(q_ref[hh], k_ref[hh])
        m = jnp.max(s, axis=-1, keepdims=True)
        p = jnp.exp2(s - m)
        l = jnp.sum(p, axis=-1, keepdims=True)
        outs.append(_dot(p.astype(BF16), v_ref[hh]) / l)
    o_ref[...] = jnp.concatenate(outs, axis=1).astype(BF16)


def _ctx_attention(q, k, v):
    b, h, n, dh = q.shape
    hs = HEADS_PER_STEP
    full = pl.BlockSpec((None, hs, n, dh), lambda bb, hp: (bb, hp, 0, 0))
    return pl.pallas_call(
        _ctx_attn_kernel,
        name="ctx_attn",
        grid=(b, h // hs),
        in_specs=[full, full, full],
        out_specs=pl.BlockSpec((None, n, hs * dh), lambda bb, hp: (bb, 0, hp)),
        out_shape=jax.ShapeDtypeStruct((b, n, h * dh), BF16),
        compiler_params=_cparams("parallel", "parallel"),
    )(q, k, v)


def _na_bias_table(rpb):
    cidx = np.arange(GRID_W)
    dc = np.clip(cidx[None, :] - cidx[:, None] + NA_COLS - 1, 0, 2 * NA_COLS - 2)
    onehot = (dc[None] == np.arange(2 * NA_COLS - 1)[:, None, None]).astype(np.float32)
    cs = np.clip(cidx - NA_COLS // 2, 0, GRID_W - NA_COLS)
    col_ok = (cidx[None, :] >= cs[:, None]) & (cidx[None, :] < cs[:, None] + NA_COLS)
    toep = jnp.einsum("lhrd,dqk->lhrqk", rpb, onehot, precision=lax.Precision.HIGHEST)
    toep = jnp.where(col_ok, toep * LOG2E, NEG_INF)
    return jnp.concatenate([toep[:, :, :-1], toep[:, :, 1:]], axis=-1)


def _proj_c_kernel(hp_ref, h_ref, hn_ref, wq_ref, wk_ref, wv_ref, wo_ref, wdb_ref, cw_ref, rc_ref, rs_ref,
                   gm_ref, gbp_ref, q_ref, k_ref, v_ref, og_ref, gb_ref, p_ref, *, tm, use_rope):
    i = pl.program_id(1)
    last = pl.num_programs(1) - 1
    hp = jnp.where(i > 0, hp_ref[...], jnp.zeros_like(hp_ref))
    hn = jnp.where(i < last, hn_ref[...], jnp.zeros_like(hn_ref))
    hc = h_ref[...]
    pad_l = SHORT_CONV // 2
    n_ext = tm + 2 * HALO
    lane = lax.broadcasted_iota(jnp.int32, (tm, DK_C), 1)
    first_half = (lane & (DK_C // 2 - 1)) < DK_C // 4
    sections = ((wq_ref, q_ref), (wk_ref, k_ref), (wv_ref, v_ref))

    def project(sec):
        w_ref = sections[sec][0]
        p_ref[sec % 2, 0:HALO] = _dot(hp, w_ref[...])
        p_ref[sec % 2, HALO:HALO + tm] = _dot(hc, w_ref[...])
        p_ref[sec % 2, HALO + tm:n_ext] = _dot(hn, w_ref[...])

    def finish(sec):
        o_ref = sections[sec][1]
        cols = slice(sec * D_CQK, (sec + 1) * D_CQK)
        pv = p_ref[sec % 2]
        y = None
        for kk in range(SHORT_CONV):
            sh = (pad_l - kk) % n_ext
            pk = pv if sh == 0 else pltpu.roll(pv, sh, axis=0)
            t = pk[HALO:HALO + tm] * cw_ref[kk:kk + 1, cols]
            y = t if y is None else y + t
        y = _silu(y)
        if sec < 2:
            ss = _dot((y * y).astype(BF16), gm_ref[...])
            y = y * lax.rsqrt(ss + EPS)
            if use_rope:
                heads = []
                for hh in range(H_C):
                    yh = y[:, hh * DK_C:(hh + 1) * DK_C]
                    swapped = jnp.where(first_half, pltpu.roll(yh, DK_C - DK_C // 4, axis=1),
                                        pltpu.roll(yh, DK_C // 4, axis=1))
                    heads.append(yh * rc_ref[...] + swapped * rs_ref[...])
                y = jnp.concatenate(heads, axis=1)
            if sec == 0:
                y = y * (DK_C ** -0.5)
        o_ref[...] = y

    project(0)
    for sec in range(1, len(sections)):
        project(sec)
        finish(sec - 1)
    og_ref[...] = _dot(hc, wo_ref[...])
    finish(len(sections) - 1)
    db = _dot(hc, wdb_ref[...])
    gval = -jnp.exp(gbp_ref[0:1, :]) * _softplus(db + gbp_ref[1:2, :])
    gb_ref[...] = gbp_ref[2:3, :] * gval + gbp_ref[3:4, :] * _sigmoid(db)


def _proj_c(h, w_bf, cw3, rc, rs, gmat, gbp3, l, tm, use_rope):
    b, n, d = h.shape
    first = (2 * D_A + 3 * D_B) // D_CQK
    db_block = (2 * D_A + 3 * D_B + 3 * D_CQK + D_CV) // DK_C
    prev, nxt = _halo_specs(tm, n, d)
    tile512 = pl.BlockSpec((None, tm, D_CQK), lambda bb, i: (bb, i, 0))
    shp = jax.ShapeDtypeStruct((b, n, D_CQK), F32)
    rope_spec = pl.BlockSpec((tm, DK_C), lambda bb, i: (i, 0))
    return pl.pallas_call(
        functools.partial(_proj_c_kernel, tm=tm, use_rope=use_rope),
        name="proj_c",
        grid=(b, n // tm),
        in_specs=[
            prev, pl.BlockSpec((None, tm, d), lambda bb, i: (bb, i, 0)), nxt,
            _wcols_spec(d, D_CQK, l, first), _wcols_spec(d, D_CQK, l, first + 1),
            _wcols_spec(d, D_CQK, l, first + 2), _wcols_spec(d, D_CQK, l, first + 3),
            _wcols_spec(d, DK_C, l, db_block),
            _layer_spec((SHORT_CONV, 3 * D_CQK), l),
            rope_spec, rope_spec,
            pl.BlockSpec((D_CQK, D_CQK), lambda bb, i: (0, 0)),
            _layer_spec((8, DK_C), l),
        ],
        out_specs=[tile512] * 4 + [pl.BlockSpec((None, tm, DK_C), lambda bb, i: (bb, i, 0))],
        out_shape=[shp] * 4 + [jax.ShapeDtypeStruct((b, n, DK_C), F32)],
        scratch_shapes=[pltpu.VMEM((2, tm + 2 * HALO, D_CQK), F32)],
        compiler_params=_cparams("parallel", "parallel"),
    )(h, h, h, w_bf, w_bf, w_bf, w_bf, w_bf, cw3, rc, rs, gmat, gbp3)


def _chunk_masks(n):
    r = np.arange(n)[:, None]
    c = np.arange(n)[None, :]
    same = lambda s: (r // s) == (c // s)
    chunk = same(CHUNK)
    ms = [chunk & (r >= c), chunk & (r > c), chunk & (r <= c), chunk & (r < c), same(8),
          same(16) & ~same(8), same(32) & ~same(16), same(64) & ~same(32)]
    return np.stack(ms).astype(np.float32)


def _tri_inv_all(lmats, eye, m_ref):
    m8 = m_ref[4]
    n0 = [-(l * m8) for l in lmats]
    n2 = [_mm(a, a) for a in n0]
    n4 = [_mm(a, a) for a in n2]
    n3 = [_mm(a, b) for a, b in zip(n0, n2)]
    t1 = [eye + a + b + c for a, b, c in zip(n0, n2, n3)]
    t1n4 = [_mm(a, b) for a, b in zip(t1, n4)]
    t = [a + b for a, b in zip(t1, t1n4)]
    for lvl in (5, 6, 7):
        off = m_ref[lvl]
        lt = [_mm(l * off, a) for l, a in zip(lmats, t)]
        tlt = [_mm(a, b) for a, b in zip(t, lt)]
        t = [a - b for a, b in zip(t, tlt)]
    return t


def _gdn_step(io, m_ref, cum_ref):
    nsub = GDN_BLOCK // GDN_SUB
    csub = GDN_SUB // CHUNK
    nch = nsub * csub
    r16, cols = [], []
    for d in range(2):
        g_t = io[d][3][...].T[0:4 * H_C]
        csum = _dot_exact_rhs(g_t, cum_ref[d], 3)
        rid = lax.broadcasted_iota(jnp.int32, g_t.shape, 0)
        r = jnp.where(rid // H_C == d, csum, g_t)
        r16.append(r)
        cols.append(jnp.concatenate([r, jnp.zeros((DK_C - 4 * H_C, GDN_BLOCK), F32)], axis=0).T)
    probs = [(d, h, sb) for d in range(2) for h in range(H_C) for sb in range(nsub)]
    incl = [m_ref[0], m_ref[2]]
    strict = [m_ref[1], m_ref[3]]
    eye = incl[0] - strict[0]

    def tile(ref, h, sb):
        return ref[sb * GDN_SUB:(sb + 1) * GDN_SUB, h * DK_C:(h + 1) * DK_C]

    q = [tile(io[d][0], h, sb) for d, h, sb in probs]
    k = [tile(io[d][1], h, sb) for d, h, sb in probs]
    v = [tile(io[d][2], h, sb) for d, h, sb in probs]
    gcol, bcol, decay = [], [], []
    for d, h, sb in probs:
        rows = slice(sb * GDN_SUB, (sb + 1) * GDN_SUB)
        c_g = d * H_C + h
        c_b = 2 * H_C + c_g
        gc = jnp.broadcast_to(cols[d][rows, c_g:c_g + 1], (GDN_SUB, DK_C))
        gcol.append(gc)
        bcol.append(jnp.broadcast_to(cols[d][rows, c_b:c_b + 1], (GDN_SUB, DK_C)))
        decay.append(jnp.exp(jnp.where(incl[d] > 0.5, gc - r16[d][c_g:c_g + 1, rows], NEG_INF)))
    k16 = [a.astype(BF16) for a in k]
    qkk = [_dot_nt(jnp.concatenate([a.astype(BF16), b], axis=0), b) for a, b in zip(q, k16)]
    qk = [a[:GDN_SUB] for a in qkk]
    kk = [a[GDN_SUB:] for a in qkk]
    lmats = [a * b * (c * strict[p[0]]) for a, b, c, p in zip(kk, bcol, decay, probs)]
    amat = [(a * c).astype(BF16) for a, c in zip(qk, decay)]
    tinv = _tri_inv_all(lmats, eye, m_ref)
    eg = [jnp.exp(a) for a in gcol]
    rhs = [jnp.concatenate([vv * b, kx * (b * e)], axis=1).astype(BF16) for vv, kx, b, e in zip(v, k, bcol, eg)]
    sol = [_dot(t.astype(BF16), r) for t, r in zip(tinv, rhs)]
    qd = [a * e for a, e in zip(q, eg)]
    gtot, kd_t = [], []
    for (d, h, sb), kx, gc in zip(probs, k, gcol):
        parts, gts = [], []
        for c in range(csub):
            end = c * CHUNK + (CHUNK - 1 if d == 0 else 0)
            gt = gc[end:end + 1, :]
            gts.append(gt)
            sl = slice(c * CHUNK, (c + 1) * CHUNK)
            parts.append(kx[sl] * jnp.exp(gt - gc[sl]))
        gtot.append(gts)
        kd_t.append(jnp.concatenate(parts, axis=0).T.astype(BF16))
    pidx = {p: n for n, p in enumerate(probs)}
    chains = [(d, h) for d in range(2) for h in range(H_C)]
    state = [io[d][4][h] for d, h in chains]
    vnew = [[None] * nch for _ in chains]
    ointer = [[None] * nch for _ in chains]
    zeros = jnp.zeros((CHUNK, DV_C), BF16)
    for step in range(nch):
        pos = [(step if d == 0 else nch - 1 - step) for d, _ in chains]
        loc = [(pidx[(d, h, n // csub)], n % csub) for (d, h), n in zip(chains, pos)]
        sls = [slice(c * CHUNK, (c + 1) * CHUNK) for _, c in loc]
        wq = [jnp.concatenate([sol[p][sl, DV_C:], qd[p][sl]], axis=0).astype(BF16) for (p, _), sl in zip(loc, sls)]
        r = [_dot(a, s.astype(BF16)) for a, s in zip(wq, state)]
        vn = [(sol[p][sl, :DV_C] - rr[0:CHUNK]).astype(BF16) for (p, _), sl, rr in zip(loc, sls, r)]
        vpad = [jnp.concatenate([x if m == c else zeros for m in range(csub)], axis=0) for x, (_, c) in zip(vn, loc)]
        upd = [_dot(kd_t[p], x) for (p, _), x in zip(loc, vpad)]
        state = [s * jnp.exp(gtot[p][c]) + u for s, (p, c), u in zip(state, loc, upd)]
        for ci, n in enumerate(pos):
            vnew[ci][n] = vn[ci]
            ointer[ci][n] = r[ci][CHUNK:2 * CHUNK]
    for (d, h), s in zip(chains, state):
        io[d][4][h] = s
    out = [[None] * H_C for _ in range(2)]
    for ci, (d, h) in enumerate(chains):
        parts = []
        for sb in range(nsub):
            p = pidx[(d, h, sb)]
            vn_sb = jnp.concatenate(vnew[ci][sb * csub:(sb + 1) * csub], axis=0)
            parts.append(jnp.concatenate(ointer[ci][sb * csub:(sb + 1) * csub], axis=0) + _dot(amat[p], vn_sb))
        out[d][h] = jnp.concatenate(parts, axis=0)
    return out


def _gdn_finalize(o_heads, gate_ref, gain):
    ys = []
    for h, o in enumerate(o_heads):
        ms = jnp.mean(o * o, axis=-1, keepdims=True)
        ys.append(o * lax.rsqrt(ms + EPS) * gain * _silu(gate_ref[:, h * DV_C:(h + 1) * DV_C]))
    return jnp.concatenate(ys, axis=1).astype(BF16)


def _gdn_kernel(qf_ref, kf_ref, vf_ref, gf_ref, ogf_ref, qb_ref, kb_ref, vb_ref, gb_ref, ogb_ref,
                s0f_ref, s0b_ref, m_ref, cum_ref, gain_ref,
                y_ref, sfo_ref, sbo_ref, oacc_ref, sf_ref, sb_ref, *, nb):
    i = pl.program_id(1)

    @pl.when(i == 0)
    def _():
        sf_ref[...] = s0f_ref[...]
        sb_ref[...] = s0b_ref[...]

    of, ob = _gdn_step(((qf_ref, kf_ref, vf_ref, gf_ref, sf_ref), (qb_ref, kb_ref, vb_ref, gb_ref, sb_ref)),
                       m_ref, cum_ref)
    rows_f = pl.ds(pl.multiple_of(i * GDN_BLOCK, GDN_BLOCK), GDN_BLOCK)
    rows_b = pl.ds(pl.multiple_of((nb - 1 - i) * GDN_BLOCK, GDN_BLOCK), GDN_BLOCK)
    gain = gain_ref[...]

    def split(x):
        return [x[:, h * DV_C:(h + 1) * DV_C] for h in range(H_C)]

    @pl.when(2 * i < nb - 1)
    def _():
        oacc_ref[rows_f, :] = jnp.concatenate(of, axis=1)
        oacc_ref[rows_b, :] = jnp.concatenate(ob, axis=1)

    if nb % 2 == 1:
        @pl.when(2 * i == nb - 1)
        def _():
            y_ref[rows_f, :] = _gdn_finalize([a + b for a, b in zip(of, ob)], ogf_ref, gain)

    @pl.when(2 * i > nb - 1)
    def _():
        y_ref[rows_f, :] = _gdn_finalize([a + b for a, b in zip(split(oacc_ref[rows_f, :]), of)], ogf_ref, gain)
        y_ref[rows_b, :] = _gdn_finalize([a + b for a, b in zip(split(oacc_ref[rows_b, :]), ob)], ogb_ref, gain)

    @pl.when(i == nb - 1)
    def _():
        sfo_ref[...] = sf_ref[...]
        sbo_ref[...] = sb_ref[...]


def _gdn(q, k, v, gb, og, s0f, s0b, masks, cums, gain3, l):
    b, n, _ = q.shape
    nb = n // GDN_BLOCK
    fwd = pl.BlockSpec((None, GDN_BLOCK, D_CQK), lambda bb, i: (bb, i, 0))
    bwd = pl.BlockSpec((None, GDN_BLOCK, D_CQK), lambda bb, i: (bb, nb - 1 - i, 0))
    gfwd = pl.BlockSpec((None, GDN_BLOCK, DK_C), lambda bb, i: (bb, i, 0))
    gbwd = pl.BlockSpec((None, GDN_BLOCK, DK_C), lambda bb, i: (bb, nb - 1 - i, 0))
    st = pl.BlockSpec((None, H_C, DK_C, DV_C), lambda bb, i: (bb, 0, 0, 0))
    st_shape = jax.ShapeDtypeStruct((b, H_C, DK_C, DV_C), F32)
    return pl.pallas_call(
        functools.partial(_gdn_kernel, nb=nb),
        name="gdn_scan",
        grid=(b, nb),
        in_specs=[fwd, fwd, fwd, gfwd, fwd, bwd, bwd, bwd, gbwd, bwd,
                  st, st,
                  pl.BlockSpec((8, GDN_SUB, GDN_SUB), lambda bb, i: (0, 0, 0)),
                  pl.BlockSpec((2, GDN_BLOCK, GDN_BLOCK), lambda bb, i: (0, 0, 0)),
                  _layer_spec((1, DV_C), l)],
        out_specs=[pl.BlockSpec((None, n, D_CV), lambda bb, i: (bb, 0, 0)), st, st],
        out_shape=[jax.ShapeDtypeStruct((b, n, D_CV), BF16), st_shape, st_shape],
        scratch_shapes=[pltpu.VMEM((n, D_CV), F32), pltpu.VMEM((H_C, DK_C, DV_C), F32),
                        pltpu.VMEM((H_C, DK_C, DV_C), F32)],
        compiler_params=_cparams("parallel", "arbitrary"),
    )(q, k, v, gb, og, q, k, v, gb, og, s0f, s0b, masks, cums, gain3)


def _merge_kernel(h_ref, ya_ref, yb_ref, yc_ref, wga_ref, wgb_ref, wgc_ref, wbr_ref, z_ref):
    h = h_ref[...]
    gates = [_dot(h, w_ref[...]) for w_ref in (wga_ref, wgb_ref, wgc_ref)]
    vals = [_dot(y_ref[...], wbr_ref[br]) for br, y_ref in enumerate((ya_ref, yb_ref, yc_ref))]
    z = None
    for gt, vl in zip(gates, vals):
        t = _sigmoid(gt) * vl
        z = t if z is None else z + t
    z_ref[...] = z.astype(BF16)


def _merge(h, ya, yb, yc, wgates, wbr4, l, tm, tn):
    b, n, d = h.shape
    gate_spec = lambda br: pl.BlockSpec((None, d, tn), lambda bb, i, j: (l, 0, br * (d // tn) + j))
    yt = pl.BlockSpec((None, tm, D_A), lambda bb, i, j: (bb, i, 0))
    return pl.pallas_call(
        _merge_kernel,
        name="merge_gate",
        grid=(b, n // tm, d // tn),
        in_specs=[
            pl.BlockSpec((None, tm, d), lambda bb, i, j: (bb, i, 0)), yt, yt, yt,
            gate_spec(0), gate_spec(1), gate_spec(2),
            pl.BlockSpec((None, 3, D_A, tn), lambda bb, i, j: (l, 0, 0, j)),
        ],
        out_specs=pl.BlockSpec((None, tm, tn), lambda bb, i, j: (bb, i, j)),
        out_shape=jax.ShapeDtypeStruct((b, n, d), BF16),
        compiler_params=_cparams("parallel", "parallel", "parallel"),
    )(h, ya, yb, yc, wgates, wgates, wgates, wbr4)


def _resid_kernel(x_ref, a_ref, gt_ref, w_ref, *rest, with_norm):
    xn = x_ref[...] + gt_ref[...] * _dot(a_ref[...], w_ref[...])
    if with_norm:
        g_ref, sh_ref, sc_ref, o_ref, h_ref = rest
        h_ref[...] = _norm_mod(xn, g_ref[...], sh_ref[...], sc_ref[...]).astype(BF16)
    else:
        (o_ref,) = rest
    o_ref[...] = xn


def _resid_mm(x, a, mods, l, row, j_gate, w3, norm, tm):
    b, n, d = x.shape
    kdim = a.shape[-1]
    xt = pl.BlockSpec((None, tm, d), lambda bb, i: (bb, i, 0))
    in_specs = [xt, pl.BlockSpec((None, tm, kdim), lambda bb, i: (bb, i, 0)), _mod_spec(d, l, j_gate, row),
                _layer_spec((kdim, d), l)]
    args = [x, a, mods, w3]
    out_specs, out_shape = [xt], [jax.ShapeDtypeStruct((b, n, d), F32)]
    if norm is not None:
        g3, ln, j_shift = norm
        in_specs += [_layer_spec((1, d), ln), _mod_spec(d, ln, j_shift, row), _mod_spec(d, ln, j_shift + 1, row)]
        args += [g3, mods, mods]
        out_specs.append(xt)
        out_shape.append(jax.ShapeDtypeStruct((b, n, d), BF16))
    res = pl.pallas_call(
        functools.partial(_resid_kernel, with_norm=norm is not None),
        name="resid_mm",
        grid=(b, n // tm),
        in_specs=in_specs,
        out_specs=out_specs,
        out_shape=out_shape,
        compiler_params=_cparams("parallel", "parallel"),
    )(*args)
    return (res[0], res[1]) if norm is not None else (res[0], None)


def _ffn_up_kernel(hp_ref, h_ref, hn_ref, wup_ref, cw_ref, cb_ref, o_ref, ug_ref, uv_ref, *, tm, tf):
    i = pl.program_id(1)
    last = pl.num_programs(1) - 1
    hp = jnp.where(i > 0, hp_ref[...], jnp.zeros_like(hp_ref))
    hn = jnp.where(i < last, hn_ref[...], jnp.zeros_like(hn_ref))
    hc = h_ref[...]
    pad = FFN_CONV // 2
    n_ext = tm + 2 * HALO
    nf = D_FF // tf

    def project(j):
        slot = j % 2
        for u_ref, c0 in ((ug_ref, j * tf), (uv_ref, D_FF + j * tf)):
            w = wup_ref[:, c0:c0 + tf]
            u_ref[slot, 0:HALO] = _dot(hp, w)
            u_ref[slot, HALO:HALO + tm] = _dot(hc, w)
            u_ref[slot, HALO + tm:n_ext] = _dot(hn, w)

    def conv(j):
        slot = j % 2
        ug = ug_ref[slot]
        uv = uv_ref[slot]
        gc = slice(j * tf, (j + 1) * tf)
        vc = slice(D_FF + j * tf, D_FF + (j + 1) * tf)
        cg = cb_ref[:, gc]
        cv = cb_ref[:, vc]
        for kk in range(FFN_CONV):
            sh = (pad - kk) % n_ext
            ugk = ug if sh == 0 else pltpu.roll(ug, sh, axis=0)
            uvk = uv if sh == 0 else pltpu.roll(uv, sh, axis=0)
            cg = cg + ugk[HALO:HALO + tm] * cw_ref[kk:kk + 1, gc]
            cv = cv + uvk[HALO:HALO + tm] * cw_ref[kk:kk + 1, vc]
        o_ref[:, gc] = (_silu(cg) * cv).astype(BF16)

    project(0)
    for j in range(1, nf):
        project(j)
        conv(j - 1)
    conv(nf - 1)


def _ffn_up(h, wup3, cw3, cb3, l, tm, tf):
    b, n, d = h.shape
    prev, nxt = _halo_specs(tm, n, d)
    const = lambda shape: _layer_spec(shape, l, pipeline_mode=pl.Buffered(1))
    return pl.pallas_call(
        functools.partial(_ffn_up_kernel, tm=tm, tf=tf),
        name="ffn_up",
        grid=(b, n // tm),
        in_specs=[
            prev, pl.BlockSpec((None, tm, d), lambda bb, i: (bb, i, 0)), nxt,
            const((d, 2 * D_FF)), const((FFN_CONV, 2 * D_FF)), const((1, 2 * D_FF)),
        ],
        out_specs=pl.BlockSpec((None, tm, D_FF), lambda bb, i: (bb, i, 0)),
        out_shape=jax.ShapeDtypeStruct((b, n, D_FF), BF16),
        scratch_shapes=[pltpu.VMEM((2, tm + 2 * HALO, tf), F32), pltpu.VMEM((2, tm + 2 * HALO, tf), F32)],
        compiler_params=_cparams("parallel", "parallel"),
    )(h, h, h, wup3, cw3, cb3)


def _rope_tables(n_tok):
    t = jnp.arange(n_tok)
    row = (t // GRID_W).astype(F32)
    col = (t % GRID_W).astype(F32)
    n_freq = DK_C // 4
    inv = jnp.power(ROPE_BASE, -jnp.arange(n_freq, dtype=F32) / n_freq)
    ar = row[:, None] * inv
    ac = col[:, None] * inv
    cos = jnp.concatenate([jnp.cos(ar), jnp.cos(ar), jnp.cos(ac), jnp.cos(ac)], axis=-1)
    sin = jnp.concatenate([-jnp.sin(ar), jnp.sin(ar), -jnp.sin(ac), jnp.sin(ac)], axis=-1)
    return cos, sin


def _block_ones(n, blk):
    idx = np.arange(n) // blk
    return jnp.asarray((idx[:, None] == idx[None, :]).astype(np.float32), dtype=BF16)


def kernel(x, c, ctx, c_ctx, ada_w, ada_b, norm1_g, norm2_g, w_in, conv_a_w, conv_a_b, ln_a_g, ln_a_b, qn_g, kn_g,
           rpb, conv_c_w, a_log, dt_bias, onorm_g, w_branch, w_out, ffn_up, ffn_conv_w, ffn_conv_b, ffn_down):
    batch, n_lat, d = x.shape
    n_ctx = ctx.shape[1]
    depth = ada_w.shape[0]

    cc = jnp.zeros((8, d), F32).at[:batch].set(c).at[batch].set(c_ctx)
    mods = _ada(cc, ada_w, ada_b).reshape(depth, 8, 6, 1, d)
    ctx_row = batch

    rope_c, rope_s = _rope_tables(n_lat)
    ones_c = jnp.ones((n_ctx, DK_C), F32)
    zeros_c = jnp.zeros((n_ctx, DK_C), F32)
    gm64 = _block_ones(D_B, DH_B)
    gm128 = _block_ones(D_CQK, DK_C)
    masks = jnp.asarray(_chunk_masks(GDN_SUB))
    blk_masks = _chunk_masks(GDN_BLOCK)
    cums = jnp.asarray(np.stack([blk_masks[0].T, blk_masks[2].T]), dtype=BF16)
    s_zero = jnp.zeros((batch, H_C, DK_C, DV_C), F32)

    off_g = 2 * D_A + 3 * D_B + 3 * D_CQK + D_CV + 4 * H_C
    w_bf = w_in.astype(BF16)
    wgates = w_bf[:, :, off_g:]
    wbr4 = w_branch.astype(BF16)
    wo3 = w_out.astype(BF16)
    wup3 = ffn_up.astype(BF16)
    wdn3 = ffn_down.astype(BF16)
    g1_3 = norm1_g.reshape(depth, 1, d)
    g2_3 = norm2_g.reshape(depth, 1, d)
    gq3 = (jnp.tile(qn_g, (1, H_B)) * (DH_B ** -0.5 * LOG2E)).reshape(depth, 1, D_B)
    gk3 = jnp.tile(kn_g, (1, H_B)).reshape(depth, 1, D_B)
    gbp3 = jnp.zeros((depth, 8, DK_C), F32)
    gbp3 = gbp3.at[:, 0, :2 * H_C].set(a_log.reshape(depth, 2 * H_C)).at[:, 1, :2 * H_C].set(
        dt_bias.reshape(depth, 2 * H_C)).at[:, 2, :2 * H_C].set(1.0).at[:, 3, 2 * H_C:4 * H_C].set(1.0)
    bias5 = _na_bias_table(rpb)
    cb_a3 = conv_a_b.reshape(depth, 1, D_A)
    lg_a3 = ln_a_g.reshape(depth, 1, D_A)
    lb_a3 = ln_a_b.reshape(depth, 1, D_A)
    cbf3 = ffn_conv_b.reshape(depth, 1, 2 * D_FF)
    gain3 = onorm_g.reshape(depth, 1, DV_C)

    tl = _tiles(n_lat)
    tc = _tiles(n_ctx)
    x_lat, x_ctx = x, ctx
    h_lat = _normmod(x_lat, g1_3, mods, 0, None, 0, tl["norm"])
    h_ctx = _normmod(x_ctx, g1_3, mods, 0, ctx_row, 0, tc["norm"])
    for l in range(depth):
        ctx_out = l < depth - 1
        next_norm = (g1_3, l + 1, 0) if ctx_out else None

        qb_c, kb_c, vb_c = _proj_b(h_ctx, w_bf, gq3, gk3, gm64, l, tc["proj"])
        qc_c, kc_c, vc_c, og_c, gb_c = _proj_c(h_ctx, w_bf, conv_c_w, ones_c, zeros_c, gm128, gbp3, l,
                                               tc["proj"], False)
        yc_c, sf_c, sb_c = _gdn(qc_c, kc_c, vc_c, gb_c, og_c, s_zero, s_zero, masks, cums, gain3, l)

        ya_l = _conv_a(_proj_a(h_lat, w_bf, l, tl["proj"]), conv_a_w, cb_a3, lg_a3, lb_a3, l, tl["conv_a"])
        qb_l, kb_l, vb_l = _proj_b(h_lat, w_bf, gq3, gk3, gm64, l, tl["proj"])
        yb_l = _na_attention(qb_l, kb_l, vb_l, kb_c, vb_c, bias5, l, NA_ROWS_PER_STEP)
        qc_l, kc_l, vc_l, og_l, gb_l = _proj_c(h_lat, w_bf, conv_c_w, rope_c, rope_s, gm128, gbp3, l,
                                               tl["proj"], True)
        yc_l, _, _ = _gdn(qc_l, kc_l, vc_l, gb_l, og_l, sf_c, sb_c, masks, cums, gain3, l)
        z_l = _merge(h_lat, ya_l, yb_l, yc_l, wgates, wbr4, l, tl["merge"], MERGE_TN)
        x_lat, h2_l = _resid_mm(x_lat, z_l, mods, l, None, 2, wo3, (g2_3, l, 3), tl["resid"])
        act_l = _ffn_up(h2_l, wup3, ffn_conv_w, cbf3, l, tl["ffn"], FFN_TF)
        x_lat, h_lat = _resid_mm(x_lat, act_l, mods, l, None, 5, wdn3, next_norm, tl["resid"])

        if ctx_out:
            ya_c = _conv_a(_proj_a(h_ctx, w_bf, l, tc["proj"]), conv_a_w, cb_a3, lg_a3, lb_a3, l, tc["conv_a"])
            yb_c = _ctx_attention(qb_c, kb_c, vb_c)
            z_c = _merge(h_ctx, ya_c, yb_c, yc_c, wgates, wbr4, l, tc["merge"], MERGE_TN)
            x_ctx, h2_c = _resid_mm(x_ctx, z_c, mods, l, ctx_row, 2, wo3, (g2_3, l, 3), tc["resid"])
            act_c = _ffn_up(h2_c, wup3, ffn_conv_w, cbf3, l, tc["ffn"], FFN_TF)
            x_ctx, h_ctx = _resid_mm(x_ctx, act_c, mods, l, ctx_row, 5, wdn3, next_norm, tc["resid"])
    return x_lat
```

```python
import functools
import math

import numpy as np
import jax
import jax.numpy as jnp
from jax import lax
from jax.experimental import pallas as pl
from jax.experimental.pallas import tpu as pltpu

F32 = jnp.float32
BF16 = jnp.bfloat16

D_MODEL = 1024
GRID_W = 64
EPS = 1e-6
NEG_INF = -1e30
LOG2E = math.log2(math.e)
D_A = 512
CONV_A = 31
H_B = 8
DH_B = 64
D_B = H_B * DH_B
NA_ROWS = 8
NA_COLS = 16
H_C = 4
DK_C = 128
DV_C = 128
D_CQK = H_C * DK_C
D_CV = H_C * DV_C
SHORT_CONV = 4
CHUNK = 64
ROPE_BASE = 10000.0
D_FF = 2816
FFN_CONV = 3

VMEM_LIMIT_BYTES = 48 * 1024 * 1024
HALO = 16
SUBLANES = 8
GDN_BLOCK = 256
GDN_SUB = 128
NA_ROWS_PER_STEP = 16
HEADS_PER_STEP = 2
MERGE_TN = 1024
FFN_TF = 256


def _tiles(n_tok):
    cap = lambda t: min(t, n_tok)
    return {"norm": cap(1024), "proj": cap(1024), "conv_a": cap(512), "merge": cap(1024), "resid": cap(1024),
            "ffn": cap(1024)}


def _cparams(*sem):
    return pltpu.CompilerParams(dimension_semantics=sem, vmem_limit_bytes=VMEM_LIMIT_BYTES)


def _dot(a, b):
    return jnp.dot(a, b, preferred_element_type=F32)


def _dot_nt(a, b):
    return lax.dot_general(a, b, (((1,), (1,)), ((), ())), preferred_element_type=F32)


def _split_bf16(x, n):
    parts = []
    r = x
    for idx in range(n):
        p = r.astype(BF16)
        parts.append(p)
        if idx + 1 < n:
            r = r - p.astype(F32)
    return parts


def _dot_exact_rhs(a, b_bf16, n):
    out = None
    for p in _split_bf16(a, n):
        t = _dot(p, b_bf16)
        out = t if out is None else out + t
    return out


def _dot_exact_lhs(a_bf16, b, n):
    out = None
    for p in _split_bf16(b, n):
        t = _dot(a_bf16, p)
        out = t if out is None else out + t
    return out


def _mm(a, b, passes=1):
    if passes == 1:
        return _dot(a.astype(BF16), b.astype(BF16))
    a_hi, a_lo = _split_bf16(a, 2)
    b_hi, b_lo = _split_bf16(b, 2)
    return _dot(a_hi, b_hi) + (_dot(a_lo, b_hi) + _dot(a_hi, b_lo))


def _sigmoid(x):
    return 1.0 / (1.0 + jnp.exp(-x))


def _silu(x):
    return x * _sigmoid(x)


def _softplus(x):
    return jnp.maximum(x, 0.0) + jnp.log(1.0 + jnp.exp(-jnp.abs(x)))


def _norm_mod(x, g, shift, scale):
    ms = jnp.mean(x * x, axis=-1, keepdims=True)
    y = x * lax.rsqrt(ms + EPS) * g
    return y * (1.0 + scale) + shift


def _layer_spec(shape, l, **kw):
    return pl.BlockSpec((None,) + tuple(shape), lambda *_: (l,) + (0,) * len(shape), **kw)


def _wcols_spec(rows, width, l, col_block):
    return pl.BlockSpec((None, rows, width), lambda *_: (l, 0, col_block))


def _mod_spec(d, l, j, row):
    if row is None:
        return pl.BlockSpec((None, None, None, 1, d), lambda bb, *_: (l, bb, j, 0, 0))
    return pl.BlockSpec((None, None, None, 1, d), lambda *_: (l, row, j, 0, 0))


def _halo_specs(tm, n_tok, width):
    per = tm // HALO
    nblk = n_tok // HALO
    prev = pl.BlockSpec((None, HALO, width), lambda b, i, *_: (b, jnp.maximum(i * per - 1, 0), 0))
    nxt = pl.BlockSpec((None, HALO, width), lambda b, i, *_: (b, jnp.minimum((i + 1) * per, nblk - 1), 0))
    return prev, nxt


def _ada_kernel(c_ref, w_ref, b_ref, o_ref):
    s = _silu(c_ref[...])
    o_ref[...] = _mm(s, w_ref[...], 3) + b_ref[...]


def _ada(cc, ada_w, ada_b):
    depth = ada_w.shape[0]
    d = D_MODEL
    return pl.pallas_call(
        _ada_kernel,
        name="ada_mod",
        grid=(depth, 6),
        in_specs=[
            pl.BlockSpec((8, d), lambda l, j: (0, 0)),
            pl.BlockSpec((None, d, d), lambda l, j: (l, 0, j)),
            pl.BlockSpec((None, 1, d), lambda l, j: (l, 0, j)),
        ],
        out_specs=pl.BlockSpec((None, 8, d), lambda l, j: (l, 0, j)),
        out_shape=jax.ShapeDtypeStruct((depth, 8, 6 * d), F32),
        compiler_params=_cparams("arbitrary", "arbitrary"),
    )(cc, ada_w, ada_b.reshape(depth, 1, 6 * d))


def _normmod_kernel(x_ref, g_ref, sh_ref, sc_ref, o_ref):
    o_ref[...] = _norm_mod(x_ref[...], g_ref[...], sh_ref[...], sc_ref[...]).astype(BF16)


def _normmod(x, g3, mods, l, row, j_shift, tm):
    b, n, d = x.shape
    tile = pl.BlockSpec((None, tm, d), lambda bb, i: (bb, i, 0))
    return pl.pallas_call(
        _normmod_kernel,
        name="norm_mod",
        grid=(b, n // tm),
        in_specs=[tile, _layer_spec((1, d), l), _mod_spec(d, l, j_shift, row), _mod_spec(d, l, j_shift + 1, row)],
        out_specs=tile,
        out_shape=jax.ShapeDtypeStruct((b, n, d), BF16),
        compiler_params=_cparams("parallel", "parallel"),
    )(x, g3, mods, mods)


def _proj_a_kernel(h_ref, wa_ref, wg_ref, o_ref):
    h = h_ref[...]
    wc = D_A // 2

    def project(c):
        cols = slice(c * wc, (c + 1) * wc)
        return _dot(h, wa_ref[:, cols]), _dot(h, wg_ref[:, cols])

    first = project(0)
    second = project(1)
    o_ref[:, 0:wc] = first[0] * _sigmoid(first[1])
    o_ref[:, wc:2 * wc] = second[0] * _sigmoid(second[1])


def _proj_a(h, w_bf, l, tm):
    b, n, d = h.shape
    return pl.pallas_call(
        _proj_a_kernel,
        name="proj_a",
        grid=(b, n // tm),
        in_specs=[
            pl.BlockSpec((None, tm, d), lambda bb, i: (bb, i, 0)),
            _wcols_spec(d, D_A, l, 0), _wcols_spec(d, D_A, l, 1),
        ],
        out_specs=pl.BlockSpec((None, tm, D_A), lambda bb, i: (bb, i, 0)),
        out_shape=jax.ShapeDtypeStruct((b, n, D_A), F32),
        compiler_params=_cparams("parallel", "parallel"),
    )(h, w_bf, w_bf)


def _conv_a_kernel(yp_ref, y_ref, yn_ref, cw_ref, cb_ref, lg_ref, lb_ref, o_ref, ext_ref, *, tm):
    i = pl.program_id(1)
    last = pl.num_programs(1) - 1
    ext_ref[0, 0:HALO] = yp_ref[...] * (i > 0).astype(F32)
    ext_ref[0, HALO:HALO + tm] = y_ref[...]
    ext_ref[0, HALO + tm:2 * HALO + tm] = yn_ref[...] * (i < last).astype(F32)
    n_ext = tm + 2 * HALO
    ext0 = ext_ref[0]
    for r in range(1, SUBLANES):
        ext_ref[r] = pltpu.roll(ext0, n_ext - r, axis=0)
    pad = CONV_A // 2
    acc = jnp.zeros((tm, D_A), F32) + cb_ref[...]
    for k in range(CONV_A):
        off = HALO - pad + k
        r = off % SUBLANES
        acc = acc + ext_ref[r, pl.ds(off - r, tm), :] * cw_ref[k:k + 1, :]
    mu = jnp.mean(acc, axis=-1, keepdims=True)
    cen = acc - mu
    var = jnp.mean(cen * cen, axis=-1, keepdims=True)
    y = cen * lax.rsqrt(var + EPS) * lg_ref[...] + lb_ref[...]
    o_ref[...] = _silu(y).astype(BF16)


def _conv_a(y, cw3, cb3, lg3, lb3, l, tm):
    b, n, c = y.shape
    prev, nxt = _halo_specs(tm, n, c)
    vec = _layer_spec((1, c), l)
    return pl.pallas_call(
        functools.partial(_conv_a_kernel, tm=tm),
        name="conv_a",
        grid=(b, n // tm),
        in_specs=[prev, pl.BlockSpec((None, tm, c), lambda bb, i: (bb, i, 0)), nxt,
                  _layer_spec((CONV_A, c), l), vec, vec, vec],
        out_specs=pl.BlockSpec((None, tm, c), lambda bb, i: (bb, i, 0)),
        out_shape=jax.ShapeDtypeStruct((b, n, c), BF16),
        scratch_shapes=[pltpu.VMEM((SUBLANES, tm + 2 * HALO, c), F32)],
        compiler_params=_cparams("parallel", "parallel"),
    )(y, y, y, cw3, cb3, lg3, lb3)


def _proj_b_kernel(h_ref, wq_ref, wk_ref, wv_ref, gq_ref, gk_ref, gm_ref, q_ref, k_ref, v_ref):
    h = h_ref[...]
    for w_ref, gain_ref, o_ref in ((wq_ref, gq_ref, q_ref), (wk_ref, gk_ref, k_ref), (wv_ref, None, v_ref)):
        acc = _dot(h, w_ref[...])
        if gain_ref is not None:
            ss = _dot((acc * acc).astype(BF16), gm_ref[...])
            acc = acc * lax.rsqrt(ss * (1.0 / DH_B) + EPS) * gain_ref[...]
        for hh in range(H_B):
            o_ref[hh] = acc[:, hh * DH_B:(hh + 1) * DH_B].astype(BF16)


def _proj_b(h, w_bf, gq3, gk3, gmat, l, tm):
    b, n, d = h.shape
    vec = _layer_spec((1, D_B), l)
    first = 2 * D_A // D_B
    head_out = pl.BlockSpec((None, H_B, tm, DH_B), lambda bb, i: (bb, 0, i, 0))
    shp = jax.ShapeDtypeStruct((b, H_B, n, DH_B), BF16)
    return pl.pallas_call(
        _proj_b_kernel,
        name="proj_b",
        grid=(b, n // tm),
        in_specs=[
            pl.BlockSpec((None, tm, d), lambda bb, i: (bb, i, 0)),
            _wcols_spec(d, D_B, l, first), _wcols_spec(d, D_B, l, first + 1), _wcols_spec(d, D_B, l, first + 2),
            vec, vec,
            pl.BlockSpec((D_B, D_B), lambda bb, i: (0, 0)),
        ],
        out_specs=[head_out, head_out, head_out],
        out_shape=[shp, shp, shp],
        compiler_params=_cparams("parallel", "parallel"),
    )(h, w_bf, w_bf, w_bf, gq3, gk3, gmat)


def _na_kernel(q_ref, k_ref, v_ref, kc_ref, vc_ref, bias_ref, o_ref, *, rb, rows):
    i = pl.program_id(2)
    n_loc = NA_ROWS * GRID_W
    rws = [i * rb + rr for rr in range(rb)]
    starts = [jnp.clip(r - NA_ROWS // 2, 0, rows - NA_ROWS) for r in rws]
    variants = [st - r + NA_ROWS - 1 for st, r in zip(starts, rws)]
    tok0 = [pl.multiple_of(st * GRID_W, GRID_W) for st in starts]
    probs = [(hh, rr) for hh in range(HEADS_PER_STEP) for rr in range(rb)]
    qs = [q_ref[hh, rr * GRID_W:(rr + 1) * GRID_W, :] for hh, rr in probs]
    s = [_dot_nt(q, k_ref[hh, pl.ds(tok0[rr], n_loc), :]) for q, (hh, rr) in zip(qs, probs)]
    sc = [_dot_nt(q, kc_ref[hh]) for q, (hh, rr) in zip(qs, probs)]
    bias = {(hh, rr): jnp.concatenate([bias_ref[hh, variants[rr] + kk] for kk in range(0, NA_ROWS, 2)], axis=1)
            for hh, rr in probs}
    s = [a + bias[pr] for a, pr in zip(s, probs)]
    m = [jnp.maximum(jnp.max(a, axis=-1, keepdims=True), jnp.max(b, axis=-1, keepdims=True)) for a, b in zip(s, sc)]
    p = [jnp.exp2(a - mm) for a, mm in zip(s, m)]
    pc = [jnp.exp2(b - mm) for b, mm in zip(sc, m)]
    l = [jnp.sum(a, axis=-1, keepdims=True) + jnp.sum(b, axis=-1, keepdims=True) for a, b in zip(p, pc)]
    o = [_dot(a.astype(BF16), v_ref[hh, pl.ds(tok0[rr], n_loc), :]) + _dot(b.astype(BF16), vc_ref[hh])
         for a, b, (hh, rr) in zip(p, pc, probs)]
    o = [a / b for a, b in zip(o, l)]
    for rr in range(rb):
        o_ref[rr * GRID_W:(rr + 1) * GRID_W, :] = jnp.concatenate(
            [o[probs.index((hh, rr))] for hh in range(HEADS_PER_STEP)], axis=1).astype(BF16)


def _na_attention(q, k, v, kc, vc, bias, l, rb):
    b, h, t, dh = q.shape
    n_ctx = kc.shape[2]
    rows = t // GRID_W
    hs = HEADS_PER_STEP
    full = pl.BlockSpec((None, hs, t, dh), lambda bb, hp, i: (bb, hp, 0, 0))
    cfull = pl.BlockSpec((None, hs, n_ctx, dh), lambda bb, hp, i: (bb, hp, 0, 0))
    return pl.pallas_call(
        functools.partial(_na_kernel, rb=rb, rows=rows),
        name="na_attn",
        grid=(b, h // hs, rows // rb),
        in_specs=[pl.BlockSpec((None, hs, rb * GRID_W, dh), lambda bb, hp, i: (bb, hp, i, 0)),
                  full, full, cfull, cfull,
                  pl.BlockSpec((None, hs, 2 * NA_ROWS - 2, GRID_W, 2 * GRID_W),
                               lambda bb, hp, i: (l, hp, 0, 0, 0))],
        out_specs=pl.BlockSpec((None, rb * GRID_W, hs * dh), lambda bb, hp, i: (bb, i, hp)),
        out_shape=jax.ShapeDtypeStruct((b, t, h * dh), BF16),
        compiler_params=_cparams("parallel", "parallel", "arbitrary"),
    )(q, k, v, kc, vc, bias)


def _ctx_attn_kernel(q_ref, k_ref, v_ref, o_ref):
    outs = []
    for hh in range(HEADS_PER_STEP):
        s = _dot_nt(q_ref[hh], k_ref[hh])
        m = jnp.max(s, axis=-1, keepdims=True)
        p = jnp.exp2(s - m)
        l = jnp.sum(p, axis=-1, keepdims=True)
        outs.append(_dot(p.astype(BF16), v_ref[hh]) / l)
    o_ref[...] = jnp.concatenate(outs, axis=1).astype(BF16)


def _ctx_attention(q, k, v):
    b, h, n, dh = q.shape
    hs = HEADS_PER_STEP
    full = pl.BlockSpec((None, hs, n, dh), lambda bb, hp: (bb, hp, 0, 0))
    return pl.pallas_call(
        _ctx_attn_kernel,
        name="ctx_attn",
        grid=(b, h // hs),
        in_specs=[full, full, full],
        out_specs=pl.BlockSpec((None, n, hs * dh), lambda bb, hp: (bb, 0, hp)),
        out_shape=jax.ShapeDtypeStruct((b, n, h * dh), BF16),
        compiler_params=_cparams("parallel", "parallel"),
    )(q, k, v)


def _na_bias_table(rpb):
    cidx = np.arange(GRID_W)
    dc = np.clip(cidx[None, :] - cidx[:, None] + NA_COLS - 1, 0, 2 * NA_COLS - 2)
    onehot = (dc[None] == np.arange(2 * NA_COLS - 1)[:, None, None]).astype(np.float32)
    cs = np.clip(cidx - NA_COLS // 2, 0, GRID_W - NA_COLS)
    col_ok = (cidx[None, :] >= cs[:, None]) & (cidx[None, :] < cs[:, None] + NA_COLS)
    toep = jnp.einsum("lhrd,dqk->lhrqk", rpb, onehot, precision=lax.Precision.HIGHEST)
    toep = jnp.where(col_ok, toep * LOG2E, NEG_INF)
    return jnp.concatenate([toep[:, :, :-1], toep[:, :, 1:]], axis=-1)


def _proj_c_kernel(hp_ref, h_ref, hn_ref, wq_ref, wk_ref, wv_ref, wo_ref, wdb_ref, cw_ref, rc_ref, rs_ref,
                   gm_ref, gbp_ref, q_ref, k_ref, v_ref, og_ref, gb_ref, hs_ref, p_ref, *, tm, use_rope):
    i = pl.program_id(1)
    last = pl.num_programs(1) - 1
    pad_l = SHORT_CONV // 2
    n_ext = tm + 2 * HALO
    hs_ref[0:HALO] = jnp.where(i > 0, hp_ref[...], jnp.zeros_like(hp_ref))
    hs_ref[HALO:HALO + tm] = h_ref[...]
    hs_ref[HALO + tm:n_ext] = jnp.where(i < last, hn_ref[...], jnp.zeros_like(hn_ref))
    hs = hs_ref[...]
    hc = h_ref[...]
    lane = lax.broadcasted_iota(jnp.int32, (tm, DK_C), 1)
    first_half = (lane & (DK_C // 2 - 1)) < DK_C // 4
    sections = ((wq_ref, q_ref), (wk_ref, k_ref), (wv_ref, v_ref))

    def project(sec):
        p_ref[sec % 2] = _dot(hs, sections[sec][0][...])

    def finish(sec):
        o_ref = sections[sec][1]
        cols = slice(sec * D_CQK, (sec + 1) * D_CQK)
        pv = p_ref[sec % 2]
        y = None
        for kk in range(SHORT_CONV):
            sh = (pad_l - kk) % n_ext
            pk = pv if sh == 0 else pltpu.roll(pv, sh, axis=0)
            t = pk[HALO:HALO + tm] * cw_ref[kk:kk + 1, cols]
            y = t if y is None else y + t
        y = _silu(y)
        if sec < 2:
            ss = _dot((y * y).astype(BF16), gm_ref[...])
            y = y * lax.rsqrt(ss + EPS)
            if use_rope:
                heads = []
                for hh in range(H_C):
                    yh = y[:, hh * DK_C:(hh + 1) * DK_C]
                    swapped = jnp.where(first_half, pltpu.roll(yh, DK_C - DK_C // 4, axis=1),
                                        pltpu.roll(yh, DK_C // 4, axis=1))
                    heads.append(yh * rc_ref[...] + swapped * rs_ref[...])
                y = jnp.concatenate(heads, axis=1)
            if sec == 0:
                y = y * (DK_C ** -0.5)
        o_ref[...] = y

    project(0)
    for sec in range(1, len(sections)):
        project(sec)
        finish(sec - 1)
    og_ref[...] = _dot(hc, wo_ref[...])
    finish(len(sections) - 1)
    db = _dot(hc, wdb_ref[...])
    gval = -jnp.exp(gbp_ref[0:1, :]) * _softplus(db + gbp_ref[1:2, :])
    gb_ref[...] = gbp_ref[2:3, :] * gval + gbp_ref[3:4, :] * _sigmoid(db)


def _proj_c(h, w_bf, cw3, rc, rs, gmat, gbp3, l, tm, use_rope):
    b, n, d = h.shape
    first = (2 * D_A + 3 * D_B) // D_CQK
    db_block = (2 * D_A + 3 * D_B + 3 * D_CQK + D_CV) // DK_C
    prev, nxt = _halo_specs(tm, n, d)
    tile512 = pl.BlockSpec((None, tm, D_CQK), lambda bb, i: (bb, i, 0))
    shp = jax.ShapeDtypeStruct((b, n, D_CQK), F32)
    rope_spec = pl.BlockSpec((tm, DK_C), lambda bb, i: (i, 0))
    return pl.pallas_call(
        functools.partial(_proj_c_kernel, tm=tm, use_rope=use_rope),
        name="proj_c",
        grid=(b, n // tm),
        in_specs=[
            prev, pl.BlockSpec((None, tm, d), lambda bb, i: (bb, i, 0)), nxt,
            _wcols_spec(d, D_CQK, l, first), _wcols_spec(d, D_CQK, l, first + 1),
            _wcols_spec(d, D_CQK, l, first + 2), _wcols_spec(d, D_CQK, l, first + 3),
            _wcols_spec(d, DK_C, l, db_block),
            _layer_spec((SHORT_CONV, 3 * D_CQK), l),
            rope_spec, rope_spec,
            pl.BlockSpec((D_CQK, D_CQK), lambda bb, i: (0, 0)),
            _layer_spec((8, DK_C), l),
        ],
        out_specs=[tile512] * 4 + [pl.BlockSpec((None, tm, DK_C), lambda bb, i: (bb, i, 0))],
        out_shape=[shp] * 4 + [jax.ShapeDtypeStruct((b, n, DK_C), F32)],
        scratch_shapes=[pltpu.VMEM((tm + 2 * HALO, d), BF16), pltpu.VMEM((2, tm + 2 * HALO, D_CQK), F32)],
        compiler_params=_cparams("parallel", "parallel"),
    )(h, h, h, w_bf, w_bf, w_bf, w_bf, w_bf, cw3, rc, rs, gmat, gbp3)


def _chunk_masks(n):
    r = np.arange(n)[:, None]
    c = np.arange(n)[None, :]
    same = lambda s: (r // s) == (c // s)
    chunk = same(CHUNK)
    ms = [chunk & (r >= c), chunk & (r > c), chunk & (r <= c), chunk & (r < c), same(8),
          same(16) & ~same(8), same(32) & ~same(16), same(64) & ~same(32)]
    return np.stack(ms).astype(np.float32)


def _tri_inv_all(lmats, eye, m_ref):
    m8 = m_ref[4]
    n0 = [-(l * m8) for l in lmats]
    n2 = [_mm(a, a) for a in n0]
    n4 = [_mm(a, a) for a in n2]
    n3 = [_mm(a, b) for a, b in zip(n0, n2)]
    t1 = [eye + a + b + c for a, b, c in zip(n0, n2, n3)]
    t1n4 = [_mm(a, b) for a, b in zip(t1, n4)]
    t = [a + b for a, b in zip(t1, t1n4)]
    for lvl in (5, 6, 7):
        off = m_ref[lvl]
        lt = [_mm(l * off, a) for l, a in zip(lmats, t)]
        tlt = [_mm(a, b) for a, b in zip(t, lt)]
        t = [a - b for a, b in zip(t, tlt)]
    return t


def _gdn_step(io, m_ref, cum_ref):
    nsub = GDN_BLOCK // GDN_SUB
    csub = GDN_SUB // CHUNK
    nch = nsub * csub
    r16, cols = [], []
    for d in range(2):
        g_t = io[d][3][...].T[0:4 * H_C]
        csum = _dot_exact_rhs(g_t, cum_ref[d], 3)
        rid = lax.broadcasted_iota(jnp.int32, g_t.shape, 0)
        r = jnp.where(rid // H_C == d, csum, g_t)
        r16.append(r)
        cols.append(jnp.concatenate([r, jnp.zeros((DK_C - 4 * H_C, GDN_BLOCK), F32)], axis=0).T)
    probs = [(d, h, sb) for d in range(2) for h in range(H_C) for sb in range(nsub)]
    incl = [m_ref[0], m_ref[2]]
    strict = [m_ref[1], m_ref[3]]
    eye = incl[0] - strict[0]

    def tile(ref, h, sb):
        return ref[sb * GDN_SUB:(sb + 1) * GDN_SUB, h * DK_C:(h + 1) * DK_C]

    q = [tile(io[d][0], h, sb) for d, h, sb in probs]
    k = [tile(io[d][1], h, sb) for d, h, sb in probs]
    v = [tile(io[d][2], h, sb) for d, h, sb in probs]
    gcol, bcol, decay = [], [], []
    for d, h, sb in probs:
        rows = slice(sb * GDN_SUB, (sb + 1) * GDN_SUB)
        c_g = d * H_C + h
        c_b = 2 * H_C + c_g
        gc = jnp.broadcast_to(cols[d][rows, c_g:c_g + 1], (GDN_SUB, DK_C))
        gcol.append(gc)
        bcol.append(jnp.broadcast_to(cols[d][rows, c_b:c_b + 1], (GDN_SUB, DK_C)))
        decay.append(jnp.exp(jnp.where(incl[d] > 0.5, gc - r16[d][c_g:c_g + 1, rows], NEG_INF)))
    k16 = [a.astype(BF16) for a in k]
    qkk = [_dot_nt(jnp.concatenate([a.astype(BF16), b], axis=0), b) for a, b in zip(q, k16)]
    qk = [a[:GDN_SUB] for a in qkk]
    kk = [a[GDN_SUB:] for a in qkk]
    lmats = [a * b * (c * strict[p[0]]) for a, b, c, p in zip(kk, bcol, decay, probs)]
    amat = [(a * c).astype(BF16) for a, c in zip(qk, decay)]
    tinv = _tri_inv_all(lmats, eye, m_ref)
    eg = [jnp.exp(a) for a in gcol]
    rhs = [jnp.concatenate([vv * b, kx * (b * e)], axis=1).astype(BF16) for vv, kx, b, e in zip(v, k, bcol, eg)]
    sol = [_dot(t.astype(BF16), r) for t, r in zip(tinv, rhs)]
    qd = [a * e for a, e in zip(q, eg)]
    gtot, kd_t = [], []
    for (d, h, sb), kx, gc in zip(probs, k, gcol):
        parts, gts = [], []
        for c in range(csub):
            end = c * CHUNK + (CHUNK - 1 if d == 0 else 0)
            gt = gc[end:end + 1, :]
            gts.append(gt)
            sl = slice(c * CHUNK, (c + 1) * CHUNK)
            parts.append(kx[sl] * jnp.exp(gt - gc[sl]))
        gtot.append(gts)
        kd_t.append(jnp.concatenate(parts, axis=0).T.astype(BF16))
    pidx = {p: n for n, p in enumerate(probs)}
    chains = [(d, h) for d in range(2) for h in range(H_C)]
    state = [io[d][4][h] for d, h in chains]
    vnew = [[None] * nch for _ in chains]
    ointer = [[None] * nch for _ in chains]
    zeros = jnp.zeros((CHUNK, DV_C), BF16)
    for step in range(nch):
        pos = [(step if d == 0 else nch - 1 - step) for d, _ in chains]
        loc = [(pidx[(d, h, n // csub)], n % csub) for (d, h), n in zip(chains, pos)]
        sls = [slice(c * CHUNK, (c + 1) * CHUNK) for _, c in loc]
        wq = [jnp.concatenate([sol[p][sl, DV_C:], qd[p][sl]], axis=0).astype(BF16) for (p, _), sl in zip(loc, sls)]
        r = [_dot(a, s.astype(BF16)) for a, s in zip(wq, state)]
        vn = [(sol[p][sl, :DV_C] - rr[0:CHUNK]).astype(BF16) for (p, _), sl, rr in zip(loc, sls, r)]
        vpad = [jnp.concatenate([x if m == c else zeros for m in range(csub)], axis=0) for x, (_, c) in zip(vn, loc)]
        upd = [_dot(kd_t[p], x) for (p, _), x in zip(loc, vpad)]
        state = [s * jnp.exp(gtot[p][c]) + u for s, (p, c), u in zip(state, loc, upd)]
        for ci, n in enumerate(pos):
            vnew[ci][n] = vn[ci]
            ointer[ci][n] = r[ci][CHUNK:2 * CHUNK]
    for (d, h), s in zip(chains, state):
        io[d][4][h] = s
    out = [[None] * H_C for _ in range(2)]
    for ci, (d, h) in enumerate(chains):
        parts = []
        for sb in range(nsub):
            p = pidx[(d, h, sb)]
            vn_sb = jnp.concatenate(vnew[ci][sb * csub:(sb + 1) * csub], axis=0)
            parts.append(jnp.concatenate(ointer[ci][sb * csub:(sb + 1) * csub], axis=0) + _dot(amat[p], vn_sb))
        out[d][h] = jnp.concatenate(parts, axis=0)
    return out


def _gdn_finalize(o_heads, gate_ref, gain):
    ys = []
    for h, o in enumerate(o_heads):
        ms = jnp.mean(o * o, axis=-1, keepdims=True)
        ys.append(o * lax.rsqrt(ms + EPS) * gain * _silu(gate_ref[:, h * DV_C:(h + 1) * DV_C]))
    return jnp.concatenate(ys, axis=1).astype(BF16)


def _gdn_kernel(qf_ref, kf_ref, vf_ref, gf_ref, ogf_ref, qb_ref, kb_ref, vb_ref, gb_ref, ogb_ref,
                s0f_ref, s0b_ref, m_ref, cum_ref, gain_ref,
                y_ref, sfo_ref, sbo_ref, oacc_ref, sf_ref, sb_ref, *, nb):
    i = pl.program_id(1)

    @pl.when(i == 0)
    def _():
        sf_ref[...] = s0f_ref[...]
        sb_ref[...] = s0b_ref[...]

    of, ob = _gdn_step(((qf_ref, kf_ref, vf_ref, gf_ref, sf_ref), (qb_ref, kb_ref, vb_ref, gb_ref, sb_ref)),
                       m_ref, cum_ref)
    rows_f = pl.ds(pl.multiple_of(i * GDN_BLOCK, GDN_BLOCK), GDN_BLOCK)
    rows_b = pl.ds(pl.multiple_of((nb - 1 - i) * GDN_BLOCK, GDN_BLOCK), GDN_BLOCK)
    gain = gain_ref[...]

    def split(x):
        return [x[:, h * DV_C:(h + 1) * DV_C] for h in range(H_C)]

    @pl.when(2 * i < nb - 1)
    def _():
        oacc_ref[rows_f, :] = jnp.concatenate(of, axis=1)
        oacc_ref[rows_b, :] = jnp.concatenate(ob, axis=1)

    if nb % 2 == 1:
        @pl.when(2 * i == nb - 1)
        def _():
            y_ref[rows_f, :] = _gdn_finalize([a + b for a, b in zip(of, ob)], ogf_ref, gain)

    @pl.when(2 * i > nb - 1)
    def _():
        y_ref[rows_f, :] = _gdn_finalize([a + b for a, b in zip(split(oacc_ref[rows_f, :]), of)], ogf_ref, gain)
        y_ref[rows_b, :] = _gdn_finalize([a + b for a, b in zip(split(oacc_ref[rows_b, :]), ob)], ogb_ref, gain)

    @pl.when(i == nb - 1)
    def _():
        sfo_ref[...] = sf_ref[...]
        sbo_ref[...] = sb_ref[...]


def _gdn(q, k, v, gb, og, s0f, s0b, masks, cums, gain3, l):
    b, n, _ = q.shape
    nb = n // GDN_BLOCK
    fwd = pl.BlockSpec((None, GDN_BLOCK, D_CQK), lambda bb, i: (bb, i, 0))
    bwd = pl.BlockSpec((None, GDN_BLOCK, D_CQK), lambda bb, i: (bb, nb - 1 - i, 0))
    gfwd = pl.BlockSpec((None, GDN_BLOCK, DK_C), lambda bb, i: (bb, i, 0))
    gbwd = pl.BlockSpec((None, GDN_BLOCK, DK_C), lambda bb, i: (bb, nb - 1 - i, 0))
    st = pl.BlockSpec((None, H_C, DK_C, DV_C), lambda bb, i: (bb, 0, 0, 0))
    st_shape = jax.ShapeDtypeStruct((b, H_C, DK_C, DV_C), F32)
    return pl.pallas_call(
        functools.partial(_gdn_kernel, nb=nb),
        name="gdn_scan",
        grid=(b, nb),
        in_specs=[fwd, fwd, fwd, gfwd, fwd, bwd, bwd, bwd, gbwd, bwd,
                  st, st,
                  pl.BlockSpec((8, GDN_SUB, GDN_SUB), lambda bb, i: (0, 0, 0)),
                  pl.BlockSpec((2, GDN_BLOCK, GDN_BLOCK), lambda bb, i: (0, 0, 0)),
                  _layer_spec((1, DV_C), l)],
        out_specs=[pl.BlockSpec((None, n, D_CV), lambda bb, i: (bb, 0, 0)), st, st],
        out_shape=[jax.ShapeDtypeStruct((b, n, D_CV), BF16), st_shape, st_shape],
        scratch_shapes=[pltpu.VMEM((n, D_CV), F32), pltpu.VMEM((H_C, DK_C, DV_C), F32),
                        pltpu.VMEM((H_C, DK_C, DV_C), F32)],
        compiler_params=_cparams("parallel", "arbitrary"),
    )(q, k, v, gb, og, q, k, v, gb, og, s0f, s0b, masks, cums, gain3)


def _merge_kernel(h_ref, ya_ref, yb_ref, yc_ref, wga_ref, wgb_ref, wgc_ref, wbr_ref, z_ref):
    h = h_ref[...]
    gates = [_dot(h, w_ref[...]) for w_ref in (wga_ref, wgb_ref, wgc_ref)]
    vals = [_dot(y_ref[...], wbr_ref[br]) for br, y_ref in enumerate((ya_ref, yb_ref, yc_ref))]
    z = None
    for gt, vl in zip(gates, vals):
        t = _sigmoid(gt) * vl
        z = t if z is None else z + t
    z_ref[...] = z.astype(BF16)


def _merge(h, ya, yb, yc, wgates, wbr4, l, tm, tn):
    b, n, d = h.shape
    gate_spec = lambda br: pl.BlockSpec((None, d, tn), lambda bb, i, j: (l, 0, br * (d // tn) + j))
    yt = pl.BlockSpec((None, tm, D_A), lambda bb, i, j: (bb, i, 0))
    return pl.pallas_call(
        _merge_kernel,
        name="merge_gate",
        grid=(b, n // tm, d // tn),
        in_specs=[
            pl.BlockSpec((None, tm, d), lambda bb, i, j: (bb, i, 0)), yt, yt, yt,
            gate_spec(0), gate_spec(1), gate_spec(2),
            pl.BlockSpec((None, 3, D_A, tn), lambda bb, i, j: (l, 0, 0, j)),
        ],
        out_specs=pl.BlockSpec((None, tm, tn), lambda bb, i, j: (bb, i, j)),
        out_shape=jax.ShapeDtypeStruct((b, n, d), BF16),
        compiler_params=_cparams("parallel", "parallel", "parallel"),
    )(h, ya, yb, yc, wgates, wgates, wgates, wbr4)


def _resid_kernel(x_ref, a_ref, gt_ref, w_ref, *rest, with_norm):
    if with_norm:
        g_ref, sh_ref, sc_ref, o_ref, h_ref = rest
    else:
        (o_ref,) = rest
    tm = x_ref.shape[0]
    nhalf = 2 if tm % (2 * HALO) == 0 else 1
    rh = tm // nhalf

    def finish(c, acc):
        rows = slice(c * rh, (c + 1) * rh)
        xn = x_ref[rows, :] + gt_ref[...] * acc
        if with_norm:
            h_ref[rows, :] = _norm_mod(xn, g_ref[...], sh_ref[...], sc_ref[...]).astype(BF16)
        o_ref[rows, :] = xn

    accs = [_dot(a_ref[0:rh, :], w_ref[...])]
    for c in range(1, nhalf):
        accs.append(_dot(a_ref[c * rh:(c + 1) * rh, :], w_ref[...]))
        finish(c - 1, accs[c - 1])
    finish(nhalf - 1, accs[-1])


def _resid_mm(x, a, mods, l, row, j_gate, w3, norm, tm):
    b, n, d = x.shape
    kdim = a.shape[-1]
    xt = pl.BlockSpec((None, tm, d), lambda bb, i: (bb, i, 0))
    in_specs = [xt, pl.BlockSpec((None, tm, kdim), lambda bb, i: (bb, i, 0)), _mod_spec(d, l, j_gate, row),
                _layer_spec((kdim, d), l)]
    args = [x, a, mods, w3]
    out_specs, out_shape = [xt], [jax.ShapeDtypeStruct((b, n, d), F32)]
    if norm is not None:
        g3, ln, j_shift = norm
        in_specs += [_layer_spec((1, d), ln), _mod_spec(d, ln, j_shift, row), _mod_spec(d, ln, j_shift + 1, row)]
        args += [g3, mods, mods]
        out_specs.append(xt)
        out_shape.append(jax.ShapeDtypeStruct((b, n, d), BF16))
    res = pl.pallas_call(
        functools.partial(_resid_kernel, with_norm=norm is not None),
        name="resid_mm",
        grid=(b, n // tm),
        in_specs=in_specs,
        out_specs=out_specs,
        out_shape=out_shape,
        compiler_params=_cparams("parallel", "parallel"),
    )(*args)
    return (res[0], res[1]) if norm is not None else (res[0], None)


def _ffn_up_kernel(hp_ref, h_ref, hn_ref, wup_ref, cw_ref, cb_ref, o_ref, hs_ref, ug_ref, uv_ref, *, tm, tf):
    i = pl.program_id(1)
    last = pl.num_programs(1) - 1
    pad = FFN_CONV // 2
    n_ext = tm + 2 * HALO
    nf = D_FF // tf
    hs_ref[0:HALO] = jnp.where(i > 0, hp_ref[...], jnp.zeros_like(hp_ref))
    hs_ref[HALO:HALO + tm] = h_ref[...]
    hs_ref[HALO + tm:n_ext] = jnp.where(i < last, hn_ref[...], jnp.zeros_like(hn_ref))
    hs = hs_ref[...]

    def project(j):
        slot = j % 2
        ug_ref[slot] = _dot(hs, wup_ref[:, j * tf:(j + 1) * tf])
        uv_ref[slot] = _dot(hs, wup_ref[:, D_FF + j * tf:D_FF + (j + 1) * tf])

    def conv(j):
        slot = j % 2
        ug = ug_ref[slot]
        uv = uv_ref[slot]
        gc = slice(j * tf, (j + 1) * tf)
        vc = slice(D_FF + j * tf, D_FF + (j + 1) * tf)
        cg = cb_ref[:, gc]
        cv = cb_ref[:, vc]
        for kk in range(FFN_CONV):
            sh = (pad - kk) % n_ext
            ugk = ug if sh == 0 else pltpu.roll(ug, sh, axis=0)
            uvk = uv if sh == 0 else pltpu.roll(uv, sh, axis=0)
            cg = cg + ugk[HALO:HALO + tm] * cw_ref[kk:kk + 1, gc]
            cv = cv + uvk[HALO:HALO + tm] * cw_ref[kk:kk + 1, vc]
        o_ref[:, gc] = (_silu(cg) * cv).astype(BF16)

    project(0)
    for j in range(1, nf):
        project(j)
        conv(j - 1)
    conv(nf - 1)


def _ffn_up(h, wup3, cw3, cb3, l, tm, tf):
    b, n, d = h.shape
    prev, nxt = _halo_specs(tm, n, d)
    const = lambda shape: _layer_spec(shape, l, pipeline_mode=pl.Buffered(1))
    return pl.pallas_call(
        functools.partial(_ffn_up_kernel, tm=tm, tf=tf),
        name="ffn_up",
        grid=(b, n // tm),
        in_specs=[
            prev, pl.BlockSpec((None, tm, d), lambda bb, i: (bb, i, 0)), nxt,
            const((d, 2 * D_FF)), const((FFN_CONV, 2 * D_FF)), const((1, 2 * D_FF)),
        ],
        out_specs=pl.BlockSpec((None, tm, D_FF), lambda bb, i: (bb, i, 0)),
        out_shape=jax.ShapeDtypeStruct((b, n, D_FF), BF16),
        scratch_shapes=[pltpu.VMEM((tm + 2 * HALO, d), BF16),
                        pltpu.VMEM((2, tm + 2 * HALO, tf), F32), pltpu.VMEM((2, tm + 2 * HALO, tf), F32)],
        compiler_params=_cparams("parallel", "parallel"),
    )(h, h, h, wup3, cw3, cb3)


def _rope_tables(n_tok):
    t = jnp.arange(n_tok)
    row = (t // GRID_W).astype(F32)
    col = (t % GRID_W).astype(F32)
    n_freq = DK_C // 4
    inv = jnp.power(ROPE_BASE, -jnp.arange(n_freq, dtype=F32) / n_freq)
    ar = row[:, None] * inv
    ac = col[:, None] * inv
    cos = jnp.concatenate([jnp.cos(ar), jnp.cos(ar), jnp.cos(ac), jnp.cos(ac)], axis=-1)
    sin = jnp.concatenate([-jnp.sin(ar), jnp.sin(ar), -jnp.sin(ac), jnp.sin(ac)], axis=-1)
    return cos, sin


def _block_ones(n, blk):
    idx = np.arange(n) // blk
    return jnp.asarray((idx[:, None] == idx[None, :]).astype(np.float32), dtype=BF16)


def kernel(x, c, ctx, c_ctx, ada_w, ada_b, norm1_g, norm2_g, w_in, conv_a_w, conv_a_b, ln_a_g, ln_a_b, qn_g, kn_g,
           rpb, conv_c_w, a_log, dt_bias, onorm_g, w_branch, w_out, ffn_up, ffn_conv_w, ffn_conv_b, ffn_down):
    batch, n_lat, d = x.shape
    n_ctx = ctx.shape[1]
    depth = ada_w.shape[0]

    cc = jnp.zeros((8, d), F32).at[:batch].set(c).at[batch].set(c_ctx)
    mods = _ada(cc, ada_w, ada_b).reshape(depth, 8, 6, 1, d)
    ctx_row = batch

    rope_c, rope_s = _rope_tables(n_lat)
    ones_c = jnp.ones((n_ctx, DK_C), F32)
    zeros_c = jnp.zeros((n_ctx, DK_C), F32)
    gm64 = _block_ones(D_B, DH_B)
    gm128 = _block_ones(D_CQK, DK_C)
    masks = jnp.asarray(_chunk_masks(GDN_SUB))
    blk_masks = _chunk_masks(GDN_BLOCK)
    cums = jnp.asarray(np.stack([blk_masks[0].T, blk_masks[2].T]), dtype=BF16)
    s_zero = jnp.zeros((batch, H_C, DK_C, DV_C), F32)

    off_g = 2 * D_A + 3 * D_B + 3 * D_CQK + D_CV + 4 * H_C
    w_bf = w_in.astype(BF16)
    wgates = w_bf[:, :, off_g:]
    wbr4 = w_branch.astype(BF16)
    wo3 = w_out.astype(BF16)
    wup3 = ffn_up.astype(BF16)
    wdn3 = ffn_down.astype(BF16)
    g1_3 = norm1_g.reshape(depth, 1, d)
    g2_3 = norm2_g.reshape(depth, 1, d)
    gq3 = (jnp.tile(qn_g, (1, H_B)) * (DH_B ** -0.5 * LOG2E)).reshape(depth, 1, D_B)
    gk3 = jnp.tile(kn_g, (1, H_B)).reshape(depth, 1, D_B)
    gbp3 = jnp.zeros((depth, 8, DK_C), F32)
    gbp3 = gbp3.at[:, 0, :2 * H_C].set(a_log.reshape(depth, 2 * H_C)).at[:, 1, :2 * H_C].set(
        dt_bias.reshape(depth, 2 * H_C)).at[:, 2, :2 * H_C].set(1.0).at[:, 3, 2 * H_C:4 * H_C].set(1.0)
    bias5 = _na_bias_table(rpb)
    cb_a3 = conv_a_b.reshape(depth, 1, D_A)
    lg_a3 = ln_a_g.reshape(depth, 1, D_A)
    lb_a3 = ln_a_b.reshape(depth, 1, D_A)
    cbf3 = ffn_conv_b.reshape(depth, 1, 2 * D_FF)
    gain3 = onorm_g.reshape(depth, 1, DV_C)

    tl = _tiles(n_lat)
    tc = _tiles(n_ctx)
    x_lat, x_ctx = x, ctx
    h_lat = _normmod(x_lat, g1_3, mods, 0, None, 0, tl["norm"])
    h_ctx = _normmod(x_ctx, g1_3, mods, 0, ctx_row, 0, tc["norm"])
    for l in range(depth):
        ctx_out = l < depth - 1
        next_norm = (g1_3, l + 1, 0) if ctx_out else None

        qb_c, kb_c, vb_c = _proj_b(h_ctx, w_bf, gq3, gk3, gm64, l, tc["proj"])
        qc_c, kc_c, vc_c, og_c, gb_c = _proj_c(h_ctx, w_bf, conv_c_w, ones_c, zeros_c, gm128, gbp3, l,
                                               tc["proj"], False)
        yc_c, sf_c, sb_c = _gdn(qc_c, kc_c, vc_c, gb_c, og_c, s_zero, s_zero, masks, cums, gain3, l)

        ya_l = _conv_a(_proj_a(h_lat, w_bf, l, tl["proj"]), conv_a_w, cb_a3, lg_a3, lb_a3, l, tl["conv_a"])
        qb_l, kb_l, vb_l = _proj_b(h_lat, w_bf, gq3, gk3, gm64, l, tl["proj"])
        yb_l = _na_attention(qb_l, kb_l, vb_l, kb_c, vb_c, bias5, l, NA_ROWS_PER_STEP)
        qc_l, kc_l, vc_l, og_l, gb_l = _proj_c(h_lat, w_bf, conv_c_w, rope_c, rope_s, gm128, gbp3, l,
                                               tl["proj"], True)
        yc_l, _, _ = _gdn(qc_l, kc_l, vc_l, gb_l, og_l, sf_c, sb_c, masks, cums, gain3, l)
        z_l = _merge(h_lat, ya_l, yb_l, yc_l, wgates, wbr4, l, tl["merge"], MERGE_TN)
        x_lat, h2_l = _resid_mm(x_lat, z_l, mods, l, None, 2, wo3, (g2_3, l, 3), tl["resid"])
        act_l = _ffn_up(h2_l, wup3, ffn_conv_w, cbf3, l, tl["ffn"], FFN_TF)
        x_lat, h_lat = _resid_mm(x_lat, act_l, mods, l, None, 5, wdn3, next_norm, tl["resid"])

        if ctx_out:
            ya_c = _conv_a(_proj_a(h_ctx, w_bf, l, tc["proj"]), conv_a_w, cb_a3, lg_a3, lb_a3, l, tc["conv_a"])
            yb_c = _ctx_attention(qb_c, kb_c, vb_c)
            z_c = _merge(h_ctx, ya_c, yb_c, yc_c, wgates, wbr4, l, tc["merge"], MERGE_TN)
            x_ctx, h2_c = _resid_mm(x_ctx, z_c, mods, l, ctx_row, 2, wo3, (g2_3, l, 3), tc["resid"])
            act_c = _ffn_up(h2_c, wup3, ffn_conv_w, cbf3, l, tc["ffn"], FFN_TF)
            x_ctx, h_ctx = _resid_mm(x_ctx, act_c, mods, l, ctx_row, 5, wdn3, next_norm, tc["resid"])
    return x_lat
```

```python
import functools
import math

import numpy as np
import jax
import jax.numpy as jnp
from jax import lax
from jax.experimental import pallas as pl
from jax.experimental.pallas import tpu as pltpu

F32 = jnp.float32
BF16 = jnp.bfloat16

D_MODEL = 1024
GRID_W = 64
EPS = 1e-6
NEG_INF = -1e30
LOG2E = math.log2(math.e)
D_A = 512
CONV_A = 31
H_B = 8
DH_B = 64
D_B = H_B * DH_B
NA_ROWS = 8
NA_COLS = 16
H_C = 4
DK_C = 128
DV_C = 128
D_CQK = H_C * DK_C
D_CV = H_C * DV_C
SHORT_CONV = 4
CHUNK = 64
ROPE_BASE = 10000.0
D_FF = 2816
FFN_CONV = 3

VMEM_LIMIT_BYTES = 48 * 1024 * 1024
HALO = 16
SUBLANES = 8
GDN_BLOCK = 256
GDN_SUB = 128
NA_ROWS_PER_STEP = 32
HEADS_PER_STEP = 2
MERGE_TN = 1024
FFN_TF = 256


def _tiles(n_tok):
    cap = lambda t: min(t, n_tok)
    return {"norm": cap(1024), "proj": cap(1024), "conv_a": cap(512), "merge": cap(1024), "resid": cap(1024),
            "ffn": cap(1024)}


def _cparams(*sem):
    return pltpu.CompilerParams(dimension_semantics=sem, vmem_limit_bytes=VMEM_LIMIT_BYTES)


def _dot(a, b):
    return jnp.dot(a, b, preferred_element_type=F32)


def _dot_nt(a, b):
    return lax.dot_general(a, b, (((1,), (1,)), ((), ())), preferred_element_type=F32)


def _split_bf16(x, n):
    parts = []
    r = x
    for idx in range(n):
        p = r.astype(BF16)
        parts.append(p)
        if idx + 1 < n:
            r = r - p.astype(F32)
    return parts


def _dot_exact_rhs(a, b_bf16, n):
    out = None
    for p in _split_bf16(a, n):
        t = _dot(p, b_bf16)
        out = t if out is None else out + t
    return out


def _dot_exact_lhs(a_bf16, b, n):
    out = None
    for p in _split_bf16(b, n):
        t = _dot(a_bf16, p)
        out = t if out is None else out + t
    return out


def _mm(a, b, passes=1):
    if passes == 1:
        return _dot(a.astype(BF16), b.astype(BF16))
    a_hi, a_lo = _split_bf16(a, 2)
    b_hi, b_lo = _split_bf16(b, 2)
    return _dot(a_hi, b_hi) + (_dot(a_lo, b_hi) + _dot(a_hi, b_lo))


def _sigmoid(x):
    return 1.0 / (1.0 + jnp.exp(-x))


def _silu(x):
    return x * _sigmoid(x)


def _softplus(x):
    return jnp.maximum(x, 0.0) + jnp.log(1.0 + jnp.exp(-jnp.abs(x)))


def _norm_mod(x, g, shift, scale):
    ms = jnp.mean(x * x, axis=-1, keepdims=True)
    y = x * lax.rsqrt(ms + EPS) * g
    return y * (1.0 + scale) + shift


def _layer_spec(shape, l, **kw):
    return pl.BlockSpec((None,) + tuple(shape), lambda *_: (l,) + (0,) * len(shape), **kw)


def _wcols_spec(rows, width, l, col_block):
    return pl.BlockSpec((None, rows, width), lambda *_: (l, 0, col_block))


def _mod_spec(d, l, j, row):
    if row is None:
        return pl.BlockSpec((None, None, None, 1, d), lambda bb, *_: (l, bb, j, 0, 0))
    return pl.BlockSpec((None, None, None, 1, d), lambda *_: (l, row, j, 0, 0))


def _halo_specs(tm, n_tok, width):
    per = tm // HALO
    nblk = n_tok // HALO
    prev = pl.BlockSpec((None, HALO, width), lambda b, i, *_: (b, jnp.maximum(i * per - 1, 0), 0))
    nxt = pl.BlockSpec((None, HALO, width), lambda b, i, *_: (b, jnp.minimum((i + 1) * per, nblk - 1), 0))
    return prev, nxt


def _ada_kernel(c_ref, w_ref, b_ref, o_ref):
    s = _silu(c_ref[...])
    o_ref[...] = _mm(s, w_ref[...], 3) + b_ref[...]


def _ada(cc, ada_w, ada_b):
    depth = ada_w.shape[0]
    d = D_MODEL
    return pl.pallas_call(
        _ada_kernel,
        name="ada_mod",
        grid=(depth, 6),
        in_specs=[
            pl.BlockSpec((8, d), lambda l, j: (0, 0)),
            pl.BlockSpec((None, d, d), lambda l, j: (l, 0, j)),
            pl.BlockSpec((None, 1, d), lambda l, j: (l, 0, j)),
        ],
        out_specs=pl.BlockSpec((None, 8, d), lambda l, j: (l, 0, j)),
        out_shape=jax.ShapeDtypeStruct((depth, 8, 6 * d), F32),
        compiler_params=_cparams("arbitrary", "arbitrary"),
    )(cc, ada_w, ada_b.reshape(depth, 1, 6 * d))


def _normmod_kernel(x_ref, g_ref, sh_ref, sc_ref, o_ref):
    o_ref[...] = _norm_mod(x_ref[...], g_ref[...], sh_ref[...], sc_ref[...]).astype(BF16)


def _normmod(x, g3, mods, l, row, j_shift, tm):
    b, n, d = x.shape
    tile = pl.BlockSpec((None, tm, d), lambda bb, i: (bb, i, 0))
    return pl.pallas_call(
        _normmod_kernel,
        name="norm_mod",
        grid=(b, n // tm),
        in_specs=[tile, _layer_spec((1, d), l), _mod_spec(d, l, j_shift, row), _mod_spec(d, l, j_shift + 1, row)],
        out_specs=tile,
        out_shape=jax.ShapeDtypeStruct((b, n, d), BF16),
        compiler_params=_cparams("parallel", "parallel"),
    )(x, g3, mods, mods)


def _proj_a_kernel(h_ref, wa_ref, wg_ref, o_ref):
    h = h_ref[...]
    wc = D_A // 2

    def project(c):
        cols = slice(c * wc, (c + 1) * wc)
        return _dot(h, wa_ref[:, cols]), _dot(h, wg_ref[:, cols])

    first = project(0)
    second = project(1)
    o_ref[:, 0:wc] = first[0] * _sigmoid(first[1])
    o_ref[:, wc:2 * wc] = second[0] * _sigmoid(second[1])


def _proj_a(h, w_bf, l, tm):
    b, n, d = h.shape
    return pl.pallas_call(
        _proj_a_kernel,
        name="proj_a",
        grid=(b, n // tm),
        in_specs=[
            pl.BlockSpec((None, tm, d), lambda bb, i: (bb, i, 0)),
            _wcols_spec(d, D_A, l, 0), _wcols_spec(d, D_A, l, 1),
        ],
        out_specs=pl.BlockSpec((None, tm, D_A), lambda bb, i: (bb, i, 0)),
        out_shape=jax.ShapeDtypeStruct((b, n, D_A), F32),
        compiler_params=_cparams("parallel", "parallel"),
    )(h, w_bf, w_bf)


def _conv_a_kernel(yp_ref, y_ref, yn_ref, cw_ref, cb_ref, lg_ref, lb_ref, o_ref, ext_ref, *, tm):
    i = pl.program_id(1)
    last = pl.num_programs(1) - 1
    ext_ref[0, 0:HALO] = yp_ref[...] * (i > 0).astype(F32)
    ext_ref[0, HALO:HALO + tm] = y_ref[...]
    ext_ref[0, HALO + tm:2 * HALO + tm] = yn_ref[...] * (i < last).astype(F32)
    n_ext = tm + 2 * HALO
    ext0 = ext_ref[0]
    for r in range(1, SUBLANES):
        ext_ref[r] = pltpu.roll(ext0, n_ext - r, axis=0)
    pad = CONV_A // 2
    acc = jnp.zeros((tm, D_A), F32) + cb_ref[...]
    for k in range(CONV_A):
        off = HALO - pad + k
        r = off % SUBLANES
        acc = acc + ext_ref[r, pl.ds(off - r, tm), :] * cw_ref[k:k + 1, :]
    mu = jnp.mean(acc, axis=-1, keepdims=True)
    cen = acc - mu
    var = jnp.mean(cen * cen, axis=-1, keepdims=True)
    y = cen * lax.rsqrt(var + EPS) * lg_ref[...] + lb_ref[...]
    o_ref[...] = _silu(y).astype(BF16)


def _conv_a(y, cw3, cb3, lg3, lb3, l, tm):
    b, n, c = y.shape
    prev, nxt = _halo_specs(tm, n, c)
    vec = _layer_spec((1, c), l)
    return pl.pallas_call(
        functools.partial(_conv_a_kernel, tm=tm),
        name="conv_a",
        grid=(b, n // tm),
        in_specs=[prev, pl.BlockSpec((None, tm, c), lambda bb, i: (bb, i, 0)), nxt,
                  _layer_spec((CONV_A, c), l), vec, vec, vec],
        out_specs=pl.BlockSpec((None, tm, c), lambda bb, i: (bb, i, 0)),
        out_shape=jax.ShapeDtypeStruct((b, n, c), BF16),
        scratch_shapes=[pltpu.VMEM((SUBLANES, tm + 2 * HALO, c), F32)],
        compiler_params=_cparams("parallel", "parallel"),
    )(y, y, y, cw3, cb3, lg3, lb3)


def _proj_b_kernel(h_ref, wq_ref, wk_ref, wv_ref, gq_ref, gk_ref, gm_ref, q_ref, k_ref, v_ref):
    h = h_ref[...]
    for w_ref, gain_ref, o_ref in ((wq_ref, gq_ref, q_ref), (wk_ref, gk_ref, k_ref), (wv_ref, None, v_ref)):
        acc = _dot(h, w_ref[...])
        if gain_ref is not None:
            ss = _dot((acc * acc).astype(BF16), gm_ref[...])
            acc = acc * lax.rsqrt(ss * (1.0 / DH_B) + EPS) * gain_ref[...]
        for hh in range(H_B):
            o_ref[hh] = acc[:, hh * DH_B:(hh + 1) * DH_B].astype(BF16)


def _proj_b(h, w_bf, gq3, gk3, gmat, l, tm):
    b, n, d = h.shape
    vec = _layer_spec((1, D_B), l)
    first = 2 * D_A // D_B
    head_out = pl.BlockSpec((None, H_B, tm, DH_B), lambda bb, i: (bb, 0, i, 0))
    shp = jax.ShapeDtypeStruct((b, H_B, n, DH_B), BF16)
    return pl.pallas_call(
        _proj_b_kernel,
        name="proj_b",
        grid=(b, n // tm),
        in_specs=[
            pl.BlockSpec((None, tm, d), lambda bb, i: (bb, i, 0)),
            _wcols_spec(d, D_B, l, first), _wcols_spec(d, D_B, l, first + 1), _wcols_spec(d, D_B, l, first + 2),
            vec, vec,
            pl.BlockSpec((D_B, D_B), lambda bb, i: (0, 0)),
        ],
        out_specs=[head_out, head_out, head_out],
        out_shape=[shp, shp, shp],
        compiler_params=_cparams("parallel", "parallel"),
    )(h, w_bf, w_bf, w_bf, gq3, gk3, gmat)


def _na_kernel(q_ref, k_ref, v_ref, kc_ref, vc_ref, bias_ref, o_ref, *, rb, rows):
    i = pl.program_id(2)
    n_loc = NA_ROWS * GRID_W
    rws = [i * rb + rr for rr in range(rb)]
    starts = [jnp.clip(r - NA_ROWS // 2, 0, rows - NA_ROWS) for r in rws]
    variants = [st - r + NA_ROWS - 1 for st, r in zip(starts, rws)]
    tok0 = [pl.multiple_of(st * GRID_W, GRID_W) for st in starts]
    probs = [(hh, rr) for hh in range(HEADS_PER_STEP) for rr in range(rb)]
    qs = [q_ref[hh, rr * GRID_W:(rr + 1) * GRID_W, :] for hh, rr in probs]
    s = [_dot_nt(q, k_ref[hh, pl.ds(tok0[rr], n_loc), :]) for q, (hh, rr) in zip(qs, probs)]
    sc = [_dot_nt(q, kc_ref[hh]) for q, (hh, rr) in zip(qs, probs)]
    bias = {(hh, rr): jnp.concatenate([bias_ref[hh, variants[rr] + kk] for kk in range(0, NA_ROWS, 2)], axis=1)
            for hh, rr in probs}
    s = [a + bias[pr] for a, pr in zip(s, probs)]
    m = [jnp.maximum(jnp.max(a, axis=-1, keepdims=True), jnp.max(b, axis=-1, keepdims=True)) for a, b in zip(s, sc)]
    p = [jnp.exp2(a - mm) for a, mm in zip(s, m)]
    pc = [jnp.exp2(b - mm) for b, mm in zip(sc, m)]
    l = [jnp.sum(a, axis=-1, keepdims=True) + jnp.sum(b, axis=-1, keepdims=True) for a, b in zip(p, pc)]
    o = [_dot(a.astype(BF16), v_ref[hh, pl.ds(tok0[rr], n_loc), :]) + _dot(b.astype(BF16), vc_ref[hh])
         for a, b, (hh, rr) in zip(p, pc, probs)]
    o = [a / b for a, b in zip(o, l)]
    for rr in range(rb):
        o_ref[rr * GRID_W:(rr + 1) * GRID_W, :] = jnp.concatenate(
            [o[probs.index((hh, rr))] for hh in range(HEADS_PER_STEP)], axis=1).astype(BF16)


def _na_attention(q, k, v, kc, vc, bias, l, rb):
    b, h, t, dh = q.shape
    n_ctx = kc.shape[2]
    rows = t // GRID_W
    hs = HEADS_PER_STEP
    full = pl.BlockSpec((None, hs, t, dh), lambda bb, hp, i: (bb, hp, 0, 0))
    cfull = pl.BlockSpec((None, hs, n_ctx, dh), lambda bb, hp, i: (bb, hp, 0, 0))
    return pl.pallas_call(
        functools.partial(_na_kernel, rb=rb, rows=rows),
        name="na_attn",
        grid=(b, h // hs, rows // rb),
        in_specs=[pl.BlockSpec((None, hs, rb * GRID_W, dh), lambda bb, hp, i: (bb, hp, i, 0)),
                  full, full, cfull, cfull,
                  pl.BlockSpec((None, hs, 2 * NA_ROWS - 2, GRID_W, 2 * GRID_W),
                               lambda bb, hp, i: (l, hp, 0, 0, 0))],
        out_specs=pl.BlockSpec((None, rb * GRID_W, hs * dh), lambda bb, hp, i: (bb, i, hp)),
        out_shape=jax.ShapeDtypeStruct((b, t, h * dh), BF16),
        compiler_params=_cparams("parallel", "parallel", "arbitrary"),
    )(q, k, v, kc, vc, bias)


def _ctx_attn_kernel(q_ref, k_ref, v_ref, o_ref):
    outs = []
    for hh in range(HEADS_PER_STEP):
        s = _dot_nt(q_ref[hh], k_ref[hh])
        m = jnp.max(s, axis=-1, keepdims=True)
        p = jnp.exp2(s - m)
        l = jnp.sum(p, axis=-1, keepdims=True)
        outs.append(_dot(p.astype(BF16), v_ref[hh]) / l)
    o_ref[...] = jnp.concatenate(outs, axis=1).astype(BF16)


def _ctx_attention(q, k, v):
    b, h, n, dh = q.shape
    hs = HEADS_PER_STEP
    full = pl.BlockSpec((None, hs, n, dh), lambda bb, hp: (bb, hp, 0, 0))
    return pl.pallas_call(
        _ctx_attn_kernel,
        name="ctx_attn",
        grid=(b, h // hs),
        in_specs=[full, full, full],
        out_specs=pl.BlockSpec((None, n, hs * dh), lambda bb, hp: (bb, 0, hp)),
        out_shape=jax.ShapeDtypeStruct((b, n, h * dh), BF16),
        compiler_params=_cparams("parallel", "parallel"),
    )(q, k, v)


def _na_bias_table(rpb):
    cidx = np.arange(GRID_W)
    dc = np.clip(cidx[None, :] - cidx[:, None] + NA_COLS - 1, 0, 2 * NA_COLS - 2)
    onehot = (dc[None] == np.arange(2 * NA_COLS - 1)[:, None, None]).astype(np.float32)
    cs = np.clip(cidx - NA_COLS // 2, 0, GRID_W - NA_COLS)
    col_ok = (cidx[None, :] >= cs[:, None]) & (cidx[None, :] < cs[:, None] + NA_COLS)
    toep = jnp.einsum("lhrd,dqk->lhrqk", rpb, onehot, precision=lax.Precision.HIGHEST)
    toep = jnp.where(col_ok, toep * LOG2E, NEG_INF)
    return jnp.concatenate([toep[:, :, :-1], toep[:, :, 1:]], axis=-1)


def _proj_c_kernel(hp_ref, h_ref, hn_ref, wq_ref, wk_ref, wv_ref, wo_ref, wdb_ref, cw_ref, rc_ref, rs_ref,
                   gm_ref, gbp_ref, q_ref, k_ref, v_ref, og_ref, gb_ref, hs_ref, p_ref, *, tm, use_rope):
    i = pl.program_id(1)
    last = pl.num_programs(1) - 1
    pad_l = SHORT_CONV // 2
    n_ext = tm + 2 * HALO
    hs_ref[0:HALO] = jnp.where(i > 0, hp_ref[...], jnp.zeros_like(hp_ref))
    hs_ref[HALO:HALO + tm] = h_ref[...]
    hs_ref[HALO + tm:n_ext] = jnp.where(i < last, hn_ref[...], jnp.zeros_like(hn_ref))
    hs = hs_ref[...]
    hc = h_ref[...]
    lane = lax.broadcasted_iota(jnp.int32, (tm, DK_C), 1)
    first_half = (lane & (DK_C // 2 - 1)) < DK_C // 4
    sections = ((wq_ref, q_ref), (wk_ref, k_ref), (wv_ref, v_ref))

    def project(sec):
        p_ref[sec % 2] = _dot(hs, sections[sec][0][...])

    def finish(sec):
        o_ref = sections[sec][1]
        cols = slice(sec * D_CQK, (sec + 1) * D_CQK)
        pv = p_ref[sec % 2]
        y = None
        for kk in range(SHORT_CONV):
            sh = (pad_l - kk) % n_ext
            pk = pv if sh == 0 else pltpu.roll(pv, sh, axis=0)
            t = pk[HALO:HALO + tm] * cw_ref[kk:kk + 1, cols]
            y = t if y is None else y + t
        y = _silu(y)
        if sec < 2:
            ss = _dot((y * y).astype(BF16), gm_ref[...])
            y = y * lax.rsqrt(ss + EPS)
            if use_rope:
                heads = []
                for hh in range(H_C):
                    yh = y[:, hh * DK_C:(hh + 1) * DK_C]
                    swapped = jnp.where(first_half, pltpu.roll(yh, DK_C - DK_C // 4, axis=1),
                                        pltpu.roll(yh, DK_C // 4, axis=1))
                    heads.append(yh * rc_ref[...] + swapped * rs_ref[...])
                y = jnp.concatenate(heads, axis=1)
            if sec == 0:
                y = y * (DK_C ** -0.5)
        o_ref[...] = y

    project(0)
    for sec in range(1, len(sections)):
        project(sec)
        finish(sec - 1)
    og_ref[...] = _dot(hc, wo_ref[...])
    finish(len(sections) - 1)
    db = _dot(hc, wdb_ref[...])
    gval = -jnp.exp(gbp_ref[0:1, :]) * _softplus(db + gbp_ref[1:2, :])
    gb_ref[...] = gbp_ref[2:3, :] * gval + gbp_ref[3:4, :] * _sigmoid(db)


def _proj_c(h, w_bf, cw3, rc, rs, gmat, gbp3, l, tm, use_rope):
    b, n, d = h.shape
    first = (2 * D_A + 3 * D_B) // D_CQK
    db_block = (2 * D_A + 3 * D_B + 3 * D_CQK + D_CV) // DK_C
    prev, nxt = _halo_specs(tm, n, d)
    tile512 = pl.BlockSpec((None, tm, D_CQK), lambda bb, i: (bb, i, 0))
    shp = jax.ShapeDtypeStruct((b, n, D_CQK), F32)
    rope_spec = pl.BlockSpec((tm, DK_C), lambda bb, i: (i, 0))
    return pl.pallas_call(
        functools.partial(_proj_c_kernel, tm=tm, use_rope=use_rope),
        name="proj_c",
        grid=(b, n // tm),
        in_specs=[
            prev, pl.BlockSpec((None, tm, d), lambda bb, i: (bb, i, 0)), nxt,
            _wcols_spec(d, D_CQK, l, first), _wcols_spec(d, D_CQK, l, first + 1),
            _wcols_spec(d, D_CQK, l, first + 2), _wcols_spec(d, D_CQK, l, first + 3),
            _wcols_spec(d, DK_C, l, db_block),
            _layer_spec((SHORT_CONV, 3 * D_CQK), l),
            rope_spec, rope_spec,
            pl.BlockSpec((D_CQK, D_CQK), lambda bb, i: (0, 0)),
            _layer_spec((8, DK_C), l),
        ],
        out_specs=[tile512] * 4 + [pl.BlockSpec((None, tm, DK_C), lambda bb, i: (bb, i, 0))],
        out_shape=[shp] * 4 + [jax.ShapeDtypeStruct((b, n, DK_C), F32)],
        scratch_shapes=[pltpu.VMEM((tm + 2 * HALO, d), BF16), pltpu.VMEM((2, tm + 2 * HALO, D_CQK), F32)],
        compiler_params=_cparams("parallel", "parallel"),
    )(h, h, h, w_bf, w_bf, w_bf, w_bf, w_bf, cw3, rc, rs, gmat, gbp3)


def _chunk_masks(n):
    r = np.arange(n)[:, None]
    c = np.arange(n)[None, :]
    same = lambda s: (r // s) == (c // s)
    chunk = same(CHUNK)
    ms = [chunk & (r >= c), chunk & (r > c), chunk & (r <= c), chunk & (r < c), same(8),
          same(16) & ~same(8), same(32) & ~same(16), same(64) & ~same(32)]
    return np.stack(ms).astype(np.float32)


def _tri_inv_all(lmats, eye, m_ref):
    m8 = m_ref[4]
    n0 = [-(l * m8) for l in lmats]
    n2 = [_mm(a, a) for a in n0]
    n4 = [_mm(a, a) for a in n2]
    n3 = [_mm(a, b) for a, b in zip(n0, n2)]
    t1 = [eye + a + b + c for a, b, c in zip(n0, n2, n3)]
    t1n4 = [_mm(a, b) for a, b in zip(t1, n4)]
    t = [a + b for a, b in zip(t1, t1n4)]
    for lvl in (5, 6, 7):
        off = m_ref[lvl]
        lt = [_mm(l * off, a) for l, a in zip(lmats, t)]
        tlt = [_mm(a, b) for a, b in zip(t, lt)]
        t = [a - b for a, b in zip(t, tlt)]
    return t


def _gdn_step(io, m_ref, cum_ref):
    nsub = GDN_BLOCK // GDN_SUB
    csub = GDN_SUB // CHUNK
    nch = nsub * csub
    r16, cols = [], []
    for d in range(2):
        g_t = io[d][3][...].T[0:4 * H_C]
        csum = _dot_exact_rhs(g_t, cum_ref[d], 3)
        rid = lax.broadcasted_iota(jnp.int32, g_t.shape, 0)
        r = jnp.where(rid // H_C == d, csum, g_t)
        r16.append(r)
        cols.append(jnp.concatenate([r, jnp.zeros((DK_C - 4 * H_C, GDN_BLOCK), F32)], axis=0).T)
    probs = [(d, h, sb) for d in range(2) for h in range(H_C) for sb in range(nsub)]
    incl = [m_ref[0], m_ref[2]]
    strict = [m_ref[1], m_ref[3]]
    eye = incl[0] - strict[0]

    def tile(ref, h, sb):
        return ref[sb * GDN_SUB:(sb + 1) * GDN_SUB, h * DK_C:(h + 1) * DK_C]

    q = [tile(io[d][0], h, sb) for d, h, sb in probs]
    k = [tile(io[d][1], h, sb) for d, h, sb in probs]
    v = [tile(io[d][2], h, sb) for d, h, sb in probs]
    gcol, bcol, decay = [], [], []
    for d, h, sb in probs:
        rows = slice(sb * GDN_SUB, (sb + 1) * GDN_SUB)
        c_g = d * H_C + h
        c_b = 2 * H_C + c_g
        gc = jnp.broadcast_to(cols[d][rows, c_g:c_g + 1], (GDN_SUB, DK_C))
        gcol.append(gc)
        bcol.append(jnp.broadcast_to(cols[d][rows, c_b:c_b + 1], (GDN_SUB, DK_C)))
        decay.append(jnp.exp(jnp.where(incl[d] > 0.5, gc - r16[d][c_g:c_g + 1, rows], NEG_INF)))
    k16 = [a.astype(BF16) for a in k]
    qkk = [_dot_nt(jnp.concatenate([a.astype(BF16), b], axis=0), b) for a, b in zip(q, k16)]
    qk = [a[:GDN_SUB] for a in qkk]
    kk = [a[GDN_SUB:] for a in qkk]
    lmats = [a * b * (c * strict[p[0]]) for a, b, c, p in zip(kk, bcol, decay, probs)]
    amat = [(a * c).astype(BF16) for a, c in zip(qk, decay)]
    tinv = _tri_inv_all(lmats, eye, m_ref)
    eg = [jnp.exp(a) for a in gcol]
    rhs = [jnp.concatenate([vv * b, kx * (b * e)], axis=1).astype(BF16) for vv, kx, b, e in zip(v, k, bcol, eg)]
    sol = [_dot(t.astype(BF16), r) for t, r in zip(tinv, rhs)]
    qd = [a * e for a, e in zip(q, eg)]
    gtot, kd_t = [], []
    for (d, h, sb), kx, gc in zip(probs, k, gcol):
        parts, gts = [], []
        for c in range(csub):
            end = c * CHUNK + (CHUNK - 1 if d == 0 else 0)
            gt = gc[end:end + 1, :]
            gts.append(gt)
            sl = slice(c * CHUNK, (c + 1) * CHUNK)
            parts.append(kx[sl] * jnp.exp(gt - gc[sl]))
        gtot.append(gts)
        kd_t.append(jnp.concatenate(parts, axis=0).T.astype(BF16))
    pidx = {p: n for n, p in enumerate(probs)}
    chains = [(d, h) for d in range(2) for h in range(H_C)]
    state = [io[d][4][h] for d, h in chains]
    vnew = [[None] * nch for _ in chains]
    ointer = [[None] * nch for _ in chains]
    zeros = jnp.zeros((CHUNK, DV_C), BF16)
    for step in range(nch):
        pos = [(step if d == 0 else nch - 1 - step) for d, _ in chains]
        loc = [(pidx[(d, h, n // csub)], n % csub) for (d, h), n in zip(chains, pos)]
        sls = [slice(c * CHUNK, (c + 1) * CHUNK) for _, c in loc]
        wq = [jnp.concatenate([sol[p][sl, DV_C:], qd[p][sl]], axis=0).astype(BF16) for (p, _), sl in zip(loc, sls)]
        r = [_dot(a, s.astype(BF16)) for a, s in zip(wq, state)]
        vn = [(sol[p][sl, :DV_C] - rr[0:CHUNK]).astype(BF16) for (p, _), sl, rr in zip(loc, sls, r)]
        vpad = [jnp.concatenate([x if m == c else zeros for m in range(csub)], axis=0) for x, (_, c) in zip(vn, loc)]
        upd = [_dot(kd_t[p], x) for (p, _), x in zip(loc, vpad)]
        state = [s * jnp.exp(gtot[p][c]) + u for s, (p, c), u in zip(state, loc, upd)]
        for ci, n in enumerate(pos):
            vnew[ci][n] = vn[ci]
            ointer[ci][n] = r[ci][CHUNK:2 * CHUNK]
    for (d, h), s in zip(chains, state):
        io[d][4][h] = s
    out = [[None] * H_C for _ in range(2)]
    for ci, (d, h) in enumerate(chains):
        parts = []
        for sb in range(nsub):
            p = pidx[(d, h, sb)]
            vn_sb = jnp.concatenate(vnew[ci][sb * csub:(sb + 1) * csub], axis=0)
            parts.append(jnp.concatenate(ointer[ci][sb * csub:(sb + 1) * csub], axis=0) + _dot(amat[p], vn_sb))
        out[d][h] = jnp.concatenate(parts, axis=0)
    return out


def _gdn_finalize(o_heads, gate_ref, gain):
    ys = []
    for h, o in enumerate(o_heads):
        ms = jnp.mean(o * o, axis=-1, keepdims=True)
        ys.append(o * lax.rsqrt(ms + EPS) * gain * _silu(gate_ref[:, h * DV_C:(h + 1) * DV_C]))
    return jnp.concatenate(ys, axis=1).astype(BF16)


def _gdn_kernel(qf_ref, kf_ref, vf_ref, gf_ref, ogf_ref, qb_ref, kb_ref, vb_ref, gb_ref, ogb_ref,
                s0f_ref, s0b_ref, m_ref, cum_ref, gain_ref,
                y_ref, sfo_ref, sbo_ref, oacc_ref, sf_ref, sb_ref, *, nb):
    i = pl.program_id(1)

    @pl.when(i == 0)
    def _():
        sf_ref[...] = s0f_ref[...]
        sb_ref[...] = s0b_ref[...]

    of, ob = _gdn_step(((qf_ref, kf_ref, vf_ref, gf_ref, sf_ref), (qb_ref, kb_ref, vb_ref, gb_ref, sb_ref)),
                       m_ref, cum_ref)
    rows_f = pl.ds(pl.multiple_of(i * GDN_BLOCK, GDN_BLOCK), GDN_BLOCK)
    rows_b = pl.ds(pl.multiple_of((nb - 1 - i) * GDN_BLOCK, GDN_BLOCK), GDN_BLOCK)
    gain = gain_ref[...]

    def split(x):
        return [x[:, h * DV_C:(h + 1) * DV_C] for h in range(H_C)]

    @pl.when(2 * i < nb - 1)
    def _():
        oacc_ref[rows_f, :] = jnp.concatenate(of, axis=1)
        oacc_ref[rows_b, :] = jnp.concatenate(ob, axis=1)

    if nb % 2 == 1:
        @pl.when(2 * i == nb - 1)
        def _():
            y_ref[rows_f, :] = _gdn_finalize([a + b for a, b in zip(of, ob)], ogf_ref, gain)

    @pl.when(2 * i > nb - 1)
    def _():
        y_ref[rows_f, :] = _gdn_finalize([a + b for a, b in zip(split(oacc_ref[rows_f, :]), of)], ogf_ref, gain)
        y_ref[rows_b, :] = _gdn_finalize([a + b for a, b in zip(split(oacc_ref[rows_b, :]), ob)], ogb_ref, gain)

    @pl.when(i == nb - 1)
    def _():
        sfo_ref[...] = sf_ref[...]
        sbo_ref[...] = sb_ref[...]


def _gdn(q, k, v, gb, og, s0f, s0b, masks, cums, gain3, l):
    b, n, _ = q.shape
    nb = n // GDN_BLOCK
    fwd = pl.BlockSpec((None, GDN_BLOCK, D_CQK), lambda bb, i: (bb, i, 0))
    bwd = pl.BlockSpec((None, GDN_BLOCK, D_CQK), lambda bb, i: (bb, nb - 1 - i, 0))
    gfwd = pl.BlockSpec((None, GDN_BLOCK, DK_C), lambda bb, i: (bb, i, 0))
    gbwd = pl.BlockSpec((None, GDN_BLOCK, DK_C), lambda bb, i: (bb, nb - 1 - i, 0))
    st = pl.BlockSpec((None, H_C, DK_C, DV_C), lambda bb, i: (bb, 0, 0, 0))
    st_shape = jax.ShapeDtypeStruct((b, H_C, DK_C, DV_C), F32)
    return pl.pallas_call(
        functools.partial(_gdn_kernel, nb=nb),
        name="gdn_scan",
        grid=(b, nb),
        in_specs=[fwd, fwd, fwd, gfwd, fwd, bwd, bwd, bwd, gbwd, bwd,
                  st, st,
                  pl.BlockSpec((8, GDN_SUB, GDN_SUB), lambda bb, i: (0, 0, 0)),
                  pl.BlockSpec((2, GDN_BLOCK, GDN_BLOCK), lambda bb, i: (0, 0, 0)),
                  _layer_spec((1, DV_C), l)],
        out_specs=[pl.BlockSpec((None, n, D_CV), lambda bb, i: (bb, 0, 0)), st, st],
        out_shape=[jax.ShapeDtypeStruct((b, n, D_CV), BF16), st_shape, st_shape],
        scratch_shapes=[pltpu.VMEM((n, D_CV), F32), pltpu.VMEM((H_C, DK_C, DV_C), F32),
                        pltpu.VMEM((H_C, DK_C, DV_C), F32)],
        compiler_params=_cparams("parallel", "arbitrary"),
    )(q, k, v, gb, og, q, k, v, gb, og, s0f, s0b, masks, cums, gain3)


def _merge_kernel(h_ref, ya_ref, yb_ref, yc_ref, wga_ref, wgb_ref, wgc_ref, wbr_ref, z_ref):
    h = h_ref[...]
    gates = [_dot(h, w_ref[...]) for w_ref in (wga_ref, wgb_ref, wgc_ref)]
    vals = [_dot(y_ref[...], wbr_ref[br]) for br, y_ref in enumerate((ya_ref, yb_ref, yc_ref))]
    z = None
    for gt, vl in zip(gates, vals):
        t = _sigmoid(gt) * vl
        z = t if z is None else z + t
    z_ref[...] = z.astype(BF16)


def _merge(h, ya, yb, yc, wgates, wbr4, l, tm, tn):
    b, n, d = h.shape
    gate_spec = lambda br: pl.BlockSpec((None, d, tn), lambda bb, i, j: (l, 0, br * (d // tn) + j))
    yt = pl.BlockSpec((None, tm, D_A), lambda bb, i, j: (bb, i, 0))
    return pl.pallas_call(
        _merge_kernel,
        name="merge_gate",
        grid=(b, n // tm, d // tn),
        in_specs=[
            pl.BlockSpec((None, tm, d), lambda bb, i, j: (bb, i, 0)), yt, yt, yt,
            gate_spec(0), gate_spec(1), gate_spec(2),
            pl.BlockSpec((None, 3, D_A, tn), lambda bb, i, j: (l, 0, 0, j)),
        ],
        out_specs=pl.BlockSpec((None, tm, tn), lambda bb, i, j: (bb, i, j)),
        out_shape=jax.ShapeDtypeStruct((b, n, d), BF16),
        compiler_params=_cparams("parallel", "parallel", "parallel"),
    )(h, ya, yb, yc, wgates, wgates, wgates, wbr4)


def _resid_kernel(x_ref, a_ref, gt_ref, w_ref, *rest, with_norm):
    if with_norm:
        g_ref, sh_ref, sc_ref, o_ref, h_ref = rest
    else:
        (o_ref,) = rest
    tm = x_ref.shape[0]
    nhalf = 2 if tm % (2 * HALO) == 0 else 1
    rh = tm // nhalf

    def finish(c, acc):
        rows = slice(c * rh, (c + 1) * rh)
        xn = x_ref[rows, :] + gt_ref[...] * acc
        if with_norm:
            h_ref[rows, :] = _norm_mod(xn, g_ref[...], sh_ref[...], sc_ref[...]).astype(BF16)
        o_ref[rows, :] = xn

    accs = [_dot(a_ref[0:rh, :], w_ref[...])]
    for c in range(1, nhalf):
        accs.append(_dot(a_ref[c * rh:(c + 1) * rh, :], w_ref[...]))
        finish(c - 1, accs[c - 1])
    finish(nhalf - 1, accs[-1])


def _resid_mm(x, a, mods, l, row, j_gate, w3, norm, tm):
    b, n, d = x.shape
    kdim = a.shape[-1]
    xt = pl.BlockSpec((None, tm, d), lambda bb, i: (bb, i, 0))
    in_specs = [xt, pl.BlockSpec((None, tm, kdim), lambda bb, i: (bb, i, 0)), _mod_spec(d, l, j_gate, row),
                _layer_spec((kdim, d), l)]
    args = [x, a, mods, w3]
    out_specs, out_shape = [xt], [jax.ShapeDtypeStruct((b, n, d), F32)]
    if norm is not None:
        g3, ln, j_shift = norm
        in_specs += [_layer_spec((1, d), ln), _mod_spec(d, ln, j_shift, row), _mod_spec(d, ln, j_shift + 1, row)]
        args += [g3, mods, mods]
        out_specs.append(xt)
        out_shape.append(jax.ShapeDtypeStruct((b, n, d), BF16))
    res = pl.pallas_call(
        functools.partial(_resid_kernel, with_norm=norm is not None),
        name="resid_mm",
        grid=(b, n // tm),
        in_specs=in_specs,
        out_specs=out_specs,
        out_shape=out_shape,
        compiler_params=_cparams("parallel", "parallel"),
    )(*args)
    return (res[0], res[1]) if norm is not None else (res[0], None)


def _ffn_up_kernel(hp_ref, h_ref, hn_ref, wup_ref, cw_ref, cb_ref, o_ref, hs_ref, ug_ref, uv_ref, *, tm, tf):
    i = pl.program_id(1)
    last = pl.num_programs(1) - 1
    pad = FFN_CONV // 2
    n_ext = tm + 2 * HALO
    nf = D_FF // tf
    hs_ref[0:HALO] = jnp.where(i > 0, hp_ref[...], jnp.zeros_like(hp_ref))
    hs_ref[HALO:HALO + tm] = h_ref[...]
    hs_ref[HALO + tm:n_ext] = jnp.where(i < last, hn_ref[...], jnp.zeros_like(hn_ref))
    hs = hs_ref[...]

    def project(j):
        slot = j % 2
        ug_ref[slot] = _dot(hs, wup_ref[:, j * tf:(j + 1) * tf])
        uv_ref[slot] = _dot(hs, wup_ref[:, D_FF + j * tf:D_FF + (j + 1) * tf])

    def conv(j):
        slot = j % 2
        ug = ug_ref[slot]
        uv = uv_ref[slot]
        gc = slice(j * tf, (j + 1) * tf)
        vc = slice(D_FF + j * tf, D_FF + (j + 1) * tf)
        cg = cb_ref[:, gc]
        cv = cb_ref[:, vc]
        for kk in range(FFN_CONV):
            sh = (pad - kk) % n_ext
            ugk = ug if sh == 0 else pltpu.roll(ug, sh, axis=0)
            uvk = uv if sh == 0 else pltpu.roll(uv, sh, axis=0)
            cg = cg + ugk[HALO:HALO + tm] * cw_ref[kk:kk + 1, gc]
            cv = cv + uvk[HALO:HALO + tm] * cw_ref[kk:kk + 1, vc]
        o_ref[:, gc] = (_silu(cg) * cv).astype(BF16)

    project(0)
    for j in range(1, nf):
        project(j)
        conv(j - 1)
    conv(nf - 1)


def _ffn_up(h, wup3, cw3, cb3, l, tm, tf):
    b, n, d = h.shape
    prev, nxt = _halo_specs(tm, n, d)
    const = lambda shape: _layer_spec(shape, l, pipeline_mode=pl.Buffered(1))
    return pl.pallas_call(
        functools.partial(_ffn_up_kernel, tm=tm, tf=tf),
        name="ffn_up",
        grid=(b, n // tm),
        in_specs=[
            prev, pl.BlockSpec((None, tm, d), lambda bb, i: (bb, i, 0)), nxt,
            const((d, 2 * D_FF)), const((FFN_CONV, 2 * D_FF)), const((1, 2 * D_FF)),
        ],
        out_specs=pl.BlockSpec((None, tm, D_FF), lambda bb, i: (bb, i, 0)),
        out_shape=jax.ShapeDtypeStruct((b, n, D_FF), BF16),
        scratch_shapes=[pltpu.VMEM((tm + 2 * HALO, d), BF16),
                        pltpu.VMEM((2, tm + 2 * HALO, tf), F32), pltpu.VMEM((2, tm + 2 * HALO, tf), F32)],
        compiler_params=_cparams("parallel", "parallel"),
    )(h, h, h, wup3, cw3, cb3)


def _rope_tables(n_tok):
    t = jnp.arange(n_tok)
    row = (t // GRID_W).astype(F32)
    col = (t % GRID_W).astype(F32)
    n_freq = DK_C // 4
    inv = jnp.power(ROPE_BASE, -jnp.arange(n_freq, dtype=F32) / n_freq)
    ar = row[:, None] * inv
    ac = col[:, None] * inv
    cos = jnp.concatenate([jnp.cos(ar), jnp.cos(ar), jnp.cos(ac), jnp.cos(ac)], axis=-1)
    sin = jnp.concatenate([-jnp.sin(ar), jnp.sin(ar), -jnp.sin(ac), jnp.sin(ac)], axis=-1)
    return cos, sin


def _block_ones(n, blk):
    idx = np.arange(n) // blk
    return jnp.asarray((idx[:, None] == idx[None, :]).astype(np.float32), dtype=BF16)


def kernel(x, c, ctx, c_ctx, ada_w, ada_b, norm1_g, norm2_g, w_in, conv_a_w, conv_a_b, ln_a_g, ln_a_b, qn_g, kn_g,
           rpb, conv_c_w, a_log, dt_bias, onorm_g, w_branch, w_out, ffn_up, ffn_conv_w, ffn_conv_b, ffn_down):
    batch, n_lat, d = x.shape
    n_ctx = ctx.shape[1]
    depth = ada_w.shape[0]

    cc = jnp.zeros((8, d), F32).at[:batch].set(c).at[batch].set(c_ctx)
    mods = _ada(cc, ada_w, ada_b).reshape(depth, 8, 6, 1, d)
    ctx_row = batch

    rope_c, rope_s = _rope_tables(n_lat)
    ones_c = jnp.ones((n_ctx, DK_C), F32)
    zeros_c = jnp.zeros((n_ctx, DK_C), F32)
    gm64 = _block_ones(D_B, DH_B)
    gm128 = _block_ones(D_CQK, DK_C)
    masks = jnp.asarray(_chunk_masks(GDN_SUB))
    blk_masks = _chunk_masks(GDN_BLOCK)
    cums = jnp.asarray(np.stack([blk_masks[0].T, blk_masks[2].T]), dtype=BF16)
    s_zero = jnp.zeros((batch, H_C, DK_C, DV_C), F32)

    off_g = 2 * D_A + 3 * D_B + 3 * D_CQK + D_CV + 4 * H_C
    w_bf = w_in.astype(BF16)
    wgates = w_bf[:, :, off_g:]
    wbr4 = w_branch.astype(BF16)
    wo3 = w_out.astype(BF16)
    wup3 = ffn_up.astype(BF16)
    wdn3 = ffn_down.astype(BF16)
    g1_3 = norm1_g.reshape(depth, 1, d)
    g2_3 = norm2_g.reshape(depth, 1, d)
    gq3 = (jnp.tile(qn_g, (1, H_B)) * (DH_B ** -0.5 * LOG2E)).reshape(depth, 1, D_B)
    gk3 = jnp.tile(kn_g, (1, H_B)).reshape(depth, 1, D_B)
    gbp3 = jnp.zeros((depth, 8, DK_C), F32)
    gbp3 = gbp3.at[:, 0, :2 * H_C].set(a_log.reshape(depth, 2 * H_C)).at[:, 1, :2 * H_C].set(
        dt_bias.reshape(depth, 2 * H_C)).at[:, 2, :2 * H_C].set(1.0).at[:, 3, 2 * H_C:4 * H_C].set(1.0)
    bias5 = _na_bias_table(rpb)
    cb_a3 = conv_a_b.reshape(depth, 1, D_A)
    lg_a3 = ln_a_g.reshape(depth, 1, D_A)
    lb_a3 = ln_a_b.reshape(depth, 1, D_A)
    cbf3 = ffn_conv_b.reshape(depth, 1, 2 * D_FF)
    gain3 = onorm_g.reshape(depth, 1, DV_C)

    tl = _tiles(n_lat)
    tc = _tiles(n_ctx)
    x_lat, x_ctx = x, ctx
    h_lat = _normmod(x_lat, g1_3, mods, 0, None, 0, tl["norm"])
    h_ctx = _normmod(x_ctx, g1_3, mods, 0, ctx_row, 0, tc["norm"])
    for l in range(depth):
        ctx_out = l < depth - 1
        next_norm = (g1_3, l + 1, 0) if ctx_out else None

        qb_c, kb_c, vb_c = _proj_b(h_ctx, w_bf, gq3, gk3, gm64, l, tc["proj"])
        qc_c, kc_c, vc_c, og_c, gb_c = _proj_c(h_ctx, w_bf, conv_c_w, ones_c, zeros_c, gm128, gbp3, l,
                                               tc["proj"], False)
        yc_c, sf_c, sb_c = _gdn(qc_c, kc_c, vc_c, gb_c, og_c, s_zero, s_zero, masks, cums, gain3, l)

        ya_l = _conv_a(_proj_a(h_lat, w_bf, l, tl["proj"]), conv_a_w, cb_a3, lg_a3, lb_a3, l, tl["conv_a"])
        qb_l, kb_l, vb_l = _proj_b(h_lat, w_bf, gq3, gk3, gm64, l, tl["proj"])
        yb_l = _na_attention(qb_l, kb_l, vb_l, kb_c, vb_c, bias5, l, NA_ROWS_PER_STEP)
        qc_l, kc_l, vc_l, og_l, gb_l = _proj_c(h_lat, w_bf, conv_c_w, rope_c, rope_s, gm128, gbp3, l,
                                               tl["proj"], True)
        yc_l, _, _ = _gdn(qc_l, kc_l, vc_l, gb_l, og_l, sf_c, sb_c, masks, cums, gain3, l)
        z_l = _merge(h_lat, ya_l, yb_l, yc_l, wgates, wbr4, l, tl["merge"], MERGE_TN)
        x_lat, h2_l = _resid_mm(x_lat, z_l, mods, l, None, 2, wo3, (g2_3, l, 3), tl["resid"])
        act_l = _ffn_up(h2_l, wup3, ffn_conv_w, cbf3, l, tl["ffn"], FFN_TF)
        x_lat, h_lat = _resid_mm(x_lat, act_l, mods, l, None, 5, wdn3, next_norm, tl["resid"])

        if ctx_out:
            ya_c = _conv_a(_proj_a(h_ctx, w_bf, l, tc["proj"]), conv_a_w, cb_a3, lg_a3, lb_a3, l, tc["conv_a"])
            yb_c = _ctx_attention(qb_c, kb_c, vb_c)
            z_c = _merge(h_ctx, ya_c, yb_c, yc_c, wgates, wbr4, l, tc["merge"], MERGE_TN)
            x_ctx, h2_c = _resid_mm(x_ctx, z_c, mods, l, ctx_row, 2, wo3, (g2_3, l, 3), tc["resid"])
            act_c = _ffn_up(h2_c, wup3, ffn_conv_w, cbf3, l, tc["ffn"], FFN_TF)
            x_ctx, h_ctx = _resid_mm(x_ctx, act_c, mods, l, ctx_row, 5, wdn3, next_norm, tc["resid"])
    return x_lat
```

```python
import functools
import math

import numpy as np
import jax
import jax.numpy as jnp
from jax import lax
from jax.experimental import pallas as pl
from jax.experimental.pallas import tpu as pltpu

F32 = jnp.float32
BF16 = jnp.bfloat16

D_MODEL = 1024
GRID_W = 64
EPS = 1e-6
NEG_INF = -1e30
LOG2E = math.log2(math.e)
D_A = 512
CONV_A = 31
H_B = 8
DH_B = 64
D_B = H_B * DH_B
NA_ROWS = 8
NA_COLS = 16
H_C = 4
DK_C = 128
DV_C = 128
D_CQK = H_C * DK_C
D_CV = H_C * DV_C
SHORT_CONV = 4
CHUNK = 64
ROPE_BASE = 10000.0
D_FF = 2816
FFN_CONV = 3

VMEM_LIMIT_BYTES = 48 * 1024 * 1024
HALO = 16
SUBLANES = 8
GDN_BLOCK = 256
GDN_SUB = 128
NA_ROWS_PER_STEP = 64
HEADS_PER_STEP = 2
MERGE_TN = 1024
FFN_TF = 256


def _tiles(n_tok):
    cap = lambda t: min(t, n_tok)
    return {"norm": cap(1024), "proj": cap(1024), "conv_a": cap(512), "merge": cap(1024), "resid": cap(1024),
            "ffn": cap(1024)}


def _cparams(*sem):
    return pltpu.CompilerParams(dimension_semantics=sem, vmem_limit_bytes=VMEM_LIMIT_BYTES)


def _dot(a, b):
    return jnp.dot(a, b, preferred_element_type=F32)


def _dot_nt(a, b):
    return lax.dot_general(a, b, (((1,), (1,)), ((), ())), preferred_element_type=F32)


def _split_bf16(x, n):
    parts = []
    r = x
    for idx in range(n):
        p = r.astype(BF16)
        parts.append(p)
        if idx + 1 < n:
            r = r - p.astype(F32)
    return parts


def _dot_exact_rhs(a, b_bf16, n):
    out = None
    for p in _split_bf16(a, n):
        t = _dot(p, b_bf16)
        out = t if out is None else out + t
    return out


def _dot_exact_lhs(a_bf16, b, n):
    out = None
    for p in _split_bf16(b, n):
        t = _dot(a_bf16, p)
        out = t if out is None else out + t
    return out


def _mm(a, b, passes=1):
    if passes == 1:
        return _dot(a.astype(BF16), b.astype(BF16))
    a_hi, a_lo = _split_bf16(a, 2)
    b_hi, b_lo = _split_bf16(b, 2)
    return _dot(a_hi, b_hi) + (_dot(a_lo, b_hi) + _dot(a_hi, b_lo))


def _sigmoid(x):
    return 1.0 / (1.0 + jnp.exp(-x))


def _silu(x):
    return x * _sigmoid(x)


def _softplus(x):
    return jnp.maximum(x, 0.0) + jnp.log(1.0 + jnp.exp(-jnp.abs(x)))


def _norm_mod(x, g, shift, scale):
    ms = jnp.mean(x * x, axis=-1, keepdims=True)
    y = x * lax.rsqrt(ms + EPS) * g
    return y * (1.0 + scale) + shift


def _layer_spec(shape, l, **kw):
    return pl.BlockSpec((None,) + tuple(shape), lambda *_: (l,) + (0,) * len(shape), **kw)


def _wcols_spec(rows, width, l, col_block):
    return pl.BlockSpec((None, rows, width), lambda *_: (l, 0, col_block))


def _mod_spec(d, l, j, row):
    if row is None:
        return pl.BlockSpec((None, None, None, 1, d), lambda bb, *_: (l, bb, j, 0, 0))
    return pl.BlockSpec((None, None, None, 1, d), lambda *_: (l, row, j, 0, 0))


def _halo_specs(tm, n_tok, width):
    per = tm // HALO
    nblk = n_tok // HALO
    prev = pl.BlockSpec((None, HALO, width), lambda b, i, *_: (b, jnp.maximum(i * per - 1, 0), 0))
    nxt = pl.BlockSpec((None, HALO, width), lambda b, i, *_: (b, jnp.minimum((i + 1) * per, nblk - 1), 0))
    return prev, nxt


def _ada_kernel(c_ref, w_ref, b_ref, o_ref):
    s = _silu(c_ref[...])
    o_ref[...] = _mm(s, w_ref[...], 3) + b_ref[...]


def _ada(cc, ada_w, ada_b):
    depth = ada_w.shape[0]
    d = D_MODEL
    return pl.pallas_call(
        _ada_kernel,
        name="ada_mod",
        grid=(depth, 6),
        in_specs=[
            pl.BlockSpec((8, d), lambda l, j: (0, 0)),
            pl.BlockSpec((None, d, d), lambda l, j: (l, 0, j)),
            pl.BlockSpec((None, 1, d), lambda l, j: (l, 0, j)),
        ],
        out_specs=pl.BlockSpec((None, 8, d), lambda l, j: (l, 0, j)),
        out_shape=jax.ShapeDtypeStruct((depth, 8, 6 * d), F32),
        compiler_params=_cparams("arbitrary", "arbitrary"),
    )(cc, ada_w, ada_b.reshape(depth, 1, 6 * d))


def _normmod_kernel(x_ref, g_ref, sh_ref, sc_ref, o_ref):
    o_ref[...] = _norm_mod(x_ref[...], g_ref[...], sh_ref[...], sc_ref[...]).astype(BF16)


def _normmod(x, g3, mods, l, row, j_shift, tm):
    b, n, d = x.shape
    tile = pl.BlockSpec((None, tm, d), lambda bb, i: (bb, i, 0))
    return pl.pallas_call(
        _normmod_kernel,
        name="norm_mod",
        grid=(b, n // tm),
        in_specs=[tile, _layer_spec((1, d), l), _mod_spec(d, l, j_shift, row), _mod_spec(d, l, j_shift + 1, row)],
        out_specs=tile,
        out_shape=jax.ShapeDtypeStruct((b, n, d), BF16),
        compiler_params=_cparams("parallel", "parallel"),
    )(x, g3, mods, mods)


def _proj_a_kernel(h_ref, wa_ref, wg_ref, o_ref):
    h = h_ref[...]
    wc = D_A // 2

    def project(c):
        cols = slice(c * wc, (c + 1) * wc)
        return _dot(h, wa_ref[:, cols]), _dot(h, wg_ref[:, cols])

    first = project(0)
    second = project(1)
    o_ref[:, 0:wc] = first[0] * _sigmoid(first[1])
    o_ref[:, wc:2 * wc] = second[0] * _sigmoid(second[1])


def _proj_a(h, w_bf, l, tm):
    b, n, d = h.shape
    return pl.pallas_call(
        _proj_a_kernel,
        name="proj_a",
        grid=(b, n // tm),
        in_specs=[
            pl.BlockSpec((None, tm, d), lambda bb, i: (bb, i, 0)),
            _wcols_spec(d, D_A, l, 0), _wcols_spec(d, D_A, l, 1),
        ],
        out_specs=pl.BlockSpec((None, tm, D_A), lambda bb, i: (bb, i, 0)),
        out_shape=jax.ShapeDtypeStruct((b, n, D_A), F32),
        compiler_params=_cparams("parallel", "parallel"),
    )(h, w_bf, w_bf)


def _conv_a_kernel(yp_ref, y_ref, yn_ref, cw_ref, cb_ref, lg_ref, lb_ref, o_ref, ext_ref, *, tm):
    i = pl.program_id(1)
    last = pl.num_programs(1) - 1
    ext_ref[0, 0:HALO] = yp_ref[...] * (i > 0).astype(F32)
    ext_ref[0, HALO:HALO + tm] = y_ref[...]
    ext_ref[0, HALO + tm:2 * HALO + tm] = yn_ref[...] * (i < last).astype(F32)
    n_ext = tm + 2 * HALO
    ext0 = ext_ref[0]
    for r in range(1, SUBLANES):
        ext_ref[r] = pltpu.roll(ext0, n_ext - r, axis=0)
    pad = CONV_A // 2
    acc = jnp.zeros((tm, D_A), F32) + cb_ref[...]
    for k in range(CONV_A):
        off = HALO - pad + k
        r = off % SUBLANES
        acc = acc + ext_ref[r, pl.ds(off - r, tm), :] * cw_ref[k:k + 1, :]
    mu = jnp.mean(acc, axis=-1, keepdims=True)
    cen = acc - mu
    var = jnp.mean(cen * cen, axis=-1, keepdims=True)
    y = cen * lax.rsqrt(var + EPS) * lg_ref[...] + lb_ref[...]
    o_ref[...] = _silu(y).astype(BF16)


def _conv_a(y, cw3, cb3, lg3, lb3, l, tm):
    b, n, c = y.shape
    prev, nxt = _halo_specs(tm, n, c)
    vec = _layer_spec((1, c), l)
    return pl.pallas_call(
        functools.partial(_conv_a_kernel, tm=tm),
        name="conv_a",
        grid=(b, n // tm),
        in_specs=[prev, pl.BlockSpec((None, tm, c), lambda bb, i: (bb, i, 0)), nxt,
                  _layer_spec((CONV_A, c), l), vec, vec, vec],
        out_specs=pl.BlockSpec((None, tm, c), lambda bb, i: (bb, i, 0)),
        out_shape=jax.ShapeDtypeStruct((b, n, c), BF16),
        scratch_shapes=[pltpu.VMEM((SUBLANES, tm + 2 * HALO, c), F32)],
        compiler_params=_cparams("parallel", "parallel"),
    )(y, y, y, cw3, cb3, lg3, lb3)


def _proj_b_kernel(h_ref, wq_ref, wk_ref, wv_ref, gq_ref, gk_ref, gm_ref, q_ref, k_ref, v_ref):
    h = h_ref[...]
    for w_ref, gain_ref, o_ref in ((wq_ref, gq_ref, q_ref), (wk_ref, gk_ref, k_ref), (wv_ref, None, v_ref)):
        acc = _dot(h, w_ref[...])
        if gain_ref is not None:
            ss = _dot((acc * acc).astype(BF16), gm_ref[...])
            acc = acc * lax.rsqrt(ss * (1.0 / DH_B) + EPS) * gain_ref[...]
        for hh in range(H_B):
            o_ref[hh] = acc[:, hh * DH_B:(hh + 1) * DH_B].astype(BF16)


def _proj_b(h, w_bf, gq3, gk3, gmat, l, tm):
    b, n, d = h.shape
    vec = _layer_spec((1, D_B), l)
    first = 2 * D_A // D_B
    head_out = pl.BlockSpec((None, H_B, tm, DH_B), lambda bb, i: (bb, 0, i, 0))
    shp = jax.ShapeDtypeStruct((b, H_B, n, DH_B), BF16)
    return pl.pallas_call(
        _proj_b_kernel,
        name="proj_b",
        grid=(b, n // tm),
        in_specs=[
            pl.BlockSpec((None, tm, d), lambda bb, i: (bb, i, 0)),
            _wcols_spec(d, D_B, l, first), _wcols_spec(d, D_B, l, first + 1), _wcols_spec(d, D_B, l, first + 2),
            vec, vec,
            pl.BlockSpec((D_B, D_B), lambda bb, i: (0, 0)),
        ],
        out_specs=[head_out, head_out, head_out],
        out_shape=[shp, shp, shp],
        compiler_params=_cparams("parallel", "parallel"),
    )(h, w_bf, w_bf, w_bf, gq3, gk3, gmat)


def _na_kernel(q_ref, k_ref, v_ref, kc_ref, vc_ref, bias_ref, o_ref, *, rb, rows):
    i = pl.program_id(2)
    n_loc = NA_ROWS * GRID_W
    rws = [i * rb + rr for rr in range(rb)]
    starts = [jnp.clip(r - NA_ROWS // 2, 0, rows - NA_ROWS) for r in rws]
    variants = [st - r + NA_ROWS - 1 for st, r in zip(starts, rws)]
    tok0 = [pl.multiple_of(st * GRID_W, GRID_W) for st in starts]
    probs = [(hh, rr) for hh in range(HEADS_PER_STEP) for rr in range(rb)]
    qs = [q_ref[hh, rr * GRID_W:(rr + 1) * GRID_W, :] for hh, rr in probs]
    s = [_dot_nt(q, k_ref[hh, pl.ds(tok0[rr], n_loc), :]) for q, (hh, rr) in zip(qs, probs)]
    sc = [_dot_nt(q, kc_ref[hh]) for q, (hh, rr) in zip(qs, probs)]
    bias = {(hh, rr): jnp.concatenate([bias_ref[hh, variants[rr] + kk] for kk in range(0, NA_ROWS, 2)], axis=1)
            for hh, rr in probs}
    s = [a + bias[pr] for a, pr in zip(s, probs)]
    m = [jnp.maximum(jnp.max(a, axis=-1, keepdims=True), jnp.max(b, axis=-1, keepdims=True)) for a, b in zip(s, sc)]
    p = [jnp.exp2(a - mm) for a, mm in zip(s, m)]
    pc = [jnp.exp2(b - mm) for b, mm in zip(sc, m)]
    l = [jnp.sum(a, axis=-1, keepdims=True) + jnp.sum(b, axis=-1, keepdims=True) for a, b in zip(p, pc)]
    o = [_dot(a.astype(BF16), v_ref[hh, pl.ds(tok0[rr], n_loc), :]) + _dot(b.astype(BF16), vc_ref[hh])
         for a, b, (hh, rr) in zip(p, pc, probs)]
    o = [a / b for a, b in zip(o, l)]
    for rr in range(rb):
        o_ref[rr * GRID_W:(rr + 1) * GRID_W, :] = jnp.concatenate(
            [o[probs.index((hh, rr))] for hh in range(HEADS_PER_STEP)], axis=1).astype(BF16)


def _na_attention(q, k, v, kc, vc, bias, l, rb):
    b, h, t, dh = q.shape
    n_ctx = kc.shape[2]
    rows = t // GRID_W
    hs = HEADS_PER_STEP
    full = pl.BlockSpec((None, hs, t, dh), lambda bb, hp, i: (bb, hp, 0, 0))
    cfull = pl.BlockSpec((None, hs, n_ctx, dh), lambda bb, hp, i: (bb, hp, 0, 0))
    return pl.pallas_call(
        functools.partial(_na_kernel, rb=rb, rows=rows),
        name="na_attn",
        grid=(b, h // hs, rows // rb),
        in_specs=[pl.BlockSpec((None, hs, rb * GRID_W, dh), lambda bb, hp, i: (bb, hp, i, 0)),
                  full, full, cfull, cfull,
                  pl.BlockSpec((None, hs, 2 * NA_ROWS - 2, GRID_W, 2 * GRID_W),
                               lambda bb, hp, i: (l, hp, 0, 0, 0))],
        out_specs=pl.BlockSpec((None, rb * GRID_W, hs * dh), lambda bb, hp, i: (bb, i, hp)),
        out_shape=jax.ShapeDtypeStruct((b, t, h * dh), BF16),
        compiler_params=_cparams("parallel", "parallel", "arbitrary"),
    )(q, k, v, kc, vc, bias)


def _ctx_attn_kernel(q_ref, k_ref, v_ref, o_ref):
    outs = []
    for hh in range(HEADS_PER_STEP):
        s = _dot_nt(q_ref[hh], k_ref[hh])
        m = jnp.max(s, axis=-1, keepdims=True)
        p = jnp.exp2(s - m)
        l = jnp.sum(p, axis=-1, keepdims=True)
        outs.append(_dot(p.astype(BF16), v_ref[hh]) / l)
    o_ref[...] = jnp.concatenate(outs, axis=1).astype(BF16)


def _ctx_attention(q, k, v):
    b, h, n, dh = q.shape
    hs = HEADS_PER_STEP
    full = pl.BlockSpec((None, hs, n, dh), lambda bb, hp: (bb, hp, 0, 0))
    return pl.pallas_call(
        _ctx_attn_kernel,
        name="ctx_attn",
        grid=(b, h // hs),
        in_specs=[full, full, full],
        out_specs=pl.BlockSpec((None, n, hs * dh), lambda bb, hp: (bb, 0, hp)),
        out_shape=jax.ShapeDtypeStruct((b, n, h * dh), BF16),
        compiler_params=_cparams("parallel", "parallel"),
    )(q, k, v)


def _na_bias_table(rpb):
    cidx = np.arange(GRID_W)
    dc = np.clip(cidx[None, :] - cidx[:, None] + NA_COLS - 1, 0, 2 * NA_COLS - 2)
    onehot = (dc[None] == np.arange(2 * NA_COLS - 1)[:, None, None]).astype(np.float32)
    cs = np.clip(cidx - NA_COLS // 2, 0, GRID_W - NA_COLS)
    col_ok = (cidx[None, :] >= cs[:, None]) & (cidx[None, :] < cs[:, None] + NA_COLS)
    toep = jnp.einsum("lhrd,dqk->lhrqk", rpb, onehot, precision=lax.Precision.HIGHEST)
    toep = jnp.where(col_ok, toep * LOG2E, NEG_INF)
    return jnp.concatenate([toep[:, :, :-1], toep[:, :, 1:]], axis=-1)


def _proj_c_kernel(hp_ref, h_ref, hn_ref, wq_ref, wk_ref, wv_ref, wo_ref, wdb_ref, cw_ref, rc_ref, rs_ref,
                   gm_ref, gbp_ref, q_ref, k_ref, v_ref, og_ref, gb_ref, hs_ref, p_ref, *, tm, use_rope):
    i = pl.program_id(1)
    last = pl.num_programs(1) - 1
    pad_l = SHORT_CONV // 2
    n_ext = tm + 2 * HALO
    hs_ref[0:HALO] = jnp.where(i > 0, hp_ref[...], jnp.zeros_like(hp_ref))
    hs_ref[HALO:HALO + tm] = h_ref[...]
    hs_ref[HALO + tm:n_ext] = jnp.where(i < last, hn_ref[...], jnp.zeros_like(hn_ref))
    hs = hs_ref[...]
    hc = h_ref[...]
    lane = lax.broadcasted_iota(jnp.int32, (tm, DK_C), 1)
    first_half = (lane & (DK_C // 2 - 1)) < DK_C // 4
    sections = ((wq_ref, q_ref), (wk_ref, k_ref), (wv_ref, v_ref))

    def project(sec):
        p_ref[sec % 2] = _dot(hs, sections[sec][0][...])

    def finish(sec):
        o_ref = sections[sec][1]
        cols = slice(sec * D_CQK, (sec + 1) * D_CQK)
        pv = p_ref[sec % 2]
        y = None
        for kk in range(SHORT_CONV):
            sh = (pad_l - kk) % n_ext
            pk = pv if sh == 0 else pltpu.roll(pv, sh, axis=0)
            t = pk[HALO:HALO + tm] * cw_ref[kk:kk + 1, cols]
            y = t if y is None else y + t
        y = _silu(y)
        if sec < 2:
            ss = _dot((y * y).astype(BF16), gm_ref[...])
            y = y * lax.rsqrt(ss + EPS)
            if use_rope:
                heads = []
                for hh in range(H_C):
                    yh = y[:, hh * DK_C:(hh + 1) * DK_C]
                    swapped = jnp.where(first_half, pltpu.roll(yh, DK_C - DK_C // 4, axis=1),
                                        pltpu.roll(yh, DK_C // 4, axis=1))
                    heads.append(yh * rc_ref[...] + swapped * rs_ref[...])
                y = jnp.concatenate(heads, axis=1)
            if sec == 0:
                y = y * (DK_C ** -0.5)
        o_ref[...] = y

    project(0)
    for sec in range(1, len(sections)):
        project(sec)
        finish(sec - 1)
    og_ref[...] = _dot(hc, wo_ref[...])
    finish(len(sections) - 1)
    db = _dot(hc, wdb_ref[...])
    gval = -jnp.exp(gbp_ref[0:1, :]) * _softplus(db + gbp_ref[1:2, :])
    gb_ref[...] = gbp_ref[2:3, :] * gval + gbp_ref[3:4, :] * _sigmoid(db)


def _proj_c(h, w_bf, cw3, rc, rs, gmat, gbp3, l, tm, use_rope):
    b, n, d = h.shape
    first = (2 * D_A + 3 * D_B) // D_CQK
    db_block = (2 * D_A + 3 * D_B + 3 * D_CQK + D_CV) // DK_C
    prev, nxt = _halo_specs(tm, n, d)
    tile512 = pl.BlockSpec((None, tm, D_CQK), lambda bb, i: (bb, i, 0))
    shp = jax.ShapeDtypeStruct((b, n, D_CQK), F32)
    rope_spec = pl.BlockSpec((tm, DK_C), lambda bb, i: (i, 0))
    return pl.pallas_call(
        functools.partial(_proj_c_kernel, tm=tm, use_rope=use_rope),
        name="proj_c",
        grid=(b, n // tm),
        in_specs=[
            prev, pl.BlockSpec((None, tm, d), lambda bb, i: (bb, i, 0)), nxt,
            _wcols_spec(d, D_CQK, l, first), _wcols_spec(d, D_CQK, l, first + 1),
            _wcols_spec(d, D_CQK, l, first + 2), _wcols_spec(d, D_CQK, l, first + 3),
            _wcols_spec(d, DK_C, l, db_block),
            _layer_spec((SHORT_CONV, 3 * D_CQK), l),
            rope_spec, rope_spec,
            pl.BlockSpec((D_CQK, D_CQK), lambda bb, i: (0, 0)),
            _layer_spec((8, DK_C), l),
        ],
        out_specs=[tile512] * 4 + [pl.BlockSpec((None, tm, DK_C), lambda bb, i: (bb, i, 0))],
        out_shape=[shp] * 4 + [jax.ShapeDtypeStruct((b, n, DK_C), F32)],
        scratch_shapes=[pltpu.VMEM((tm + 2 * HALO, d), BF16), pltpu.VMEM((2, tm + 2 * HALO, D_CQK), F32)],
        compiler_params=_cparams("parallel", "parallel"),
    )(h, h, h, w_bf, w_bf, w_bf, w_bf, w_bf, cw3, rc, rs, gmat, gbp3)


def _chunk_masks(n):
    r = np.arange(n)[:, None]
    c = np.arange(n)[None, :]
    same = lambda s: (r // s) == (c // s)
    chunk = same(CHUNK)
    ms = [chunk & (r >= c), chunk & (r > c), chunk & (r <= c), chunk & (r < c), same(8),
          same(16) & ~same(8), same(32) & ~same(16), same(64) & ~same(32)]
    return np.stack(ms).astype(np.float32)


def _tri_inv_all(lmats, eye, m_ref):
    m8 = m_ref[4]
    n0 = [-(l * m8) for l in lmats]
    n2 = [_mm(a, a) for a in n0]
    n4 = [_mm(a, a) for a in n2]
    n3 = [_mm(a, b) for a, b in zip(n0, n2)]
    t1 = [eye + a + b + c for a, b, c in zip(n0, n2, n3)]
    t1n4 = [_mm(a, b) for a, b in zip(t1, n4)]
    t = [a + b for a, b in zip(t1, t1n4)]
    for lvl in (5, 6, 7):
        off = m_ref[lvl]
        lt = [_mm(l * off, a) for l, a in zip(lmats, t)]
        tlt = [_mm(a, b) for a, b in zip(t, lt)]
        t = [a - b for a, b in zip(t, tlt)]
    return t


def _gdn_step(io, m_ref, cum_ref):
    nsub = GDN_BLOCK // GDN_SUB
    csub = GDN_SUB // CHUNK
    nch = nsub * csub
    r16, cols = [], []
    for d in range(2):
        g_t = io[d][3][...].T[0:4 * H_C]
        csum = _dot_exact_rhs(g_t, cum_ref[d], 3)
        rid = lax.broadcasted_iota(jnp.int32, g_t.shape, 0)
        r = jnp.where(rid // H_C == d, csum, g_t)
        r16.append(r)
        cols.append(jnp.concatenate([r, jnp.zeros((DK_C - 4 * H_C, GDN_BLOCK), F32)], axis=0).T)
    probs = [(d, h, sb) for d in range(2) for h in range(H_C) for sb in range(nsub)]
    incl = [m_ref[0], m_ref[2]]
    strict = [m_ref[1], m_ref[3]]
    eye = incl[0] - strict[0]

    def tile(ref, h, sb):
        return ref[sb * GDN_SUB:(sb + 1) * GDN_SUB, h * DK_C:(h + 1) * DK_C]

    q = [tile(io[d][0], h, sb) for d, h, sb in probs]
    k = [tile(io[d][1], h, sb) for d, h, sb in probs]
    v = [tile(io[d][2], h, sb) for d, h, sb in probs]
    gcol, bcol, decay = [], [], []
    for d, h, sb in probs:
        rows = slice(sb * GDN_SUB, (sb + 1) * GDN_SUB)
        c_g = d * H_C + h
        c_b = 2 * H_C + c_g
        gc = jnp.broadcast_to(cols[d][rows, c_g:c_g + 1], (GDN_SUB, DK_C))
        gcol.append(gc)
        bcol.append(jnp.broadcast_to(cols[d][rows, c_b:c_b + 1], (GDN_SUB, DK_C)))
        decay.append(jnp.exp(jnp.where(incl[d] > 0.5, gc - r16[d][c_g:c_g + 1, rows], NEG_INF)))
    k16 = [a.astype(BF16) for a in k]
    qkk = [_dot_nt(jnp.concatenate([a.astype(BF16), b], axis=0), b) for a, b in zip(q, k16)]
    qk = [a[:GDN_SUB] for a in qkk]
    kk = [a[GDN_SUB:] for a in qkk]
    lmats = [a * b * (c * strict[p[0]]) for a, b, c, p in zip(kk, bcol, decay, probs)]
    amat = [(a * c).astype(BF16) for a, c in zip(qk, decay)]
    tinv = _tri_inv_all(lmats, eye, m_ref)
    eg = [jnp.exp(a) for a in gcol]
    rhs = [jnp.concatenate([vv * b, kx * (b * e)], axis=1).astype(BF16) for vv, kx, b, e in zip(v, k, bcol, eg)]
    sol = [_dot(t.astype(BF16), r) for t, r in zip(tinv, rhs)]
    qd = [a * e for a, e in zip(q, eg)]
    gtot, kd_t = [], []
    for (d, h, sb), kx, gc in zip(probs, k, gcol):
        parts, gts = [], []
        for c in range(csub):
            end = c * CHUNK + (CHUNK - 1 if d == 0 else 0)
            gt = gc[end:end + 1, :]
            gts.append(gt)
            sl = slice(c * CHUNK, (c + 1) * CHUNK)
            parts.append(kx[sl] * jnp.exp(gt - gc[sl]))
        gtot.append(gts)
        kd_t.append(jnp.concatenate(parts, axis=0).T.astype(BF16))
    pidx = {p: n for n, p in enumerate(probs)}
    chains = [(d, h) for d in range(2) for h in range(H_C)]
    state = [io[d][4][h] for d, h in chains]
    vnew = [[None] * nch for _ in chains]
    ointer = [[None] * nch for _ in chains]
    zeros = jnp.zeros((CHUNK, DV_C), BF16)
    for step in range(nch):
        pos = [(step if d == 0 else nch - 1 - step) for d, _ in chains]
        loc = [(pidx[(d, h, n // csub)], n % csub) for (d, h), n in zip(chains, pos)]
        sls = [slice(c * CHUNK, (c + 1) * CHUNK) for _, c in loc]
        wq = [jnp.concatenate([sol[p][sl, DV_C:], qd[p][sl]], axis=0).astype(BF16) for (p, _), sl in zip(loc, sls)]
        r = [_dot(a, s.astype(BF16)) for a, s in zip(wq, state)]
        vn = [(sol[p][sl, :DV_C] - rr[0:CHUNK]).astype(BF16) for (p, _), sl, rr in zip(loc, sls, r)]
        vpad = [jnp.concatenate([x if m == c else zeros for m in range(csub)], axis=0) for x, (_, c) in zip(vn, loc)]
        upd = [_dot(kd_t[p], x) for (p, _), x in zip(loc, vpad)]
        state = [s * jnp.exp(gtot[p][c]) + u for s, (p, c), u in zip(state, loc, upd)]
        for ci, n in enumerate(pos):
            vnew[ci][n] = vn[ci]
            ointer[ci][n] = r[ci][CHUNK:2 * CHUNK]
    for (d, h), s in zip(chains, state):
        io[d][4][h] = s
    out = [[None] * H_C for _ in range(2)]
    for ci, (d, h) in enumerate(chains):
        parts = []
        for sb in range(nsub):
            p = pidx[(d, h, sb)]
            vn_sb = jnp.concatenate(vnew[ci][sb * csub:(sb + 1) * csub], axis=0)
            parts.append(jnp.concatenate(ointer[ci][sb * csub:(sb + 1) * csub], axis=0) + _dot(amat[p], vn_sb))
        out[d][h] = jnp.concatenate(parts, axis=0)
    return out


def _gdn_finalize(o_heads, gate_ref, gain):
    ys = []
    for h, o in enumerate(o_heads):
        ms = jnp.mean(o * o, axis=-1, keepdims=True)
        ys.append(o * lax.rsqrt(ms + EPS) * gain * _silu(gate_ref[:, h * DV_C:(h + 1) * DV_C]))
    return jnp.concatenate(ys, axis=1).astype(BF16)


def _gdn_kernel(qf_ref, kf_ref, vf_ref, gf_ref, ogf_ref, qb_ref, kb_ref, vb_ref, gb_ref, ogb_ref,
                s0f_ref, s0b_ref, m_ref, cum_ref, gain_ref,
                y_ref, sfo_ref, sbo_ref, oacc_ref, sf_ref, sb_ref, *, nb):
    i = pl.program_id(1)

    @pl.when(i == 0)
    def _():
        sf_ref[...] = s0f_ref[...]
        sb_ref[...] = s0b_ref[...]

    of, ob = _gdn_step(((qf_ref, kf_ref, vf_ref, gf_ref, sf_ref), (qb_ref, kb_ref, vb_ref, gb_ref, sb_ref)),
                       m_ref, cum_ref)
    rows_f = pl.ds(pl.multiple_of(i * GDN_BLOCK, GDN_BLOCK), GDN_BLOCK)
    rows_b = pl.ds(pl.multiple_of((nb - 1 - i) * GDN_BLOCK, GDN_BLOCK), GDN_BLOCK)
    gain = gain_ref[...]

    def split(x):
        return [x[:, h * DV_C:(h + 1) * DV_C] for h in range(H_C)]

    @pl.when(2 * i < nb - 1)
    def _():
        oacc_ref[rows_f, :] = jnp.concatenate(of, axis=1)
        oacc_ref[rows_b, :] = jnp.concatenate(ob, axis=1)

    if nb % 2 == 1:
        @pl.when(2 * i == nb - 1)
        def _():
            y_ref[rows_f, :] = _gdn_finalize([a + b for a, b in zip(of, ob)], ogf_ref, gain)

    @pl.when(2 * i > nb - 1)
    def _():
        y_ref[rows_f, :] = _gdn_finalize([a + b for a, b in zip(split(oacc_ref[rows_f, :]), of)], ogf_ref, gain)
        y_ref[rows_b, :] = _gdn_finalize([a + b for a, b in zip(split(oacc_ref[rows_b, :]), ob)], ogb_ref, gain)

    @pl.when(i == nb - 1)
    def _():
        sfo_ref[...] = sf_ref[...]
        sbo_ref[...] = sb_ref[...]


def _gdn(q, k, v, gb, og, s0f, s0b, masks, cums, gain3, l):
    b, n, _ = q.shape
    nb = n // GDN_BLOCK
    fwd = pl.BlockSpec((None, GDN_BLOCK, D_CQK), lambda bb, i: (bb, i, 0))
    bwd = pl.BlockSpec((None, GDN_BLOCK, D_CQK), lambda bb, i: (bb, nb - 1 - i, 0))
    gfwd = pl.BlockSpec((None, GDN_BLOCK, DK_C), lambda bb, i: (bb, i, 0))
    gbwd = pl.BlockSpec((None, GDN_BLOCK, DK_C), lambda bb, i: (bb, nb - 1 - i, 0))
    st = pl.BlockSpec((None, H_C, DK_C, DV_C), lambda bb, i: (bb, 0, 0, 0))
    st_shape = jax.ShapeDtypeStruct((b, H_C, DK_C, DV_C), F32)
    return pl.pallas_call(
        functools.partial(_gdn_kernel, nb=nb),
        name="gdn_scan",
        grid=(b, nb),
        in_specs=[fwd, fwd, fwd, gfwd, fwd, bwd, bwd, bwd, gbwd, bwd,
                  st, st,
                  pl.BlockSpec((8, GDN_SUB, GDN_SUB), lambda bb, i: (0, 0, 0)),
                  pl.BlockSpec((2, GDN_BLOCK, GDN_BLOCK), lambda bb, i: (0, 0, 0)),
                  _layer_spec((1, DV_C), l)],
        out_specs=[pl.BlockSpec((None, n, D_CV), lambda bb, i: (bb, 0, 0)), st, st],
        out_shape=[jax.ShapeDtypeStruct((b, n, D_CV), BF16), st_shape, st_shape],
        scratch_shapes=[pltpu.VMEM((n, D_CV), F32), pltpu.VMEM((H_C, DK_C, DV_C), F32),
                        pltpu.VMEM((H_C, DK_C, DV_C), F32)],
        compiler_params=_cparams("parallel", "arbitrary"),
    )(q, k, v, gb, og, q, k, v, gb, og, s0f, s0b, masks, cums, gain3)


def _merge_kernel(h_ref, ya_ref, yb_ref, yc_ref, wga_ref, wgb_ref, wgc_ref, wbr_ref, z_ref):
    h = h_ref[...]
    gates = [_dot(h, w_ref[...]) for w_ref in (wga_ref, wgb_ref, wgc_ref)]
    vals = [_dot(y_ref[...], wbr_ref[br]) for br, y_ref in enumerate((ya_ref, yb_ref, yc_ref))]
    z = None
    for gt, vl in zip(gates, vals):
        t = _sigmoid(gt) * vl
        z = t if z is None else z + t
    z_ref[...] = z.astype(BF16)


def _merge(h, ya, yb, yc, wgates, wbr4, l, tm, tn):
    b, n, d = h.shape
    gate_spec = lambda br: pl.BlockSpec((None, d, tn), lambda bb, i, j: (l, 0, br * (d // tn) + j))
    yt = pl.BlockSpec((None, tm, D_A), lambda bb, i, j: (bb, i, 0))
    return pl.pallas_call(
        _merge_kernel,
        name="merge_gate",
        grid=(b, n // tm, d // tn),
        in_specs=[
            pl.BlockSpec((None, tm, d), lambda bb, i, j: (bb, i, 0)), yt, yt, yt,
            gate_spec(0), gate_spec(1), gate_spec(2),
            pl.BlockSpec((None, 3, D_A, tn), lambda bb, i, j: (l, 0, 0, j)),
        ],
        out_specs=pl.BlockSpec((None, tm, tn), lambda bb, i, j: (bb, i, j)),
        out_shape=jax.ShapeDtypeStruct((b, n, d), BF16),
        compiler_params=_cparams("parallel", "parallel", "parallel"),
    )(h, ya, yb, yc, wgates, wgates, wgates, wbr4)


def _resid_kernel(x_ref, a_ref, gt_ref, w_ref, *rest, with_norm):
    if with_norm:
        g_ref, sh_ref, sc_ref, o_ref, h_ref = rest
    else:
        (o_ref,) = rest
    tm = x_ref.shape[0]
    nhalf = 2 if tm % (2 * HALO) == 0 else 1
    rh = tm // nhalf

    def finish(c, acc):
        rows = slice(c * rh, (c + 1) * rh)
        xn = x_ref[rows, :] + gt_ref[...] * acc
        if with_norm:
            h_ref[rows, :] = _norm_mod(xn, g_ref[...], sh_ref[...], sc_ref[...]).astype(BF16)
        o_ref[rows, :] = xn

    accs = [_dot(a_ref[0:rh, :], w_ref[...])]
    for c in range(1, nhalf):
        accs.append(_dot(a_ref[c * rh:(c + 1) * rh, :], w_ref[...]))
        finish(c - 1, accs[c - 1])
    finish(nhalf - 1, accs[-1])


def _resid_mm(x, a, mods, l, row, j_gate, w3, norm, tm):
    b, n, d = x.shape
    kdim = a.shape[-1]
    xt = pl.BlockSpec((None, tm, d), lambda bb, i: (bb, i, 0))
    in_specs = [xt, pl.BlockSpec((None, tm, kdim), lambda bb, i: (bb, i, 0)), _mod_spec(d, l, j_gate, row),
                _layer_spec((kdim, d), l)]
    args = [x, a, mods, w3]
    out_specs, out_shape = [xt], [jax.ShapeDtypeStruct((b, n, d), F32)]
    if norm is not None:
        g3, ln, j_shift = norm
        in_specs += [_layer_spec((1, d), ln), _mod_spec(d, ln, j_shift, row), _mod_spec(d, ln, j_shift + 1, row)]
        args += [g3, mods, mods]
        out_specs.append(xt)
        out_shape.append(jax.ShapeDtypeStruct((b, n, d), BF16))
    res = pl.pallas_call(
        functools.partial(_resid_kernel, with_norm=norm is not None),
        name="resid_mm",
        grid=(b, n // tm),
        in_specs=in_specs,
        out_specs=out_specs,
        out_shape=out_shape,
        compiler_params=_cparams("parallel", "parallel"),
    )(*args)
    return (res[0], res[1]) if norm is not None else (res[0], None)


def _ffn_up_kernel(hp_ref, h_ref, hn_ref, wup_ref, cw_ref, cb_ref, o_ref, hs_ref, ug_ref, uv_ref, *, tm, tf):
    i = pl.program_id(1)
    last = pl.num_programs(1) - 1
    pad = FFN_CONV // 2
    n_ext = tm + 2 * HALO
    nf = D_FF // tf
    hs_ref[0:HALO] = jnp.where(i > 0, hp_ref[...], jnp.zeros_like(hp_ref))
    hs_ref[HALO:HALO + tm] = h_ref[...]
    hs_ref[HALO + tm:n_ext] = jnp.where(i < last, hn_ref[...], jnp.zeros_like(hn_ref))
    hs = hs_ref[...]

    def project(j):
        slot = j % 2
        ug_ref[slot] = _dot(hs, wup_ref[:, j * tf:(j + 1) * tf])
        uv_ref[slot] = _dot(hs, wup_ref[:, D_FF + j * tf:D_FF + (j + 1) * tf])

    def conv(j):
        slot = j % 2
        ug = ug_ref[slot]
        uv = uv_ref[slot]
        gc = slice(j * tf, (j + 1) * tf)
        vc = slice(D_FF + j * tf, D_FF + (j + 1) * tf)
        cg = cb_ref[:, gc]
        cv = cb_ref[:, vc]
        for kk in range(FFN_CONV):
            sh = (pad - kk) % n_ext
            ugk = ug if sh == 0 else pltpu.roll(ug, sh, axis=0)
            uvk = uv if sh == 0 else pltpu.roll(uv, sh, axis=0)
            cg = cg + ugk[HALO:HALO + tm] * cw_ref[kk:kk + 1, gc]
            cv = cv + uvk[HALO:HALO + tm] * cw_ref[kk:kk + 1, vc]
        o_ref[:, gc] = (_silu(cg) * cv).astype(BF16)

    project(0)
    for j in range(1, nf):
        project(j)
        conv(j - 1)
    conv(nf - 1)


def _ffn_up(h, wup3, cw3, cb3, l, tm, tf):
    b, n, d = h.shape
    prev, nxt = _halo_specs(tm, n, d)
    const = lambda shape: _layer_spec(shape, l, pipeline_mode=pl.Buffered(1))
    return pl.pallas_call(
        functools.partial(_ffn_up_kernel, tm=tm, tf=tf),
        name="ffn_up",
        grid=(b, n // tm),
        in_specs=[
            prev, pl.BlockSpec((None, tm, d), lambda bb, i: (bb, i, 0)), nxt,
            const((d, 2 * D_FF)), const((FFN_CONV, 2 * D_FF)), const((1, 2 * D_FF)),
        ],
        out_specs=pl.BlockSpec((None, tm, D_FF), lambda bb, i: (bb, i, 0)),
        out_shape=jax.ShapeDtypeStruct((b, n, D_FF), BF16),
        scratch_shapes=[pltpu.VMEM((tm + 2 * HALO, d), BF16),
                        pltpu.VMEM((2, tm + 2 * HALO, tf), F32), pltpu.VMEM((2, tm + 2 * HALO, tf), F32)],
        compiler_params=_cparams("parallel", "parallel"),
    )(h, h, h, wup3, cw3, cb3)


def _rope_tables(n_tok):
    t = jnp.arange(n_tok)
    row = (t // GRID_W).astype(F32)
    col = (t % GRID_W).astype(F32)
    n_freq = DK_C // 4
    inv = jnp.power(ROPE_BASE, -jnp.arange(n_freq, dtype=F32) / n_freq)
    ar = row[:, None] * inv
    ac = col[:, None] * inv
    cos = jnp.concatenate([jnp.cos(ar), jnp.cos(ar), jnp.cos(ac), jnp.cos(ac)], axis=-1)
    sin = jnp.concatenate([-jnp.sin(ar), jnp.sin(ar), -jnp.sin(ac), jnp.sin(ac)], axis=-1)
    return cos, sin


def _block_ones(n, blk):
    idx = np.arange(n) // blk
    return jnp.asarray((idx[:, None] == idx[None, :]).astype(np.float32), dtype=BF16)


def kernel(x, c, ctx, c_ctx, ada_w, ada_b, norm1_g, norm2_g, w_in, conv_a_w, conv_a_b, ln_a_g, ln_a_b, qn_g, kn_g,
           rpb, conv_c_w, a_log, dt_bias, onorm_g, w_branch, w_out, ffn_up, ffn_conv_w, ffn_conv_b, ffn_down):
    batch, n_lat, d = x.shape
    n_ctx = ctx.shape[1]
    depth = ada_w.shape[0]

    cc = jnp.zeros((8, d), F32).at[:batch].set(c).at[batch].set(c_ctx)
    mods = _ada(cc, ada_w, ada_b).reshape(depth, 8, 6, 1, d)
    ctx_row = batch

    rope_c, rope_s = _rope_tables(n_lat)
    ones_c = jnp.ones((n_ctx, DK_C), F32)
    zeros_c = jnp.zeros((n_ctx, DK_C), F32)
    gm64 = _block_ones(D_B, DH_B)
    gm128 = _block_ones(D_CQK, DK_C)
    masks = jnp.asarray(_chunk_masks(GDN_SUB))
    blk_masks = _chunk_masks(GDN_BLOCK)
    cums = jnp.asarray(np.stack([blk_masks[0].T, blk_masks[2].T]), dtype=BF16)
    s_zero = jnp.zeros((batch, H_C, DK_C, DV_C), F32)

    off_g = 2 * D_A + 3 * D_B + 3 * D_CQK + D_CV + 4 * H_C
    w_bf = w_in.astype(BF16)
    wgates = w_bf[:, :, off_g:]
    wbr4 = w_branch.astype(BF16)
    wo3 = w_out.astype(BF16)
    wup3 = ffn_up.astype(BF16)
    wdn3 = ffn_down.astype(BF16)
    g1_3 = norm1_g.reshape(depth, 1, d)
    g2_3 = norm2_g.reshape(depth, 1, d)
    gq3 = (jnp.tile(qn_g, (1, H_B)) * (DH_B ** -0.5 * LOG2E)).reshape(depth, 1, D_B)
    gk3 = jnp.tile(kn_g, (1, H_B)).reshape(depth, 1, D_B)
    gbp3 = jnp.zeros((depth, 8, DK_C), F32)
    gbp3 = gbp3.at[:, 0, :2 * H_C].set(a_log.reshape(depth, 2 * H_C)).at[:, 1, :2 * H_C].set(
        dt_bias.reshape(depth, 2 * H_C)).at[:, 2, :2 * H_C].set(1.0).at[:, 3, 2 * H_C:4 * H_C].set(1.0)
    bias5 = _na_bias_table(rpb)
    cb_a3 = conv_a_b.reshape(depth, 1, D_A)
    lg_a3 = ln_a_g.reshape(depth, 1, D_A)
    lb_a3 = ln_a_b.reshape(depth, 1, D_A)
    cbf3 = ffn_conv_b.reshape(depth, 1, 2 * D_FF)
    gain3 = onorm_g.reshape(depth, 1, DV_C)

    tl = _tiles(n_lat)
    tc = _tiles(n_ctx)
    x_lat, x_ctx = x, ctx
    h_lat = _normmod(x_lat, g1_3, mods, 0, None, 0, tl["norm"])
    h_ctx = _normmod(x_ctx, g1_3, mods, 0, ctx_row, 0, tc["norm"])
    for l in range(depth):
        ctx_out = l < depth - 1
        next_norm = (g1_3, l + 1, 0) if ctx_out else None

        qb_c, kb_c, vb_c = _proj_b(h_ctx, w_bf, gq3, gk3, gm64, l, tc["proj"])
        qc_c, kc_c, vc_c, og_c, gb_c = _proj_c(h_ctx, w_bf, conv_c_w, ones_c, zeros_c, gm128, gbp3, l,
                                               tc["proj"], False)
        yc_c, sf_c, sb_c = _gdn(qc_c, kc_c, vc_c, gb_c, og_c, s_zero, s_zero, masks, cums, gain3, l)

        ya_l = _conv_a(_proj_a(h_lat, w_bf, l, tl["proj"]), conv_a_w, cb_a3, lg_a3, lb_a3, l, tl["conv_a"])
        qb_l, kb_l, vb_l = _proj_b(h_lat, w_bf, gq3, gk3, gm64, l, tl["proj"])
        yb_l = _na_attention(qb_l, kb_l, vb_l, kb_c, vb_c, bias5, l, NA_ROWS_PER_STEP)
        qc_l, kc_l, vc_l, og_l, gb_l = _proj_c(h_lat, w_bf, conv_c_w, rope_c, rope_s, gm128, gbp3, l,
                                               tl["proj"], True)
        yc_l, _, _ = _gdn(qc_l, kc_l, vc_l, gb_l, og_l, sf_c, sb_c, masks, cums, gain3, l)
        z_l = _merge(h_lat, ya_l, yb_l, yc_l, wgates, wbr4, l, tl["merge"], MERGE_TN)
        x_lat, h2_l = _resid_mm(x_lat, z_l, mods, l, None, 2, wo3, (g2_3, l, 3), tl["resid"])
        act_l = _ffn_up(h2_l, wup3, ffn_conv_w, cbf3, l, tl["ffn"], FFN_TF)
        x_lat, h_lat = _resid_mm(x_lat, act_l, mods, l, None, 5, wdn3, next_norm, tl["resid"])

        if ctx_out:
            ya_c = _conv_a(_proj_a(h_ctx, w_bf, l, tc["proj"]), conv_a_w, cb_a3, lg_a3, lb_a3, l, tc["conv_a"])
            yb_c = _ctx_attention(qb_c, kb_c, vb_c)
            z_c = _merge(h_ctx, ya_c, yb_c, yc_c, wgates, wbr4, l, tc["merge"], MERGE_TN)
            x_ctx, h2_c = _resid_mm(x_ctx, z_c, mods, l, ctx_row, 2, wo3, (g2_3, l, 3), tc["resid"])
            act_c = _ffn_up(h2_c, wup3, ffn_conv_w, cbf3, l, tc["ffn"], FFN_TF)
            x_ctx, h_ctx = _resid_mm(x_ctx, act_c, mods, l, ctx_row, 5, wdn3, next_norm, tc["resid"])
    return x_lat
```
